```python
import math
import jax, jax.numpy as jnp
from jax import lax
import numpy as np

D_MODEL = 1024
BATCH = 4
SEQ = 4096
DEPTH = 4
DEC_BATCH = 128
DEC_SEQ = 8
PAST_LEN = 2048
PAGE_SIZE = 128

PLE_DIM = 256
D_FF = 4 * D_MODEL
RMS_EPS = 1e-6
ROPE_THETA = 10000.0
N_HEADS = 16
HEAD_DIM = D_MODEL // N_HEADS
KV_HEADS = 4
Q_PER_KV = N_HEADS // KV_HEADS
Q_DIM = N_HEADS * HEAD_DIM
KV_DIM = KV_HEADS * HEAD_DIM
N_BRANCH = 3
CMP_BLOCK = 32
CMP_HIDDEN = 4 * HEAD_DIM
SEL_BLOCK = 64
CMP_PER_SEL = SEL_BLOCK // CMP_BLOCK
TOP_N = 16
WINDOW = 512
Q_BLOCK = 128
NSA_IN = Q_DIM + 6 * KV_DIM + N_BRANCH * N_HEADS
FORCED_CUR = 3e4
FORCED_FIRST = 2e4
HG_EXPAND = 128
HG_HEADS = D_MODEL // HG_EXPAND
HG_DK = HG_EXPAND
HG_DV = D_MODEL // HG_HEADS
HG_CHUNK = 64
HG_IN = 4 * D_MODEL
N_ATTN_LAYERS = (DEPTH + 1) // 2
N_REC_LAYERS = DEPTH // 2

kernel_name = 'nsa_hgrn2_hybrid_step'


def rmsnorm(x, g):
    xf = x.astype(jnp.float32)
    y = xf * lax.rsqrt(jnp.mean(xf * xf, axis=-1, keepdims=True) + RMS_EPS)
    return (y * g.astype(jnp.float32)).astype(x.dtype)


def rope(x, pos):
    half = HEAD_DIM // 2
    inv = ROPE_THETA ** (-jnp.arange(half, dtype=jnp.float32) / half)
    ang = pos.astype(jnp.float32)[:, None] * inv[None, :]
    cos = jnp.cos(ang)[:, None, :]
    sin = jnp.sin(ang)[:, None, :]
    xf = x.astype(jnp.float32)
    x1, x2 = xf[..., :half], xf[..., half:]
    return jnp.concatenate([x1 * cos - x2 * sin, x1 * sin + x2 * cos], axis=-1).astype(x.dtype)


def masked_softmax(s, mask):
    s = jnp.where(mask, s.astype(jnp.float32), -1e30)
    m = jnp.max(s, axis=-1, keepdims=True)
    e = jnp.where(mask, jnp.exp(s - m), 0.0)
    return e / jnp.maximum(jnp.sum(e, axis=-1, keepdims=True), 1e-30)


def compress(kb, pe, w1, w2):
    h = jnp.einsum('nlgd,ldh->ngh', kb + pe[None, :, None, :], w1)
    return jnp.einsum('ngh,hd->ngd', jax.nn.gelu(h), w2)


def nsa_one(q, kv, kw, vw, gates, q_pos0, win_pos0, pe, w1, w2):
    Tq = q.shape[0]
    L = kv.shape[0]
    L_pad = -(-L // SEL_BLOCK) * SEL_BLOCK
    kv = jnp.pad(kv, ((0, L_pad - L), (0, 0), (0, 0), (0, 0)))
    n_cmp = L_pad // CMP_BLOCK
    n_sel = L_pad // SEL_BLOCK
    kc = compress(kv[:, 0].reshape(n_cmp, CMP_BLOCK, KV_HEADS, HEAD_DIM), pe[0], w1[0], w2[0])
    vc = compress(kv[:, 1].reshape(n_cmp, CMP_BLOCK, KV_HEADS, HEAD_DIM), pe[1], w1[1], w2[1])
    ks = kv[:, 2].reshape(n_sel, SEL_BLOCK, KV_HEADS, HEAD_DIM).transpose(2, 0, 1, 3)
    vs = kv[:, 3].reshape(n_sel, SEL_BLOCK, KV_HEADS, HEAD_DIM).transpose(2, 0, 1, 3)
    kw_pad = jnp.pad(kw, ((WINDOW, 0), (0, 0), (0, 0)))
    vw_pad = jnp.pad(vw, ((WINDOW, 0), (0, 0), (0, 0)))
    n_top = min(TOP_N, n_sel)
    qb = math.gcd(Tq, Q_BLOCK)
    scale = HEAD_DIM ** -0.5
    cmp_end = (jnp.arange(n_cmp, dtype=jnp.int32) + 1) * CMP_BLOCK - 1
    blk = jnp.arange(n_sel, dtype=jnp.int32)
    g_idx = jnp.arange(KV_HEADS)[None, :, None]

    def block(i):
        s = i * qb
        qi = lax.dynamic_slice_in_dim(q, s, qb, 0).reshape(qb, KV_HEADS, Q_PER_KV, HEAD_DIM)
        gi = lax.dynamic_slice_in_dim(gates, s, qb, 0).reshape(qb, KV_HEADS, Q_PER_KV, N_BRANCH)
        t = q_pos0 + s + jnp.arange(qb, dtype=jnp.int32)
        sc = jnp.einsum('qgrd,ngd->qgrn', qi, kc) * scale
        pc = masked_softmax(sc, (cmp_end[None, :] <= t[:, None])[:, None, None, :])
        o_c = jnp.einsum('qgrn,ngd->qgrd', pc.astype(vc.dtype), vc)
        imp = pc.sum(2).reshape(qb, KV_HEADS, n_sel, CMP_PER_SEL).sum(-1)
        cur = (t // SEL_BLOCK)[:, None, None]
        imp = jnp.where(blk[None, None, :] > cur, -1.0, imp)
        imp = jnp.where(blk[None, None, :] == 0, FORCED_FIRST, imp)
        imp = jnp.where(blk[None, None, :] == cur, FORCED_CUR, imp)
        _, idx = lax.top_k(imp, n_top)
        ksel = ks[g_idx, idx].reshape(qb, KV_HEADS, n_top * SEL_BLOCK, HEAD_DIM)
        vsel = vs[g_idx, idx].reshape(qb, KV_HEADS, n_top * SEL_BLOCK, HEAD_DIM)
        kpos = (idx[..., None] * SEL_BLOCK + jnp.arange(SEL_BLOCK, dtype=jnp.int32)).reshape(qb, KV_HEADS, 1, -1)
        ss = jnp.einsum('qgrd,qgmd->qgrm', qi, ksel) * scale
        ps = masked_softmax(ss, kpos <= t[:, None, None, None])
        o_s = jnp.einsum('qgrm,qgmd->qgrd', ps.astype(vsel.dtype), vsel)
        j0 = q_pos0 + s - win_pos0
        kwi = lax.dynamic_slice_in_dim(kw_pad, j0, WINDOW + qb, 0)
        vwi = lax.dynamic_slice_in_dim(vw_pad, j0, WINDOW + qb, 0)
        wpos = q_pos0 + s - WINDOW + jnp.arange(WINDOW + qb, dtype=jnp.int32)
        mw = (wpos[None, :] >= win_pos0) & (wpos[None, :] <= t[:, None]) & (wpos[None, :] > t[:, None] - WINDOW)
        sw = jnp.einsum('qgrd,kgd->qgrk', qi, kwi) * scale
        pw = masked_softmax(sw, mw[:, None, None, :])
        o_w = jnp.einsum('qgrk,kgd->qgrd', pw.astype(vwi.dtype), vwi)
        return gi[..., 0:1] * o_c + gi[..., 1:2] * o_s + gi[..., 2:3] * o_w

    o = lax.map(block, jnp.arange(Tq // qb))
    return o.reshape(Tq, N_HEADS, HEAD_DIM)


def nsa_project(h, pos, w_in):
    B, T, _ = h.shape
    q, kv, gl = jnp.split(h @ w_in, [Q_DIM, Q_DIM + 6 * KV_DIM], axis=-1)
    q = rope(q.reshape(B, T, N_HEADS, HEAD_DIM), pos)
    kv = kv.reshape(B, T, 3, 2, KV_HEADS, HEAD_DIM)
    k = rope(kv[:, :, :, 0].reshape(B, T, 3 * KV_HEADS, HEAD_DIM), pos).reshape(B, T, 3, KV_HEADS, HEAD_DIM)
    kv = jnp.stack([k, kv[:, :, :, 1]], axis=3).reshape(B, T, 6, KV_HEADS, HEAD_DIM)
    gates = jax.nn.sigmoid(gl.astype(jnp.float32)).astype(h.dtype).reshape(B, T, N_HEADS, N_BRANCH)
    return q, kv, gates


def nsa_prompt(h, pos, w_in, pe, w1, w2, w_out):
    B, T, _ = h.shape
    q, kv, gates = nsa_project(h, pos, w_in)
    rows, wrows = kv[:, :, :4], kv[:, :, 4:]

    def one(args):
        q_i, r_i, w_i, g_i = args
        return nsa_one(q_i, r_i, w_i[:, 0], w_i[:, 1], g_i, 0, 0, pe, w1, w2)

    o = lax.map(one, (q, rows, wrows, gates))
    y = o.reshape(B, T, D_MODEL) @ w_out
    return y, rows, wrows[:, T - min(WINDOW, T):]


def nsa_sample(h, pos, pool, win_buf, page_table, past_len, w_in, pe, w1, w2, w_out):
    B, T, _ = h.shape
    q, kv, gates = nsa_project(h, pos, w_in)
    rows, wrows = kv[:, :, :4], kv[:, :, 4:]
    n_buf = win_buf.shape[1]

    def one(args):
        q_i, r_i, w_i, g_i, pt_i, buf_i = args
        past = pool[pt_i].reshape(past_len, 4, KV_HEADS, HEAD_DIM)
        kv_i = jnp.concatenate([past, r_i], axis=0)
        wk = jnp.concatenate([buf_i, w_i], axis=0)
        return nsa_one(q_i, kv_i, wk[:, 0], wk[:, 1], g_i, past_len, past_len - n_buf, pe, w1, w2)

    o = lax.map(one, (q, rows, wrows, gates, page_table, win_buf))
    y = o.reshape(B, T, D_MODEL) @ w_out
    new_buf = jnp.concatenate([win_buf, wrows], axis=1)[:, T:]
    return y, rows, new_buf


def hgrn_scan(q, k, v, log_f, S0):
    B, T = q.shape[:2]
    C = math.gcd(T, HG_CHUNK)
    n = T // C

    def resh(a):
        return a.reshape(B, n, C, HG_HEADS, a.shape[-1]).transpose(1, 0, 3, 2, 4)

    tri = jnp.tril(jnp.ones((C, C), dtype=bool))[:, :, None]

    def step(S, xs):
        qc, kc, vc, fc = xs
        b = jnp.cumsum(fc, axis=2)
        o_inter = jnp.einsum('bhtk,bhkv->bhtv', qc * jnp.exp(b), S)
        diff = b[:, :, :, None, :] - b[:, :, None, :, :]
        dec = jnp.where(tri, jnp.exp(jnp.where(tri, diff, 0.0)), 0.0)
        A = jnp.einsum('bhtk,bhsk,bhtsk->bhts', qc, kc, dec)
        o_intra = jnp.einsum('bhts,bhsv->bhtv', A, vc)
        b_last = b[:, :, -1]
        S_new = jnp.exp(b_last)[..., None] * S + jnp.einsum('bhsk,bhsv->bhkv', kc * jnp.exp(b_last[:, :, None] - b), vc)
        return S_new, o_inter + o_intra

    S, o = lax.scan(step, S0, (resh(q), resh(k), resh(v), resh(log_f)))
    return o.transpose(1, 0, 3, 2, 4).reshape(B, T, HG_HEADS, HG_DV), S


def hgrn_mixer(h, S0, w_in, lb, norm_g, w_out):
    B, T, _ = h.shape
    f32 = jnp.float32
    q, f, i, g = jnp.split((h @ w_in).astype(f32), 4, axis=-1)
    shp = (B, T, HG_HEADS, HG_DK)
    q = (jax.nn.silu(q) * HG_DK ** -0.5).reshape(shp)
    lbf = lb.astype(f32)
    log_f = jnp.log(lbf + (1.0 - lbf) * jax.nn.sigmoid(f)).reshape(shp)
    k = ((1.0 - lbf) * jax.nn.sigmoid(-f)).reshape(shp)
    v = i.reshape(B, T, HG_HEADS, HG_DV)
    o, S = hgrn_scan(q, k, v, log_f, S0.astype(f32))
    o = rmsnorm(o, norm_g) * jax.nn.silu(g).reshape(B, T, HG_HEADS, HG_DV)
    y = o.reshape(B, T, D_MODEL).astype(h.dtype) @ w_out
    return y, S.astype(S0.dtype)


def post_block(x, p_i, g_mlp, w1, w2, g_ple, w_gate, w_proj):
    hid = jax.nn.relu(rmsnorm(x, g_mlp) @ w1)
    x = x + (hid * hid) @ w2
    gate = jax.nn.sigmoid((rmsnorm(x, g_ple) @ w_gate).astype(jnp.float32)).astype(x.dtype)
    return x + (p_i @ w_proj) * gate


def setup_inputs(seed: int = 0) -> dict:
    key = jax.random.key(seed)
    ks = jax.random.split(key, 32)
    f32 = jnp.float32

    def nrm(k, shape, scale):
        return jax.random.normal(k, shape, f32) * scale

    def gain(k, shape):
        return 1.0 + 0.01 * jax.random.normal(k, shape, f32)

    n_pages = PAST_LEN // PAGE_SIZE
    n_used = DEC_BATCH * n_pages
    n_phys = (n_used * 5 + 3) // 4
    page_table = jax.random.permutation(ks[5], n_phys)[:n_used].astype(jnp.int32).reshape(DEC_BATCH, n_pages)
    win_buf = min(WINDOW, PAST_LEN)
    return {
        'x_prompt': nrm(ks[0], (BATCH, SEQ, D_MODEL), 1.0),
        'x_sample': nrm(ks[1], (DEC_BATCH, DEC_SEQ, D_MODEL), 1.0),
        'cache_nsa_kv': nrm(ks[2], (N_ATTN_LAYERS, n_phys, PAGE_SIZE, 4, KV_HEADS, HEAD_DIM), 1.0),
        'state_nsa_win': nrm(ks[3], (N_ATTN_LAYERS, DEC_BATCH, win_buf, 2, KV_HEADS, HEAD_DIM), 1.0),
        'state_hgrn': nrm(ks[4], (N_REC_LAYERS, DEC_BATCH, HG_HEADS, HG_DK, HG_DV), 0.5),
        'page_table': page_table,
        'p_prompt': nrm(ks[6], (DEPTH, BATCH, SEQ, PLE_DIM), 1.0),
        'p_sample': nrm(ks[7], (DEPTH, DEC_BATCH, DEC_SEQ, PLE_DIM), 1.0),
        'norm_mix': gain(ks[8], (DEPTH, D_MODEL)),
        'norm_mlp': gain(ks[9], (DEPTH, D_MODEL)),
        'norm_ple': gain(ks[10], (DEPTH, D_MODEL)),
        'norm_final': gain(ks[11], (D_MODEL,)),
        'nsa_w_in': nrm(ks[12], (N_ATTN_LAYERS, D_MODEL, NSA_IN), D_MODEL ** -0.5),
        'nsa_cmp_pe': nrm(ks[13], (N_ATTN_LAYERS, 2, CMP_BLOCK, HEAD_DIM), 0.1),
        'nsa_cmp_w1': nrm(ks[14], (N_ATTN_LAYERS, 2, CMP_BLOCK, HEAD_DIM, CMP_HIDDEN), (CMP_BLOCK * HEAD_DIM) ** -0.5),
        'nsa_cmp_w2': nrm(ks[15], (N_ATTN_LAYERS, 2, CMP_HIDDEN, HEAD_DIM), CMP_HIDDEN ** -0.5),
        'nsa_w_out': nrm(ks[16], (N_ATTN_LAYERS, D_MODEL, D_MODEL), D_MODEL ** -0.5),
        'hg_w_in': nrm(ks[17], (N_REC_LAYERS, D_MODEL, HG_IN), D_MODEL ** -0.5),
        'hg_lb_logits': nrm(ks[18], (N_REC_LAYERS, D_MODEL), 0.5),
        'hg_norm': gain(ks[19], (N_REC_LAYERS, HG_DV)),
        'hg_w_out': nrm(ks[20], (N_REC_LAYERS, D_MODEL, D_MODEL), D_MODEL ** -0.5),
        'mlp_w1': nrm(ks[21], (DEPTH, D_MODEL, D_FF), D_MODEL ** -0.5),
        'mlp_w2': nrm(ks[22], (DEPTH, D_FF, D_MODEL), D_FF ** -0.5),
        'ple_w_proj': nrm(ks[23], (DEPTH, PLE_DIM, D_MODEL), PLE_DIM ** -0.5),
        'ple_w_gate': nrm(ks[24], (DEPTH, D_MODEL, D_MODEL), D_MODEL ** -0.5),
    }


def reference(x_prompt, x_sample, cache_nsa_kv, state_nsa_win, state_hgrn, page_table, p_prompt, p_sample,
              norm_mix, norm_mlp, norm_ple, norm_final,
              nsa_w_in, nsa_cmp_pe, nsa_cmp_w1, nsa_cmp_w2, nsa_w_out,
              hg_w_in, hg_lb_logits, hg_norm, hg_w_out,
              mlp_w1, mlp_w2, ple_w_proj, ple_w_gate):
    T_p = x_prompt.shape[1]
    T_s = x_sample.shape[1]
    past_len = page_table.shape[1] * cache_nsa_kv.shape[2]
    pos_p = jnp.arange(T_p, dtype=jnp.int32)
    pos_s = past_len + jnp.arange(T_s, dtype=jnp.int32)
    lb_w = jax.nn.softmax(hg_lb_logits.astype(jnp.float32), axis=0)
    lower_bounds = jnp.cumsum(lb_w, axis=0) - lb_w[0:1]

    xp, xs = x_prompt, x_sample
    kv_p, kv_s, win_p, win_s, st_p, st_s = [], [], [], [], [], []
    for i in range(DEPTH):
        hp = rmsnorm(xp, norm_mix[i])
        hs = rmsnorm(xs, norm_mix[i])
        if i % 2 == 0:
            a = i // 2
            yp, rp, wp = nsa_prompt(hp, pos_p, nsa_w_in[a], nsa_cmp_pe[a], nsa_cmp_w1[a], nsa_cmp_w2[a], nsa_w_out[a])
            ys, rs, ws = nsa_sample(hs, pos_s, cache_nsa_kv[a], state_nsa_win[a], page_table, past_len,
                                    nsa_w_in[a], nsa_cmp_pe[a], nsa_cmp_w1[a], nsa_cmp_w2[a], nsa_w_out[a])
            kv_p.append(rp)
            kv_s.append(rs)
            win_p.append(wp)
            win_s.append(ws)
        else:
            r = i // 2
            S0 = jnp.zeros((xp.shape[0], HG_HEADS, HG_DK, HG_DV), dtype=state_hgrn.dtype)
            yp, sp = hgrn_mixer(hp, S0, hg_w_in[r], lower_bounds[r], hg_norm[r], hg_w_out[r])
            ys, ss = hgrn_mixer(hs, state_hgrn[r], hg_w_in[r], lower_bounds[r], hg_norm[r], hg_w_out[r])
            st_p.append(sp)
            st_s.append(ss)
        xp = post_block(xp + yp, p_prompt[i], norm_mlp[i], mlp_w1[i], mlp_w2[i], norm_ple[i], ple_w_gate[i], ple_w_proj[i])
        xs = post_block(xs + ys, p_sample[i], norm_mlp[i], mlp_w1[i], mlp_w2[i], norm_ple[i], ple_w_gate[i], ple_w_proj[i])
    y_prompt = rmsnorm(xp, norm_final)
    y_sample = rmsnorm(xs, norm_final)
    return (y_prompt, y_sample, jnp.stack(kv_p), jnp.stack(kv_s), jnp.stack(win_p), jnp.stack(win_s),
            jnp.stack(st_p), jnp.stack(st_s))
```

```python
import functools

import jax
import jax.numpy as jnp
from jax import lax
from jax.experimental import pallas as pl
from jax.experimental.pallas import tpu as pltpu

F32 = jnp.float32
BF16 = jnp.bfloat16

RMS_EPS = 1e-6
ROPE_THETA = 10000.0
N_HEADS = 16
HEAD_DIM = 64
KV_HEADS = 4
Q_PER_KV = N_HEADS // KV_HEADS
Q_DIM = N_HEADS * HEAD_DIM
KV_DIM = KV_HEADS * HEAD_DIM
N_BRANCH = 3
CMP_BLOCK = 32
SEL_BLOCK = 64
TOP_N = 16
WINDOW = 512
FORCED_CUR = 3e4
FORCED_FIRST = 2e4
HG_HEADS = 8
HG_DK = 128
HG_DV = 128
HG_CHUNK = 64
HG_SUB = 16
NEG = -1e30

LANES = 128
ROW_TILE = 512
Q_TILE = 128
VMEM_LIMIT = 56 * 1024 * 1024


def _cparams(*sem):
    return pltpu.CompilerParams(dimension_semantics=sem, vmem_limit_bytes=VMEM_LIMIT)


def _full(shape):
    return pl.BlockSpec(shape, lambda *_: (0,) * len(shape))


def _rms(x, g):
    return x * lax.rsqrt(jnp.mean(x * x, axis=-1, keepdims=True) + RMS_EPS) * g


def _dot(a, b):
    return jnp.dot(a, b, preferred_element_type=F32)


def _dot_nt(a, b):
    return lax.dot_general(a, b, (((1,), (1,)), ((), ())), preferred_element_type=F32)


def _rope_table_kernel(inv_ref, invc_ref, cos_ref, sin_ref, cost_ref, sint_ref, *, pos0, period):
    i = pl.program_id(0)
    tm = cos_ref.shape[0]
    half = HEAD_DIM // 2
    row = i * tm + lax.broadcasted_iota(jnp.int32, (tm, LANES), 0)
    ang = (pos0 + row % period).astype(F32) * inv_ref[...]
    lane = lax.broadcasted_iota(jnp.int32, (tm, LANES), 1)
    cos_ref[...] = jnp.cos(ang)
    s = jnp.sin(ang)
    sin_ref[...] = jnp.where(lane % HEAD_DIM < half, -s, s)
    col = i * tm + lax.broadcasted_iota(jnp.int32, (HEAD_DIM, tm), 1)
    ang_t = (pos0 + col % period).astype(F32) * jnp.concatenate([invc_ref[...]] * (tm // LANES), axis=1)
    feat = lax.broadcasted_iota(jnp.int32, (HEAD_DIM, tm), 0)
    cost_ref[...] = jnp.cos(ang_t)
    st = jnp.sin(ang_t)
    sint_ref[...] = jnp.where(feat < half, -st, st)


def _rope_tables(rows, pos0, period):
    half = HEAD_DIM // 2
    inv = ROPE_THETA ** (-jnp.arange(half, dtype=F32) / half)
    inv_row = jnp.tile(inv, LANES // half)[None, :]
    inv_col = jnp.broadcast_to(jnp.tile(inv, HEAD_DIM // half)[:, None], (HEAD_DIM, LANES))
    kern = functools.partial(_rope_table_kernel, pos0=pos0, period=period)
    tm = min(ROW_TILE, rows)
    tok = pl.BlockSpec((tm, LANES), lambda i: (i, 0))
    feat = pl.BlockSpec((HEAD_DIM, tm), lambda i: (0, i))
    return pl.pallas_call(
        kern,
        grid=(rows // tm,),
        in_specs=[_full((1, LANES)), _full((HEAD_DIM, LANES))],
        out_specs=[tok, tok, feat, feat],
        out_shape=[jax.ShapeDtypeStruct((rows, LANES), F32)] * 2 + [jax.ShapeDtypeStruct((HEAD_DIM, rows), F32)] * 2,
        compiler_params=_cparams("parallel"),
        name="rope_tables",
    )(inv_row, inv_col)


def _nsa_inproj_kernel(x_ref, g_ref, cos_ref, sin_ref, *refs, feature_major):
    if feature_major:
        (cost_ref, sint_ref, w_ref, q_ref, gates_ref, cmp_ref,
         rowst_ref, wrowst_ref, ksa_ref, vsa_ref, kwa_ref, vwa_ref) = refs
    else:
        w_ref, q_ref, gates_ref, rows_ref, wrows_ref = refs
    tm = x_ref.shape[0]
    xn = _rms(x_ref[...], g_ref[...]).astype(BF16)
    cos = cos_ref[...]
    sin = sin_ref[...]
    lane = lax.broadcasted_iota(jnp.int32, (tm, LANES), 1)
    first = lane % HEAD_DIM < HEAD_DIM // 2

    def rope(y):
        sw = jnp.where(first, pltpu.roll(y, LANES - HEAD_DIM // 2, 1), pltpu.roll(y, HEAD_DIM // 2, 1))
        return y * cos + sw * sin

    scale = HEAD_DIM ** -0.5
    yq = _dot_nt(xn, w_ref[0:Q_DIM, :])
    for c in range(Q_DIM // LANES):
        cs = slice(c * LANES, (c + 1) * LANES)
        q_ref[:, cs] = (rope(yq[:, cs]) * scale).astype(BF16)
    g0 = Q_DIM + 6 * KV_DIM
    gates_ref[...] = jax.nn.sigmoid(_dot_nt(xn, w_ref[g0:g0 + LANES, :]))

    n_tok_parts = 2 if feature_major else 6
    ykv = _dot_nt(xn, w_ref[Q_DIM:Q_DIM + n_tok_parts * KV_DIM, :])
    for c in range(n_tok_parts * KV_DIM // LANES):
        cs = slice(c * LANES, (c + 1) * LANES)
        part = c * LANES // KV_DIM
        y = rope(ykv[:, cs]) if part % 2 == 0 else ykv[:, cs]
        if feature_major:
            cmp_ref[:, cs] = y
        elif part < 4:
            rows_ref[:, cs] = y
        else:
            wrows_ref[:, c * LANES - 4 * KV_DIM:(c + 1) * LANES - 4 * KV_DIM] = y

    if feature_major:
        i = pl.program_id(1)
        cost = cost_ref[...]
        sint = sint_ref[...]
        yt = _dot_nt(w_ref[Q_DIM:Q_DIM + 6 * KV_DIM, :], xn)
        brow = lax.broadcasted_iota(jnp.int32, (HEAD_DIM, tm), 0)
        bcol = i * tm + lax.broadcasted_iota(jnp.int32, (HEAD_DIM, tm), 1)
        sel_bias = jnp.where(brow == bcol // SEL_BLOCK, NEG, 0.0)
        ones = jnp.ones((HEAD_DIM, tm), F32)
        zeros = jnp.zeros((HEAD_DIM, tm), F32)
        for part in range(6):
            for g in range(KV_HEADS):
                r0 = part * KV_DIM + g * HEAD_DIM
                y = yt[r0:r0 + HEAD_DIM, :]
                if part % 2 == 0:
                    y = y * cost + pltpu.roll(y, HEAD_DIM // 2, 0) * sint
                if part < 4:
                    rowst_ref[r0:r0 + HEAD_DIM, :] = y
                else:
                    wrowst_ref[r0 - 4 * KV_DIM:r0 - 4 * KV_DIM + HEAD_DIM, :] = y
                if part >= 2:
                    ref, fill = {2: (ksa_ref, sel_bias), 3: (vsa_ref, ones), 4: (kwa_ref, zeros), 5: (vwa_ref, ones)}[part]
                    ya = jnp.concatenate([y, fill], axis=0).astype(BF16)
                    for j in range(tm // Q_TILE):
                        ref[g, j] = ya[:, j * Q_TILE:(j + 1) * Q_TILE]


def _nsa_inproj_prompt(x, g, tabs, w, batch, t_seq):
    m, d = x.shape
    tm = min(ROW_TILE, t_seq)
    nt = t_seq // tm
    cos, sin, cost, sint = tabs
    row = lambda cols: pl.BlockSpec((tm, cols), lambda b, i: (b * nt + i, 0))
    tok_tab = pl.BlockSpec((tm, LANES), lambda b, i: (i, 0))
    feat_tab = pl.BlockSpec((HEAD_DIM, tm), lambda b, i: (0, i))
    featmaj = lambda rows: pl.BlockSpec((None, rows, tm), lambda b, i: (b, 0, i))
    tiles = pl.BlockSpec((None, KV_HEADS, tm // Q_TILE, LANES, Q_TILE), lambda b, i: (b, 0, i, 0, 0))
    tile_shape = jax.ShapeDtypeStruct((batch, KV_HEADS, t_seq // Q_TILE, LANES, Q_TILE), BF16)
    return pl.pallas_call(
        functools.partial(_nsa_inproj_kernel, feature_major=True),
        grid=(batch, nt),
        in_specs=[row(d), _full((1, d)), tok_tab, tok_tab, feat_tab, feat_tab, _full(w.shape)],
        out_specs=[row(d), row(LANES), row(2 * KV_DIM), featmaj(4 * KV_DIM), featmaj(2 * KV_DIM)] + [tiles] * 4,
        out_shape=[jax.ShapeDtypeStruct((m, d), BF16), jax.ShapeDtypeStruct((m, LANES), F32),
                   jax.ShapeDtypeStruct((m, 2 * KV_DIM), F32),
                   jax.ShapeDtypeStruct((batch, 4 * KV_DIM, t_seq), F32),
                   jax.ShapeDtypeStruct((batch, 2 * KV_DIM, t_seq), F32)] + [tile_shape] * 4,
        compiler_params=_cparams("parallel", "parallel"),
        name="nsa_inproj_prompt",
    )(x, g, cos, sin, cost, sint, w)


def _nsa_inproj_sample(x, g, tabs, w):
    m, d = x.shape
    tm = min(ROW_TILE, m)
    cos, sin = tabs[0], tabs[1]
    n_tab = cos.shape[0] // tm
    row = lambda cols: pl.BlockSpec((tm, cols), lambda i: (i, 0))
    tab = pl.BlockSpec((tm, LANES), lambda i: (i % n_tab, 0))
    return pl.pallas_call(
        functools.partial(_nsa_inproj_kernel, feature_major=False),
        grid=(m // tm,),
        in_specs=[row(d), _full((1, d)), tab, tab, _full(w.shape)],
        out_specs=[row(d), row(LANES), row(4 * KV_DIM), row(2 * KV_DIM)],
        out_shape=[jax.ShapeDtypeStruct((m, d), BF16), jax.ShapeDtypeStruct((m, LANES), F32),
                   jax.ShapeDtypeStruct((m, 4 * KV_DIM), F32), jax.ShapeDtypeStruct((m, 2 * KV_DIM), F32)],
        compiler_params=_cparams("parallel"),
        name="nsa_inproj_sample",
    )(x, g, cos, sin, w)


def _compress_rows(load, pe_ref, w1_ref, w2t_ref, kv, n):
    lane = lax.broadcasted_iota(jnp.int32, (n, LANES), 1)
    lo = lane < HEAD_DIM
    acc = jnp.zeros((KV_HEADS * n, w1_ref.shape[-1]), F32)
    lq = 4
    for l0 in range(0, CMP_BLOCK, lq):
        halves = [[], []]
        for l in range(l0, l0 + lq, 2):
            for hf in range(2):
                a = load(l, hf) + pe_ref[kv, l:l + 1, :]
                b = load(l + 1, hf) + pe_ref[kv, l + 1:l + 2, :]
                even = jnp.where(lo, a, pltpu.roll(b, HEAD_DIM, 1))
                odd = jnp.where(lo, pltpu.roll(a, HEAD_DIM, 1), b)
                halves[hf].append((even, odd))
        groups = []
        for g in range(KV_HEADS):
            hf, par = divmod(g, 2)
            groups.append(jnp.concatenate([pc[par] for pc in halves[hf]], axis=1))
        xg = jnp.concatenate(groups, axis=0).astype(BF16)
        acc = acc + _dot(xg, w1_ref[kv, l0 * HEAD_DIM:(l0 + lq) * HEAD_DIM, :])
    h = jax.nn.gelu(acc).astype(BF16)
    return _dot_nt(h, w2t_ref[kv])


def _compress_prompt_kernel(s0_ref, s1_ref, s2_ref, s3_ref, pe_ref, w1_ref, w2t_ref, kc_ref, vc_ref):
    n = kc_ref.shape[0]
    hn = n // 2
    srcs = ((s0_ref, s1_ref), (s2_ref, s3_ref))
    for kv, out_ref in ((0, kc_ref), (1, vc_ref)):
        def load(l, hf, kv=kv):
            ev = srcs[kv][hf][pl.ds(l, hn, stride=2 * CMP_BLOCK), :]
            od = srcs[kv][hf][pl.ds(CMP_BLOCK + l, hn, stride=2 * CMP_BLOCK), :]
            return jnp.concatenate([ev, od], axis=0)
        y = _compress_rows(load, pe_ref, w1_ref, w2t_ref, kv, n)
        for g in range(KV_HEADS):
            out_ref[:, g * HEAD_DIM:(g + 1) * HEAD_DIM] = y[g * n:(g + 1) * n].astype(BF16)


def _compress_prompt(cmp_rows, pe2, w1, w2t, batch, t_seq):
    n = t_seq // CMP_BLOCK
    chunk = lambda c: pl.BlockSpec((t_seq, LANES), lambda b: (b, c))
    return pl.pallas_call(
        _compress_prompt_kernel,
        grid=(batch,),
        in_specs=[chunk(0), chunk(1), chunk(2), chunk(3), _full(pe2.shape), _full(w1.shape), _full(w2t.shape)],
        out_specs=[pl.BlockSpec((None, n, KV_DIM), lambda b: (b, 0, 0))] * 2,
        out_shape=[jax.ShapeDtypeStruct((batch, n, KV_DIM), BF16)] * 2,
        compiler_params=_cparams("parallel"),
        name="nsa_compress_prompt",
    )(cmp_rows, cmp_rows, cmp_rows, cmp_rows, pe2, w1, w2t)


def _topk_not_selected(imp_t):
    n_sel, nq = imp_t.shape
    b = lax.broadcasted_iota(jnp.int32, (n_sel, nq), 0)
    rank = jnp.zeros((n_sel, nq), jnp.int32)
    for bp in range(n_sel):
        other = imp_t[bp:bp + 1, :]
        ahead = (other > imp_t) | ((other == imp_t) & (bp < b))
        rank = rank + ahead.astype(jnp.int32)
    return jnp.where(rank < TOP_N, 0.0, 1.0)


def _nsa_attn_prompt_kernel(q_ref, gates_ref, kc_ref, vc_ref, ksa_ref, vsa_ref, kwa_ref, vwa_ref, o_ref,
                            acc_ref, m_ref):
    tq = q_ref.shape[0]
    n_cmp = kc_ref.shape[0]
    n_sel = n_cmp // 2
    rq = Q_PER_KV * tq
    tk = tq
    i = pl.program_id(1)
    t0 = i * tq
    t_row = t0 + lax.broadcasted_iota(jnp.int32, (rq, 1), 0) % tq
    key_iota = lax.broadcasted_iota(jnp.int32, (rq, tk), 1)

    def flash_step(qa, k, v, mask):
        s = _dot(qa, k)
        if mask is not None:
            s = jnp.where(mask, s, NEG)
        m_old = m_ref[...]
        m_new = jnp.maximum(m_old, jnp.max(s, axis=1, keepdims=True))
        p = jnp.exp(s - m_new)
        acc_ref[...] = acc_ref[...] * jnp.exp(m_old - m_new) + _dot_nt(p.astype(BF16), v)
        m_ref[...] = m_new

    def flash_result():
        acc = acc_ref[...]
        return acc[:, :HEAD_DIM] / jnp.maximum(acc[:, HEAD_DIM:], 1e-30)

    for g in range(KV_HEADS):
        qg = jnp.concatenate(
            [q_ref[:, (Q_PER_KV * g + r) * HEAD_DIM:(Q_PER_KV * g + r + 1) * HEAD_DIM] for r in range(Q_PER_KV)],
            axis=0)

        kcg = kc_ref[:, g * HEAD_DIM:(g + 1) * HEAD_DIM]
        vcg = vc_ref[:, g * HEAD_DIM:(g + 1) * HEAD_DIM]
        sc = _dot_nt(kcg, qg)
        nrow = lax.broadcasted_iota(jnp.int32, (n_cmp, rq), 0)
        cblk = jnp.where(nrow < n_sel, 2 * nrow, 2 * (nrow - n_sel) + 1)
        t_col = t0 + lax.broadcasted_iota(jnp.int32, (n_cmp, rq), 1) % tq
        cmask = (cblk + 1) * CMP_BLOCK - 1 <= t_col
        sc = jnp.where(cmask, sc, NEG)
        e = jnp.where(cmask, jnp.exp(sc - jnp.max(sc, axis=0, keepdims=True)), 0.0)
        pc = e / jnp.maximum(jnp.sum(e, axis=0, keepdims=True), 1e-30)
        o_c = _dot(pc.T.astype(BF16), vcg)

        pq = pc[:, 0:tq]
        for r in range(1, Q_PER_KV):
            pq = pq + pc[:, r * tq:(r + 1) * tq]
        imp = pq[:n_sel] + pq[n_sel:]
        blk = lax.broadcasted_iota(jnp.int32, (n_sel, tq), 0)
        cur = (t0 + lax.broadcasted_iota(jnp.int32, (n_sel, tq), 1)) // SEL_BLOCK
        imp = jnp.where(blk > cur, -1.0, imp)
        imp = jnp.where(blk == 0, FORCED_FIRST, imp)
        imp = jnp.where(blk == cur, FORCED_CUR, imp)
        notsel_t = jnp.concatenate([_topk_not_selected(imp), jnp.zeros((LANES - n_sel, tq), F32)], axis=0)
        notsel = notsel_t.T[:, :HEAD_DIM]
        nots = jnp.concatenate([notsel] * Q_PER_KV, axis=0).astype(BF16)
        qa = jnp.concatenate([qg, nots], axis=1)

        m_ref[...] = jnp.full((rq, 1), NEG, F32)
        acc_ref[...] = jnp.zeros((rq, LANES), F32)

        def sel_body(kt, carry):
            flash_step(qa, ksa_ref[g, kt], vsa_ref[g, kt], None)
            return carry

        lax.fori_loop(0, i, sel_body, 0)
        flash_step(qa, ksa_ref[g, i], vsa_ref[g, i], t0 + key_iota <= t_row)
        o_s = flash_result()

        m_ref[...] = jnp.full((rq, 1), NEG, F32)
        acc_ref[...] = jnp.zeros((rq, LANES), F32)

        def win_body(step, carry):
            kt = i - step
            kpos = kt * tk + key_iota
            mask = (kpos <= t_row) & (kpos > t_row - WINDOW)
            flash_step(qa, kwa_ref[g, kt], vwa_ref[g, kt], mask)
            return carry

        lax.fori_loop(0, jnp.minimum(i, WINDOW // tk) + 1, win_body, 0)
        o_w = flash_result()

        for r in range(Q_PER_KV):
            h = Q_PER_KV * g + r
            rs = slice(r * tq, (r + 1) * tq)
            gc = gates_ref[:, N_BRANCH * h:N_BRANCH * h + 1]
            gs = gates_ref[:, N_BRANCH * h + 1:N_BRANCH * h + 2]
            gw = gates_ref[:, N_BRANCH * h + 2:N_BRANCH * h + 3]
            o_ref[:, h * HEAD_DIM:(h + 1) * HEAD_DIM] = (gc * o_c[rs] + gs * o_s[rs] + gw * o_w[rs]).astype(BF16)


def _nsa_attn_prompt(q, gates, kc, vc, ksa, vsa, kwa, vwa, batch, t_seq):
    m, d = q.shape
    tq = Q_TILE
    nq = t_seq // tq
    n_cmp = t_seq // CMP_BLOCK
    row = lambda cols: pl.BlockSpec((tq, cols), lambda b, i: (b * nq + i, 0))
    seq = pl.BlockSpec((None, KV_HEADS, nq, LANES, tq), lambda b, i: (b, 0, 0, 0, 0))
    cmp_spec = pl.BlockSpec((None, n_cmp, KV_DIM), lambda b, i: (b, 0, 0))
    return pl.pallas_call(
        _nsa_attn_prompt_kernel,
        grid=(batch, nq),
        in_specs=[row(d), row(LANES), cmp_spec, cmp_spec, seq, seq, seq, seq],
        out_specs=row(d),
        out_shape=jax.ShapeDtypeStruct((m, d), BF16),
        scratch_shapes=[pltpu.VMEM((Q_PER_KV * tq, LANES), F32), pltpu.VMEM((Q_PER_KV * tq, 1), F32)],
        compiler_params=_cparams("parallel", "arbitrary"),
        name="nsa_attn_prompt",
    )(q, gates, kc, vc, ksa, vsa, kwa, vwa)


def _compress_sample_kernel(pt_ref, *refs, n_pages, page, t_new, n_cmp):
    del pt_ref
    pages = refs[:n_pages]
    new_ref, pe_ref, w1_ref, w2t_ref, kc_ref, vc_ref = refs[n_pages:n_pages + 6]
    bufs = refs[n_pages + 6:]
    n_pad = bufs[0].shape[0] // CMP_BLOCK
    past = n_pages * page
    for c, buf_ref in enumerate(bufs):
        cs = slice(c * LANES, (c + 1) * LANES)
        for p in range(n_pages):
            buf_ref[p * page:(p + 1) * page, :] = pages[p][cs, :].T
        buf_ref[past:past + t_new, :] = new_ref[:, cs]
        buf_ref[past + t_new:, :] = jnp.zeros((buf_ref.shape[0] - past - t_new, LANES), F32)
    n_out = kc_ref.shape[0]
    for kv, out_ref in ((0, kc_ref), (1, vc_ref)):
        def load(l, hf, kv=kv):
            return bufs[2 * kv + hf][pl.ds(l, n_pad, stride=CMP_BLOCK), :]
        y = _compress_rows(load, pe_ref, w1_ref, w2t_ref, kv, n_pad)
        valid = lax.broadcasted_iota(jnp.int32, (n_pad, HEAD_DIM), 0) < n_cmp
        for g in range(KV_HEADS):
            out_ref[0:n_pad, g * HEAD_DIM:(g + 1) * HEAD_DIM] = jnp.where(
                valid, y[g * n_pad:(g + 1) * n_pad], 0.0).astype(BF16)
        out_ref[n_pad:, :] = jnp.zeros((n_out - n_pad, KV_DIM), BF16)


def _page_specs(layer, n_pages, page, feat_block):
    half = 2 * KV_DIM
    return [pl.BlockSpec((None, None, half, page),
                         functools.partial(lambda b, pt, p: (layer, pt[b, p], feat_block, 0), p=p))
            for p in range(n_pages)]


def _compress_sample(page_table, cache_t, layer, rows_new, pe2, w1, w2t, t_new):
    dec_b, n_pages = page_table.shape
    page = cache_t.shape[-1]
    past = n_pages * page
    l_pad = -(-(past + t_new) // SEL_BLOCK) * SEL_BLOCK
    n_cmp = l_pad // CMP_BLOCK
    n_pad = -(-n_cmp // 8) * 8
    n_out = LANES
    half = 2 * KV_DIM
    kern = functools.partial(_compress_sample_kernel, n_pages=n_pages, page=page, t_new=t_new, n_cmp=n_cmp)
    grid_spec = pltpu.PrefetchScalarGridSpec(
        num_scalar_prefetch=1,
        grid=(dec_b,),
        in_specs=_page_specs(layer, n_pages, page, 0) + [
            pl.BlockSpec((t_new, half), lambda b, pt: (b, 0)),
            pl.BlockSpec(pe2.shape, lambda b, pt: (0, 0, 0)),
            pl.BlockSpec(w1.shape, lambda b, pt: (0, 0, 0)),
            pl.BlockSpec(w2t.shape, lambda b, pt: (0, 0, 0))],
        out_specs=[pl.BlockSpec((None, n_out, KV_DIM), lambda b, pt: (b, 0, 0))] * 2,
        scratch_shapes=[pltpu.VMEM((n_pad * CMP_BLOCK, LANES), F32)] * (half // LANES),
    )
    return pl.pallas_call(
        kern,
        grid_spec=grid_spec,
        out_shape=[jax.ShapeDtypeStruct((dec_b, n_out, KV_DIM), BF16)] * 2,
        compiler_params=_cparams("arbitrary"),
        name="nsa_compress_sample",
    )(page_table, *([cache_t] * n_pages), rows_new, pe2, w1, w2t), n_cmp


def _nsa_attn_sample_kernel(pt_ref, *refs, n_pages, page, t_new, n_cmp):
    del pt_ref
    pages = refs[:n_pages]
    (kc_ref, vc_ref, q_ref, new_ref, wnew_ref, win_ref, gates_ref,
     o_ref, nwin_ref, kst_ref, vst_ref) = refs[n_pages:]
    past = n_pages * page
    n_buf = win_ref.shape[1]
    n_sel = n_cmp // 2
    gq = Q_PER_KV * t_new
    rq = KV_HEADS * gq
    nk = past + LANES
    nw = n_buf + LANES

    for p in range(n_pages):
        blk = pages[p][...]
        kst_ref[:, p * page:(p + 1) * page] = blk[:KV_DIM].astype(BF16)
        vst_ref[:, p * page:(p + 1) * page] = blk[KV_DIM:].astype(BF16)
    pad = jnp.zeros((LANES - t_new, KV_DIM), F32)
    new = new_ref[...]
    wnew = wnew_ref[...]
    knew = jnp.concatenate([new[:, :KV_DIM], pad], axis=0).astype(BF16)
    vnew = jnp.concatenate([new[:, KV_DIM:], pad], axis=0).astype(BF16)
    kwnew = jnp.concatenate([wnew[:, :KV_DIM], pad], axis=0).astype(BF16)
    vwnew = jnp.concatenate([wnew[:, KV_DIM:], pad], axis=0).astype(BF16)
    win = win_ref[...]
    kwt = win[:KV_DIM].astype(BF16)
    vwt = win[KV_DIM:].astype(BF16)

    rolled = pltpu.roll(win, n_buf - t_new, 1)
    wnew_t = jnp.concatenate([jnp.zeros((LANES - t_new, 2 * KV_DIM), F32), wnew], axis=0).T
    tail_lane = lax.broadcasted_iota(jnp.int32, (2 * KV_DIM, LANES), 1)
    nwin_ref[:, 0:n_buf - LANES] = rolled[:, 0:n_buf - LANES]
    nwin_ref[:, n_buf - LANES:] = jnp.where(tail_lane >= LANES - t_new, wnew_t, rolled[:, n_buf - LANES:])

    q = q_ref[...]
    lane_g = lax.broadcasted_iota(jnp.int32, (t_new, KV_DIM), 1) // HEAD_DIM
    pieces = []
    for g in range(KV_HEADS):
        for j in range(Q_PER_KV):
            qj = jnp.concatenate([q[:, (Q_PER_KV * gg + j) * HEAD_DIM:(Q_PER_KV * gg + j + 1) * HEAD_DIM]
                                  for gg in range(KV_HEADS)], axis=1)
            pieces.append(jnp.where(lane_g == g, qj, jnp.zeros_like(qj)))
    qbd = jnp.concatenate(pieces, axis=0)
    t_row = past + lax.broadcasted_iota(jnp.int32, (rq, 1), 0) % t_new

    def softmax(s, mask):
        s = jnp.where(mask, s, NEG)
        e = jnp.where(mask, jnp.exp(s - jnp.max(s, axis=1, keepdims=True)), 0.0)
        return e / jnp.maximum(jnp.sum(e, axis=1, keepdims=True), 1e-30)

    ncl = kc_ref.shape[0]
    sc = _dot_nt(qbd, kc_ref[...])
    cidx = lax.broadcasted_iota(jnp.int32, (rq, ncl), 1)
    pc = softmax(sc, (cidx < n_cmp) & ((cidx + 1) * CMP_BLOCK - 1 <= t_row))
    oc = _dot(pc.astype(BF16), vc_ref[...])

    groups = []
    for g in range(KV_HEADS):
        pg = pc[g * gq:g * gq + t_new]
        for j in range(1, Q_PER_KV):
            pg = pg + pc[g * gq + j * t_new:g * gq + (j + 1) * t_new]
        groups.append(pg)
    pg = jnp.concatenate(groups, axis=0)
    nr = KV_HEADS * t_new
    imp = pg + pltpu.roll(pg, ncl - 1, 1)
    lane = lax.broadcasted_iota(jnp.int32, (nr, ncl), 1)
    blk = lane // 2
    cur = (past + lax.broadcasted_iota(jnp.int32, (nr, ncl), 0) % t_new) // SEL_BLOCK
    imp = jnp.where(blk > cur, -1.0, imp)
    imp = jnp.where(blk == 0, FORCED_FIRST, imp)
    imp = jnp.where(blk == cur, FORCED_CUR, imp)
    rank = jnp.zeros((nr, ncl), jnp.int32)
    for bp in range(n_sel):
        other = imp[:, 2 * bp:2 * bp + 1]
        rank = rank + ((other > imp) | ((other == imp) & (bp < blk))).astype(jnp.int32)
    sel = jnp.where((rank < TOP_N) & (lane % 2 == 0) & (blk < n_sel), 1.0, 0.0)
    sel_rows = jnp.concatenate([sel[g * t_new:(g + 1) * t_new] for g in range(KV_HEADS) for _ in range(Q_PER_KV)],
                               axis=0).astype(BF16)
    expand = jnp.where(lax.broadcasted_iota(jnp.int32, (ncl, nk), 0)
                       == 2 * (lax.broadcasted_iota(jnp.int32, (ncl, nk), 1) // SEL_BLOCK), 1.0, 0.0).astype(BF16)
    selmask = _dot(sel_rows, expand) > 0.5

    kpos = lax.broadcasted_iota(jnp.int32, (rq, nk), 1)
    ss = jnp.concatenate([_dot(qbd, kst_ref[...]), _dot_nt(qbd, knew)], axis=1)
    ps = softmax(ss, selmask & (kpos <= t_row)).astype(BF16)
    os_ = _dot_nt(ps[:, :past], vst_ref[...]) + _dot(ps[:, past:], vnew)

    wpos = past - n_buf + lax.broadcasted_iota(jnp.int32, (rq, nw), 1)
    sw = jnp.concatenate([_dot(qbd, kwt), _dot_nt(qbd, kwnew)], axis=1)
    pw = softmax(sw, (wpos <= t_row) & (wpos > t_row - WINDOW)).astype(BF16)
    ow = _dot_nt(pw[:, :n_buf], vwt) + _dot(pw[:, n_buf:], vwnew)

    gates = gates_ref[...]
    for g in range(KV_HEADS):
        for j in range(Q_PER_KV):
            h = Q_PER_KV * g + j
            rs = slice(g * gq + j * t_new, g * gq + (j + 1) * t_new)
            cs = slice(g * HEAD_DIM, (g + 1) * HEAD_DIM)
            gc = gates[:, N_BRANCH * h:N_BRANCH * h + 1]
            gs = gates[:, N_BRANCH * h + 1:N_BRANCH * h + 2]
            gw = gates[:, N_BRANCH * h + 2:N_BRANCH * h + 3]
            o_ref[:, h * HEAD_DIM:(h + 1) * HEAD_DIM] = (
                gc * oc[rs, cs] + gs * os_[rs, cs] + gw * ow[rs, cs]).astype(BF16)


def _nsa_attn_sample(page_table, cache_t, win_t, layer, kc, vc, q, rows_new, wrows_new, gates, t_new, n_cmp):
    dec_b, n_pages = page_table.shape
    page = cache_t.shape[-1]
    past = n_pages * page
    n_buf = win_t.shape[-1]
    half = 2 * KV_DIM
    d = q.shape[1]
    ncl = kc.shape[1]
    kern = functools.partial(_nsa_attn_sample_kernel, n_pages=n_pages, page=page, t_new=t_new, n_cmp=n_cmp)
    grid_spec = pltpu.PrefetchScalarGridSpec(
        num_scalar_prefetch=1,
        grid=(dec_b,),
        in_specs=_page_specs(layer, n_pages, page, 1) + [
            pl.BlockSpec((None, ncl, KV_DIM), lambda b, pt: (b, 0, 0)),
            pl.BlockSpec((None, ncl, KV_DIM), lambda b, pt: (b, 0, 0)),
            pl.BlockSpec((t_new, d), lambda b, pt: (b, 0)),
            pl.BlockSpec((t_new, half), lambda b, pt: (b, 1)),
            pl.BlockSpec((t_new, half), lambda b, pt: (b, 0)),
            pl.BlockSpec((None, None, half, n_buf), lambda b, pt: (layer, b, 0, 0)),
            pl.BlockSpec((t_new, LANES), lambda b, pt: (b, 0))],
        out_specs=[pl.BlockSpec((t_new, d), lambda b, pt: (b, 0)),
                   pl.BlockSpec((None, half, n_buf), lambda b, pt: (b, 0, 0))],
        scratch_shapes=[pltpu.VMEM((KV_DIM, past), BF16), pltpu.VMEM((KV_DIM, past), BF16)],
    )
    return pl.pallas_call(
        kern,
        grid_spec=grid_spec,
        out_shape=[jax.ShapeDtypeStruct((dec_b * t_new, d), BF16), jax.ShapeDtypeStruct((dec_b, half, n_buf), F32)],
        compiler_params=_cparams("arbitrary"),
        name="nsa_attn_sample",
    )(page_table, *([cache_t] * n_pages), kc, vc, q, rows_new, wrows_new, win_t, gates)


def _outproj_kernel(x_ref, o_ref, w_ref, y_ref):
    y_ref[...] = x_ref[...] + _dot(o_ref[...], w_ref[...])


def _outproj(x, o, w):
    m, d = x.shape
    tm = min(ROW_TILE, m)
    row = pl.BlockSpec((tm, d), lambda i: (i, 0))
    return pl.pallas_call(
        _outproj_kernel,
        grid=(m // tm,),
        in_specs=[row, row, _full(w.shape)],
        out_specs=row,
        out_shape=jax.ShapeDtypeStruct((m, d), F32),
        compiler_params=_cparams("parallel"),
        name="mixer_outproj",
    )(x, o, w)


def _mlp_kernel(x_ref, g_ref, w1_ref, w2_ref, y_ref, xn_ref, acc_ref):
    j = pl.program_id(1)

    @pl.when(j == 0)
    def _():
        x = x_ref[...]
        xn_ref[...] = _rms(x, g_ref[...]).astype(BF16)
        acc_ref[...] = x

    h = jnp.maximum(_dot(xn_ref[...], w1_ref[...]), 0.0)
    acc_ref[...] += _dot((h * h).astype(BF16), w2_ref[...])

    @pl.when(j == pl.num_programs(1) - 1)
    def _():
        y_ref[...] = acc_ref[...]


def _mlp(x, g, w1, w2):
    m, d = x.shape
    ff = w1.shape[1]
    tm, tf = min(ROW_TILE, m), 1024
    row = pl.BlockSpec((tm, d), lambda i, j: (i, 0))
    return pl.pallas_call(
        _mlp_kernel,
        grid=(m // tm, ff // tf),
        in_specs=[row, pl.BlockSpec((1, d), lambda i, j: (0, 0)),
                  pl.BlockSpec((d, tf), lambda i, j: (0, j)), pl.BlockSpec((tf, d), lambda i, j: (j, 0))],
        out_specs=row,
        out_shape=jax.ShapeDtypeStruct((m, d), F32),
        scratch_shapes=[pltpu.VMEM((tm, d), BF16), pltpu.VMEM((tm, d), F32)],
        compiler_params=_cparams("parallel", "arbitrary"),
        name="sqrelu_mlp",
    )(x, g, w1, w2)


def _ple_kernel(x_ref, p_ref, g_ref, wg_ref, wp_ref, gf_ref, y_ref, *, final_norm):
    x = x_ref[...]
    gate = jax.nn.sigmoid(_dot(_rms(x, g_ref[...]).astype(BF16), wg_ref[...]))
    y = x + _dot(p_ref[...].astype(BF16), wp_ref[...]) * gate
    if final_norm:
        y = _rms(y, gf_ref[...])
    y_ref[...] = y


def _ple(x, p_all, layer, g, wg, wp, gf, final_norm):
    m, d = x.shape
    tm = min(ROW_TILE, m)
    row = pl.BlockSpec((tm, d), lambda i: (i, 0))
    return pl.pallas_call(
        functools.partial(_ple_kernel, final_norm=final_norm),
        grid=(m // tm,),
        in_specs=[row, pl.BlockSpec((None, tm, p_all.shape[-1]), lambda i: (layer, i, 0)), _full((1, d)),
                  _full(wg.shape), _full(wp.shape), _full((1, d))],
        out_specs=row,
        out_shape=jax.ShapeDtypeStruct((m, d), F32),
        compiler_params=_cparams("parallel"),
        name="ple_gate",
    )(x, p_all, g, wg, wp, gf)


def _hg_inproj_kernel(x_ref, g_ref, lbl_ref, w_ref, q_ref, f_ref, k_ref, v_ref, gs_ref, xn_ref, *, layer):
    j = pl.program_id(1)

    @pl.when(j == 0)
    def _():
        xn_ref[...] = _rms(x_ref[...], g_ref[...]).astype(BF16)

    y = _dot(xn_ref[...], w_ref[...])

    def put(ref, val):
        for h in range(HG_HEADS):
            ref[h] = val[:, h * HG_DK:(h + 1) * HG_DK]

    @pl.when(j == 0)
    def _():
        put(q_ref, jax.nn.silu(y) * HG_DK ** -0.5)

    @pl.when(j == 1)
    def _():
        lg = lbl_ref[...]
        e = jnp.exp(lg - jnp.max(lg, axis=0, keepdims=True))
        w = e / jnp.sum(e, axis=0, keepdims=True)
        cs = w[0:1]
        for r in range(1, layer + 1):
            cs = cs + w[r:r + 1]
        lb = cs - w[0:1]
        put(f_ref, jnp.log(lb + (1.0 - lb) * jax.nn.sigmoid(y)))
        put(k_ref, (1.0 - lb) * jax.nn.sigmoid(-y))

    @pl.when(j == 2)
    def _():
        put(v_ref, y)

    @pl.when(j == 3)
    def _():
        put(gs_ref, jax.nn.silu(y))


def _hg_inproj(x, g, lb_logits, w, layer):
    m, d = x.shape
    tm = min(ROW_TILE, m)
    head_major = pl.BlockSpec((HG_HEADS, tm, HG_DK), lambda i, j: (0, i, 0))
    return pl.pallas_call(
        functools.partial(_hg_inproj_kernel, layer=layer),
        grid=(m // tm, 4),
        in_specs=[pl.BlockSpec((tm, d), lambda i, j: (i, 0)), pl.BlockSpec((1, d), lambda i, j: (0, 0)),
                  pl.BlockSpec(lb_logits.shape, lambda i, j: (0, 0)), pl.BlockSpec((d, d), lambda i, j: (0, j))],
        out_specs=[head_major] * 5,
        out_shape=[jax.ShapeDtypeStruct((HG_HEADS, m, HG_DK), F32)] * 5,
        scratch_shapes=[pltpu.VMEM((tm, d), BF16)],
        compiler_params=_cparams("parallel", "arbitrary"),
        name="hgrn_inproj",
    )(x, g, lb_logits, w)


def _cumsum_rows(f):
    c = f.shape[0]
    row = lax.broadcasted_iota(jnp.int32, f.shape, 0)
    s = 1
    while s < c:
        f = f + jnp.where(row >= s, pltpu.roll(f, s, 0), 0.0)
        s *= 2
    return f


def _hg_intra(q, k, v, b, o, sub):
    c = q.shape[0]
    ones = jnp.ones((HG_DK, HG_DV), BF16)
    trow = lax.broadcasted_iota(jnp.int32, (sub, HG_DK), 0)
    outs = []
    for r0 in range(0, c, sub):
        qi, ki, vi, bi = q[r0:r0 + sub], k[r0:r0 + sub], v[r0:r0 + sub], b[r0:r0 + sub]
        oi = o[r0:r0 + sub]
        if r0 > 0:
            bs = b[r0 - 1:r0]
            qp = (qi * jnp.exp(bi - bs)).astype(BF16)
            kp = (k[:r0] * jnp.exp(bs - b[:r0])).astype(BF16)
            oi = oi + _dot(_dot_nt(qp, kp).astype(BF16), v[:r0].astype(BF16))
        for s in range(sub):
            causal = trow >= s
            dec = jnp.exp(jnp.where(causal, bi - bi[s:s + 1], 0.0))
            z = jnp.where(causal, qi * ki[s:s + 1] * dec, 0.0)
            oi = oi + _dot(z.astype(BF16), ones) * vi[s:s + 1]
        outs.append(oi)
    return jnp.concatenate(outs, axis=0) if len(outs) > 1 else outs[0]


def _hg_out(o, ng, gs):
    return (_rms(o, ng) * gs).astype(BF16)


def _hg_scan_prompt_kernel(q_ref, f_ref, k_ref, v_ref, gs_ref, ng_ref, o_ref, s_ref, st_ref):
    c = pl.program_id(1)
    rows = q_ref.shape[1]

    @pl.when(c == 0)
    def _():
        st_ref[...] = jnp.zeros(st_ref.shape, F32)

    def head(h, carry):
        st = st_ref[h]
        for r0 in range(0, rows, HG_CHUNK):
            rs = pl.ds(r0, HG_CHUNK)
            q, k, v = q_ref[h, rs, :], k_ref[h, rs, :], v_ref[h, rs, :]
            b = _cumsum_rows(f_ref[h, rs, :])
            o = _dot_nt((q * jnp.exp(b)).astype(BF16), st.astype(BF16))
            o = _hg_intra(q, k, v, b, o, HG_SUB)
            o_ref[h, rs, :] = _hg_out(o, ng_ref[...], gs_ref[h, rs, :])
            bl = b[HG_CHUNK - 1:HG_CHUNK]
            kd = (k * jnp.exp(bl - b)).astype(BF16)
            st = st * jnp.exp(bl) + lax.dot_general(v.astype(BF16), kd, (((0,), (0,)), ((), ())),
                                                    preferred_element_type=F32)
        st_ref[h] = st

        @pl.when(c == pl.num_programs(1) - 1)
        def _():
            s_ref[h] = st.T
        return carry

    lax.fori_loop(0, HG_HEADS, head, 0)


def _hg_scan_prompt(q, f, k, v, gs, ng, batch, t_seq):
    m = q.shape[1]
    rows = 2 * HG_CHUNK
    nc = t_seq // rows
    blk = pl.BlockSpec((HG_HEADS, rows, HG_DK), lambda b, c: (0, b * nc + c, 0))
    return pl.pallas_call(
        _hg_scan_prompt_kernel,
        grid=(batch, nc),
        in_specs=[blk] * 5 + [pl.BlockSpec((1, HG_DV), lambda b, c: (0, 0))],
        out_specs=[blk, pl.BlockSpec((None, HG_HEADS, HG_DK, HG_DV), lambda b, c: (b, 0, 0, 0))],
        out_shape=[jax.ShapeDtypeStruct((HG_HEADS, m, HG_DV), BF16),
                   jax.ShapeDtypeStruct((batch, HG_HEADS, HG_DK, HG_DV), F32)],
        scratch_shapes=[pltpu.VMEM((HG_HEADS, HG_DV, HG_DK), F32)],
        compiler_params=_cparams("parallel", "arbitrary"),
        name="hgrn_scan_prompt",
    )(q, f, k, v, gs, ng)


def _hg_step_sample_kernel(q_ref, f_ref, k_ref, v_ref, gs_ref, ng_ref, s0_ref, o_ref, s1_ref, *, t_new):
    n_seq = s0_ref.shape[0]

    def body(idx, carry):
        sq = idx // HG_HEADS
        h = idx % HG_HEADS
        rs = pl.ds(pl.multiple_of(sq * t_new, t_new), t_new)
        q, k, v = q_ref[h, rs, :], k_ref[h, rs, :], v_ref[h, rs, :]
        b = _cumsum_rows(f_ref[h, rs, :])
        s0 = s0_ref[sq, h]
        o = _dot((q * jnp.exp(b)).astype(BF16), s0.astype(BF16))
        o = _hg_intra(q, k, v, b, o, t_new)
        o_ref[h, rs, :] = _hg_out(o, ng_ref[...], gs_ref[h, rs, :])
        bl = b[t_new - 1:t_new]
        kd = k * jnp.exp(bl - b)
        ext = jnp.concatenate([kd, jnp.broadcast_to(jnp.exp(bl), (t_new, HG_DK)),
                               jnp.zeros((HG_DK - 2 * t_new, HG_DK), F32)], axis=0).T
        s1_ref[sq, h] = s0 * ext[:, t_new:t_new + 1] + _dot(ext[:, :t_new].astype(BF16), v.astype(BF16))
        return carry

    lax.fori_loop(0, n_seq * HG_HEADS, body, 0)


def _hg_step_sample(q, f, k, v, gs, ng, s0_all, layer, t_new):
    m = q.shape[1]
    dec_b = s0_all.shape[1]
    sb = min(8, dec_b)
    blk = pl.BlockSpec((HG_HEADS, sb * t_new, HG_DK), lambda i: (0, i, 0))
    st_in = pl.BlockSpec((None, sb, HG_HEADS, HG_DK, HG_DV), lambda i: (layer, i, 0, 0, 0))
    st_out = pl.BlockSpec((sb, HG_HEADS, HG_DK, HG_DV), lambda i: (i, 0, 0, 0))
    return pl.pallas_call(
        functools.partial(_hg_step_sample_kernel, t_new=t_new),
        grid=(dec_b // sb,),
        in_specs=[blk] * 5 + [pl.BlockSpec((1, HG_DV), lambda i: (0, 0)), st_in],
        out_specs=[blk, st_out],
        out_shape=[jax.ShapeDtypeStruct((HG_HEADS, m, HG_DV), BF16), jax.ShapeDtypeStruct(s0_all.shape[1:], F32)],
        compiler_params=_cparams("parallel"),
        name="hgrn_step_sample",
    )(q, f, k, v, gs, ng, s0_all)


def _hg_outproj_kernel(x_ref, o_ref, w_ref, y_ref):
    o = jnp.concatenate([o_ref[h] for h in range(HG_HEADS)], axis=1)
    y_ref[...] = x_ref[...] + _dot(o, w_ref[...])


def _hg_outproj(x, o, w):
    m, d = x.shape
    tm = min(ROW_TILE, m)
    row = pl.BlockSpec((tm, d), lambda i: (i, 0))
    return pl.pallas_call(
        _hg_outproj_kernel,
        grid=(m // tm,),
        in_specs=[row, pl.BlockSpec((HG_HEADS, tm, HG_DV), lambda i: (0, i, 0)), _full(w.shape)],
        out_specs=row,
        out_shape=jax.ShapeDtypeStruct((m, d), F32),
        compiler_params=_cparams("parallel"),
        name="hgrn_outproj",
    )(x, o, w)


def _feature_major_view(a):
    lead = a.shape[:-4]
    t, c, g, d = a.shape[-4:]
    n = len(lead)
    return jnp.transpose(a, tuple(range(n)) + (n + 1, n + 2, n + 3, n)).reshape(lead + (c * g * d, t))


def _token_major_view(a, c):
    lead = a.shape[:-2]
    t = a.shape[-1]
    n = len(lead)
    a = a.reshape(lead + (c, KV_HEADS, HEAD_DIM, t))
    return jnp.transpose(a, tuple(range(n)) + (n + 3, n, n + 1, n + 2))


def kernel(x_prompt, x_sample, cache_nsa_kv, state_nsa_win, state_hgrn, page_table, p_prompt, p_sample, norm_mix, norm_mlp, norm_ple, norm_final, nsa_w_in, nsa_cmp_pe, nsa_cmp_w1, nsa_cmp_w2, nsa_w_out, hg_w_in, hg_lb_logits, hg_norm, hg_w_out, mlp_w1, mlp_w2, ple_w_proj, ple_w_gate):
    batch, t_p, d = x_prompt.shape
    dec_b, t_s, _ = x_sample.shape
    depth = p_prompt.shape[0]
    page = cache_nsa_kv.shape[2]
    past = page_table.shape[1] * page

    xp = x_prompt.reshape(batch * t_p, d)
    xs = x_sample.reshape(dec_b * t_s, d)
    pp = p_prompt.reshape(depth, batch * t_p, -1)
    ps = p_sample.reshape(depth, dec_b * t_s, -1)
    row = lambda a: a.reshape(1, -1)
    cache_t = _feature_major_view(cache_nsa_kv)
    win_t = _feature_major_view(state_nsa_win)
    wl = min(WINDOW, t_p)

    tabs_p = _rope_tables(t_p, 0, t_p)
    tabs_s = _rope_tables(min(ROW_TILE, dec_b * t_s), past, t_s)

    kv_p, kv_s, win_p, win_s, st_p, st_s = [], [], [], [], [], []
    for i in range(depth):
        g_mix = row(norm_mix[i])
        if i % 2 == 0:
            a = i // 2
            n_in = nsa_w_in.shape[2]
            w_t = jnp.pad(jnp.transpose(nsa_w_in[a]).astype(BF16),
                          ((0, Q_DIM + 6 * KV_DIM + LANES - n_in), (0, 0)))
            pe2 = jnp.tile(nsa_cmp_pe[a], (1, 1, LANES // HEAD_DIM))
            cw1 = nsa_cmp_w1[a].reshape(2, CMP_BLOCK * HEAD_DIM, -1).astype(BF16)
            cw2t = jnp.swapaxes(nsa_cmp_w2[a], 1, 2).astype(BF16)
            w_out = nsa_w_out[a].astype(BF16)

            q, gates, cmp_rows, rows_t, wrows_t, ksa, vsa, kwa, vwa = _nsa_inproj_prompt(
                xp, g_mix, tabs_p, w_t, batch, t_p)
            kc, vc = _compress_prompt(cmp_rows, pe2, cw1, cw2t, batch, t_p)
            o = _nsa_attn_prompt(q, gates, kc, vc, ksa, vsa, kwa, vwa, batch, t_p)
            xp = _outproj(xp, o, w_out)
            kv_p.append(rows_t)
            win_p.append(wrows_t[:, :, t_p - wl:])

            q, gates, rows, wrows = _nsa_inproj_sample(xs, g_mix, tabs_s, w_t)
            (kc, vc), n_cmp = _compress_sample(page_table, cache_t, a, rows, pe2, cw1, cw2t, t_s)
            o, new_win_t = _nsa_attn_sample(page_table, cache_t, win_t, a, kc, vc, q, rows, wrows, gates, t_s, n_cmp)
            xs = _outproj(xs, o, w_out)
            kv_s.append(rows.reshape(dec_b, t_s, 4, KV_HEADS, HEAD_DIM))
            win_s.append(new_win_t)
        else:
            r = i // 2
            w_in = hg_w_in[r].astype(BF16)
            w_out = hg_w_out[r].astype(BF16)
            ng = row(hg_norm[r])
            q, f, k, v, gs = _hg_inproj(xp, g_mix, hg_lb_logits, w_in, r)
            o, s_new = _hg_scan_prompt(q, f, k, v, gs, ng, batch, t_p)
            xp = _hg_outproj(xp, o, w_out)
            st_p.append(s_new)
            q, f, k, v, gs = _hg_inproj(xs, g_mix, hg_lb_logits, w_in, r)
            o, s_new = _hg_step_sample(q, f, k, v, gs, ng, state_hgrn, r, t_s)
            xs = _hg_outproj(xs, o, w_out)
            st_s.append(s_new)

        w1 = mlp_w1[i].astype(BF16)
        w2 = mlp_w2[i].astype(BF16)
        wgate = ple_w_gate[i].astype(BF16)
        wproj = ple_w_proj[i].astype(BF16)
        last = i == depth - 1
        xp = _mlp(xp, row(norm_mlp[i]), w1, w2)
        xs = _mlp(xs, row(norm_mlp[i]), w1, w2)
        xp = _ple(xp, pp, i, row(norm_ple[i]), wgate, wproj, row(norm_final), last)
        xs = _ple(xs, ps, i, row(norm_ple[i]), wgate, wproj, row(norm_final), last)

    return (xp.reshape(batch, t_p, d), xs.reshape(dec_b, t_s, d),
            _token_major_view(jnp.stack(kv_p), 4), jnp.stack(kv_s),
            _token_major_view(jnp.stack(win_p), 2), _token_major_view(jnp.stack(win_s), 2),
            jnp.stack(st_p), jnp.stack(st_s))
```

```python
import functools

import jax
import jax.numpy as jnp
from jax import lax
from jax.experimental import pallas as pl
from jax.experimental.pallas import tpu as pltpu

F32 = jnp.float32
BF16 = jnp.bfloat16

RMS_EPS = 1e-6
ROPE_THETA = 10000.0
N_HEADS = 16
HEAD_DIM = 64
KV_HEADS = 4
Q_PER_KV = N_HEADS // KV_HEADS
Q_DIM = N_HEADS * HEAD_DIM
KV_DIM = KV_HEADS * HEAD_DIM
N_BRANCH = 3
CMP_BLOCK = 32
SEL_BLOCK = 64
TOP_N = 16
WINDOW = 512
FORCED_CUR = 3e4
FORCED_FIRST = 2e4
HG_HEADS = 8
HG_DK = 128
HG_DV = 128
HG_CHUNK = 64
HG_SUB = 16
HG_UNROLL = 4
NEG = -1e30

LANES = 128
SUBLANES = 8
ROW_TILE = 512
ATT_TILE = 256
VMEM_LIMIT = 56 * 1024 * 1024


def _cparams(*sem):
    return pltpu.CompilerParams(dimension_semantics=sem, vmem_limit_bytes=VMEM_LIMIT)


def _full(shape):
    return pl.BlockSpec(shape, lambda *_: (0,) * len(shape))


def _rms(x, g):
    return x * lax.rsqrt(jnp.mean(x * x, axis=-1, keepdims=True) + RMS_EPS) * g


def _dot(a, b):
    return jnp.dot(a, b, preferred_element_type=F32)


def _dot_nt(a, b):
    return lax.dot_general(a, b, (((1,), (1,)), ((), ())), preferred_element_type=F32)


def _rope_table_kernel(inv_ref, invc_ref, cos_ref, sin_ref, cost_ref, sint_ref, *, pos0, period):
    i = pl.program_id(0)
    tm = cos_ref.shape[0]
    half = HEAD_DIM // 2
    row = i * tm + lax.broadcasted_iota(jnp.int32, (tm, LANES), 0)
    ang = (pos0 + row % period).astype(F32) * inv_ref[...]
    lane = lax.broadcasted_iota(jnp.int32, (tm, LANES), 1)
    cos_ref[...] = jnp.cos(ang)
    s = jnp.sin(ang)
    sin_ref[...] = jnp.where(lane % HEAD_DIM < half, -s, s)
    col = i * tm + lax.broadcasted_iota(jnp.int32, (HEAD_DIM, tm), 1)
    ang_t = (pos0 + col % period).astype(F32) * jnp.concatenate([invc_ref[...]] * (tm // LANES), axis=1)
    feat = lax.broadcasted_iota(jnp.int32, (HEAD_DIM, tm), 0)
    cost_ref[...] = jnp.cos(ang_t)
    st = jnp.sin(ang_t)
    sint_ref[...] = jnp.where(feat < half, -st, st)


def _rope_tables(rows, pos0, period):
    half = HEAD_DIM // 2
    inv = ROPE_THETA ** (-jnp.arange(half, dtype=F32) / half)
    inv_row = jnp.tile(inv, LANES // half)[None, :]
    inv_col = jnp.broadcast_to(jnp.tile(inv, HEAD_DIM // half)[:, None], (HEAD_DIM, LANES))
    kern = functools.partial(_rope_table_kernel, pos0=pos0, period=period)
    tm = min(ROW_TILE, rows)
    tok = pl.BlockSpec((tm, LANES), lambda i: (i, 0))
    feat = pl.BlockSpec((HEAD_DIM, tm), lambda i: (0, i))
    return pl.pallas_call(
        kern,
        grid=(rows // tm,),
        in_specs=[_full((1, LANES)), _full((HEAD_DIM, LANES))],
        out_specs=[tok, tok, feat, feat],
        out_shape=[jax.ShapeDtypeStruct((rows, LANES), F32)] * 2 + [jax.ShapeDtypeStruct((HEAD_DIM, rows), F32)] * 2,
        compiler_params=_cparams("parallel"),
        name="rope_tables",
    )(inv_row, inv_col)


def _rope_tok(y, cos, sin):
    lane = lax.broadcasted_iota(jnp.int32, y.shape, 1)
    first = lane % HEAD_DIM < HEAD_DIM // 2
    sw = jnp.where(first, pltpu.roll(y, LANES - HEAD_DIM // 2, 1), pltpu.roll(y, HEAD_DIM // 2, 1))
    return y * cos + sw * sin


def _rope_feat(y, cost, sint):
    return y * cost + pltpu.roll(y, HEAD_DIM // 2, 0) * sint


def _nsa_inproj_prompt_kernel(x_ref, g_ref, cos_ref, sin_ref, cost_ref, sint_ref, w_ref,
                              qt_ref, gatest_ref, cmp_ref, rowst_ref, wrowst_ref,
                              ksa_ref, vsa_ref, kwa_ref, vwa_ref):
    tm = x_ref.shape[0]
    tk = ksa_ref.shape[2]
    i = pl.program_id(1)
    xn = _rms(x_ref[...], g_ref[...]).astype(BF16)
    cos, sin = cos_ref[...], sin_ref[...]
    cost, sint = cost_ref[...], sint_ref[...]
    scale = HEAD_DIM ** -0.5

    qt = _dot_nt(w_ref[0:Q_DIM, :], xn)
    for h in range(N_HEADS):
        rs = slice(h * HEAD_DIM, (h + 1) * HEAD_DIM)
        qt_ref[rs, :] = (_rope_feat(qt[rs], cost, sint) * scale).astype(BF16)
    g0 = Q_DIM + 6 * KV_DIM
    gatest_ref[...] = jax.nn.sigmoid(_dot_nt(w_ref[g0:g0 + LANES, :], xn))

    ycmp = _dot_nt(xn, w_ref[Q_DIM:Q_DIM + 2 * KV_DIM, :])
    for c in range(2 * KV_DIM // LANES):
        cs = slice(c * LANES, (c + 1) * LANES)
        cmp_ref[:, cs] = _rope_tok(ycmp[:, cs], cos, sin) if c * LANES < KV_DIM else ycmp[:, cs]

    lane = lax.broadcasted_iota(jnp.int32, (tm, LANES), 1)
    lo = lane < HEAD_DIM
    pos = i * tm + lax.broadcasted_iota(jnp.int32, (tm, LANES), 0)
    sel_bias = jnp.where(lane - HEAD_DIM == pos // SEL_BLOCK, NEG, 0.0)
    for part, ref, fill in ((2, ksa_ref, sel_bias), (4, kwa_ref, 0.0)):
        yk = _dot_nt(xn, w_ref[Q_DIM + part * KV_DIM:Q_DIM + (part + 1) * KV_DIM, :])
        for c in range(KV_DIM // LANES):
            y = _rope_tok(yk[:, c * LANES:(c + 1) * LANES], cos, sin)
            pair = (jnp.where(lo, y, fill).astype(BF16), jnp.where(lo, pltpu.roll(y, HEAD_DIM, 1), fill).astype(BF16))
            for gg in range(2):
                for j in range(tm // tk):
                    ref[2 * c + gg, j] = pair[gg][j * tk:(j + 1) * tk, :]

    yt = _dot_nt(w_ref[Q_DIM:Q_DIM + 6 * KV_DIM, :], xn)
    ones = jnp.ones((HEAD_DIM, tm), F32)
    for part in range(6):
        for g in range(KV_HEADS):
            r0 = part * KV_DIM + g * HEAD_DIM
            y = yt[r0:r0 + HEAD_DIM, :]
            if part % 2 == 0:
                y = _rope_feat(y, cost, sint)
            if part < 4:
                rowst_ref[r0:r0 + HEAD_DIM, :] = y
            else:
                wrowst_ref[r0 - 4 * KV_DIM:r0 - 4 * KV_DIM + HEAD_DIM, :] = y
            if part in (3, 5):
                ref = vsa_ref if part == 3 else vwa_ref
                ya = jnp.concatenate([y, ones], axis=0).astype(BF16)
                for j in range(tm // tk):
                    ref[g, j] = ya[:, j * tk:(j + 1) * tk]


def _nsa_inproj_prompt(x, g, tabs, w, batch, t_seq):
    m, d = x.shape
    tm = min(ROW_TILE, t_seq)
    tk = min(ATT_TILE, t_seq)
    nt = t_seq // tm
    cos, sin, cost, sint = tabs
    row = lambda cols: pl.BlockSpec((tm, cols), lambda b, i: (b * nt + i, 0))
    tok_tab = pl.BlockSpec((tm, LANES), lambda b, i: (i, 0))
    feat_tab = pl.BlockSpec((HEAD_DIM, tm), lambda b, i: (0, i))
    featmaj = lambda rows: pl.BlockSpec((None, rows, tm), lambda b, i: (b, 0, i))
    ktiles = pl.BlockSpec((None, KV_HEADS, tm // tk, tk, LANES), lambda b, i: (b, 0, i, 0, 0))
    vtiles = pl.BlockSpec((None, KV_HEADS, tm // tk, LANES, tk), lambda b, i: (b, 0, i, 0, 0))
    kshape = jax.ShapeDtypeStruct((batch, KV_HEADS, t_seq // tk, tk, LANES), BF16)
    vshape = jax.ShapeDtypeStruct((batch, KV_HEADS, t_seq // tk, LANES, tk), BF16)
    return pl.pallas_call(
        _nsa_inproj_prompt_kernel,
        grid=(batch, nt),
        in_specs=[row(d), _full((1, d)), tok_tab, tok_tab, feat_tab, feat_tab, _full(w.shape)],
        out_specs=[featmaj(Q_DIM), featmaj(LANES), row(2 * KV_DIM), featmaj(4 * KV_DIM), featmaj(2 * KV_DIM),
                   ktiles, vtiles, ktiles, vtiles],
        out_shape=[jax.ShapeDtypeStruct((batch, Q_DIM, t_seq), BF16), jax.ShapeDtypeStruct((batch, LANES, t_seq), F32),
                   jax.ShapeDtypeStruct((m, 2 * KV_DIM), F32),
                   jax.ShapeDtypeStruct((batch, 4 * KV_DIM, t_seq), F32),
                   jax.ShapeDtypeStruct((batch, 2 * KV_DIM, t_seq), F32), kshape, vshape, kshape, vshape],
        compiler_params=_cparams("parallel", "parallel"),
        name="nsa_inproj_prompt",
    )(x, g, cos, sin, cost, sint, w)


def _nsa_inproj_sample_kernel(x_ref, g_ref, cos_ref, sin_ref, w_ref, q_ref, gates_ref, rows_ref, wrows_ref):
    xn = _rms(x_ref[...], g_ref[...]).astype(BF16)
    cos, sin = cos_ref[...], sin_ref[...]
    scale = HEAD_DIM ** -0.5
    yq = _dot_nt(xn, w_ref[0:Q_DIM, :])
    for c in range(Q_DIM // LANES):
        cs = slice(c * LANES, (c + 1) * LANES)
        q_ref[:, cs] = (_rope_tok(yq[:, cs], cos, sin) * scale).astype(BF16)
    g0 = Q_DIM + 6 * KV_DIM
    gates_ref[...] = jax.nn.sigmoid(_dot_nt(xn, w_ref[g0:g0 + LANES, :]))
    ykv = _dot_nt(xn, w_ref[Q_DIM:g0, :])
    for c in range(6 * KV_DIM // LANES):
        cs = slice(c * LANES, (c + 1) * LANES)
        part = c * LANES // KV_DIM
        y = _rope_tok(ykv[:, cs], cos, sin) if part % 2 == 0 else ykv[:, cs]
        if part < 4:
            rows_ref[:, cs] = y
        else:
            wrows_ref[:, c * LANES - 4 * KV_DIM:(c + 1) * LANES - 4 * KV_DIM] = y


def _nsa_inproj_sample(x, g, tabs, w):
    m, d = x.shape
    tm = min(ROW_TILE, m)
    cos, sin = tabs[0], tabs[1]
    n_tab = cos.shape[0] // tm
    row = lambda cols: pl.BlockSpec((tm, cols), lambda i: (i, 0))
    tab = pl.BlockSpec((tm, LANES), lambda i: (i % n_tab, 0))
    return pl.pallas_call(
        _nsa_inproj_sample_kernel,
        grid=(m // tm,),
        in_specs=[row(d), _full((1, d)), tab, tab, _full(w.shape)],
        out_specs=[row(d), row(LANES), row(4 * KV_DIM), row(2 * KV_DIM)],
        out_shape=[jax.ShapeDtypeStruct((m, d), BF16), jax.ShapeDtypeStruct((m, LANES), F32),
                   jax.ShapeDtypeStruct((m, 4 * KV_DIM), F32), jax.ShapeDtypeStruct((m, 2 * KV_DIM), F32)],
        compiler_params=_cparams("parallel"),
        name="nsa_inproj_sample",
    )(x, g, cos, sin, w)


def _compress_rows(load, pe_ref, w1_ref, w2t_ref, kv, n, transposed=False):
    lane = lax.broadcasted_iota(jnp.int32, (n, LANES), 1)
    lo = lane < HEAD_DIM
    acc = jnp.zeros((KV_HEADS * n, w1_ref.shape[-1]), F32)
    lq = 4
    for l0 in range(0, CMP_BLOCK, lq):
        halves = [[], []]
        for l in range(l0, l0 + lq, 2):
            for hf in range(2):
                a = load(l, hf) + pe_ref[kv, l:l + 1, :]
                b = load(l + 1, hf) + pe_ref[kv, l + 1:l + 2, :]
                even = jnp.where(lo, a, pltpu.roll(b, HEAD_DIM, 1))
                odd = jnp.where(lo, pltpu.roll(a, HEAD_DIM, 1), b)
                halves[hf].append((even, odd))
        groups = []
        for g in range(KV_HEADS):
            hf, par = divmod(g, 2)
            groups.append(jnp.concatenate([pc[par] for pc in halves[hf]], axis=1))
        xg = jnp.concatenate(groups, axis=0).astype(BF16)
        acc = acc + _dot(xg, w1_ref[kv, l0 * HEAD_DIM:(l0 + lq) * HEAD_DIM, :])
    h = jax.nn.gelu(acc).astype(BF16)
    return _dot_nt(w2t_ref[kv], h) if transposed else _dot_nt(h, w2t_ref[kv])


def _compress_prompt_kernel(s0_ref, s1_ref, s2_ref, s3_ref, pe_ref, w1_ref, w2t_ref, kc_ref, vct_ref):
    n = kc_ref.shape[0]
    hn = n // 2
    srcs = ((s0_ref, s1_ref), (s2_ref, s3_ref))

    def loader(kv):
        def load(l, hf):
            ev = srcs[kv][hf][pl.ds(l, hn, stride=2 * CMP_BLOCK), :]
            od = srcs[kv][hf][pl.ds(CMP_BLOCK + l, hn, stride=2 * CMP_BLOCK), :]
            return jnp.concatenate([ev, od], axis=0)
        return load

    y = _compress_rows(loader(0), pe_ref, w1_ref, w2t_ref, 0, n)
    yt = _compress_rows(loader(1), pe_ref, w1_ref, w2t_ref, 1, n, transposed=True)
    for g in range(KV_HEADS):
        kc_ref[:, g * HEAD_DIM:(g + 1) * HEAD_DIM] = y[g * n:(g + 1) * n].astype(BF16)
        vct_ref[g * HEAD_DIM:(g + 1) * HEAD_DIM, :] = yt[:, g * n:(g + 1) * n].astype(BF16)


def _compress_prompt(cmp_rows, pe2, w1, w2t, batch, t_seq):
    n = t_seq // CMP_BLOCK
    chunk = lambda c: pl.BlockSpec((t_seq, LANES), lambda b: (b, c))
    return pl.pallas_call(
        _compress_prompt_kernel,
        grid=(batch,),
        in_specs=[chunk(0), chunk(1), chunk(2), chunk(3), _full(pe2.shape), _full(w1.shape), _full(w2t.shape)],
        out_specs=[pl.BlockSpec((None, n, KV_DIM), lambda b: (b, 0, 0)),
                   pl.BlockSpec((None, KV_DIM, n), lambda b: (b, 0, 0))],
        out_shape=[jax.ShapeDtypeStruct((batch, n, KV_DIM), BF16), jax.ShapeDtypeStruct((batch, KV_DIM, n), BF16)],
        compiler_params=_cparams("parallel"),
        name="nsa_compress_prompt",
    )(cmp_rows, cmp_rows, cmp_rows, cmp_rows, pe2, w1, w2t)


def _topk_not_selected(imp_t):
    n_sel, nq = imp_t.shape
    slabs = [imp_t[r0:r0 + SUBLANES] for r0 in range(0, n_sel, SUBLANES)]
    sub = lax.broadcasted_iota(jnp.int32, (SUBLANES, nq), 0)
    ranks = [jnp.zeros((SUBLANES, nq), F32) for _ in slabs]
    for bp in range(n_sel):
        other = imp_t[bp:bp + 1, :]
        for v, slab in enumerate(slabs):
            r0 = v * SUBLANES
            if bp < r0:
                ahead = other >= slab
            elif bp >= r0 + SUBLANES - 1:
                ahead = other > slab
            else:
                ahead = (other > slab) | ((other == slab) & (bp - r0 < sub))
            ranks[v] = ranks[v] + jnp.where(ahead, 1.0, 0.0)
    rank = jnp.concatenate(ranks, axis=0) if len(ranks) > 1 else ranks[0]
    return jnp.where(rank < TOP_N, 0.0, 1.0)


def _nsa_attn_prompt_kernel(qt_ref, gatest_ref, kc_ref, vct_ref, ksa_ref, vsa_ref, kwa_ref, vwa_ref, o_ref,
                            acc_ref, m_ref, sa_ref, sb_ref):
    tq = qt_ref.shape[1]
    tk = ksa_ref.shape[2]
    n_cmp = kc_ref.shape[0]
    n_sel = n_cmp // 2
    rq = Q_PER_KV * tq
    i = pl.program_id(1)
    t0 = i * tq
    t_col = t0 + lax.broadcasted_iota(jnp.int32, (1, rq), 1) % tq
    key_row = lax.broadcasted_iota(jnp.int32, (tk, rq), 0)
    n_wt = -(-(WINDOW - 1) // tk) + 1

    def online_update(s_ref, vt, mask):
        s = s_ref[...]
        if mask is not None:
            s = jnp.where(mask, s, NEG)
        m_old = m_ref[...]
        m_new = jnp.maximum(m_old, jnp.max(s, axis=0, keepdims=True))
        p = jnp.exp(s - m_new)
        acc_ref[...] = acc_ref[...] * jnp.exp(m_old - m_new) + _dot(vt, p.astype(BF16))
        m_ref[...] = m_new

    def normalized(acc):
        return acc[:HEAD_DIM] / jnp.maximum(acc[HEAD_DIM:HEAD_DIM + 1], 1e-30)

    kt_diag = t0 // tk
    for g in range(KV_HEADS):
        qg = jnp.concatenate(
            [qt_ref[(Q_PER_KV * g + r) * HEAD_DIM:(Q_PER_KV * g + r + 1) * HEAD_DIM, :] for r in range(Q_PER_KV)],
            axis=1)

        kts = [kt_diag - j for j in range(n_wt)]
        qw = jnp.concatenate([qg, jnp.zeros_like(qg)], axis=0)
        sw = jnp.concatenate([_dot(kwa_ref[g, jnp.maximum(kt, 0)], qw) for kt in kts], axis=0)
        kpos = jnp.concatenate([kt * tk + key_row for kt in kts], axis=0)
        wmask = (kpos >= 0) & (kpos <= t_col) & (kpos > t_col - WINDOW)
        sw = jnp.where(wmask, sw, NEG)
        pw = jnp.exp(sw - jnp.max(sw, axis=0, keepdims=True)).astype(BF16)
        vw = jnp.concatenate([vwa_ref[g, jnp.maximum(kt, 0)] for kt in kts], axis=1)
        o_w = normalized(_dot(vw, pw))

        sc = _dot(kc_ref[:, g * HEAD_DIM:(g + 1) * HEAD_DIM], qg)
        nrow = lax.broadcasted_iota(jnp.int32, (n_cmp, rq), 0)
        cblk = jnp.where(nrow < n_sel, 2 * nrow, 2 * (nrow - n_sel) + 1)
        cmask = (cblk + 1) * CMP_BLOCK - 1 <= t_col
        sc = jnp.where(cmask, sc, NEG)
        e = jnp.where(cmask, jnp.exp(sc - jnp.max(sc, axis=0, keepdims=True)), 0.0)
        pc = e / jnp.maximum(jnp.sum(e, axis=0, keepdims=True), 1e-30)
        o_c = _dot(vct_ref[g * HEAD_DIM:(g + 1) * HEAD_DIM, :], pc.astype(BF16))

        pq = pc[:, 0:tq]
        for r in range(1, Q_PER_KV):
            pq = pq + pc[:, r * tq:(r + 1) * tq]
        imp = pq[:n_sel] + pq[n_sel:]
        blk = lax.broadcasted_iota(jnp.int32, (n_sel, tq), 0)
        cur = (t0 + lax.broadcasted_iota(jnp.int32, (n_sel, tq), 1)) // SEL_BLOCK
        imp = jnp.where(blk > cur, -1.0, imp)
        imp = jnp.where(blk == 0, FORCED_FIRST, imp)
        imp = jnp.where(blk == cur, FORCED_CUR, imp)
        notsel = _topk_not_selected(imp)
        if n_sel < HEAD_DIM:
            notsel = jnp.concatenate([notsel, jnp.zeros((HEAD_DIM - n_sel, tq), F32)], axis=0)
        qa = jnp.concatenate([qg, jnp.concatenate([notsel.astype(BF16)] * Q_PER_KV, axis=1)], axis=0)

        m_ref[...] = jnp.full((1, rq), NEG, F32)
        acc_ref[...] = jnp.zeros((LANES, rq), F32)
        causal = kt_diag * tk + key_row <= t_col
        sa_ref[...] = _dot(ksa_ref[g, 0], qa)

        def sel_pair(j, carry):
            sb_ref[...] = _dot(ksa_ref[g, 2 * j + 1], qa)
            online_update(sa_ref, vsa_ref[g, 2 * j], None)
            sa_ref[...] = _dot(ksa_ref[g, 2 * j + 2], qa)
            online_update(sb_ref, vsa_ref[g, 2 * j + 1], None)
            return carry

        lax.fori_loop(0, kt_diag // 2, sel_pair, 0)

        @pl.when(kt_diag % 2 == 1)
        def _():
            sb_ref[...] = _dot(ksa_ref[g, kt_diag], qa)
            online_update(sa_ref, vsa_ref[g, kt_diag - 1], None)
            online_update(sb_ref, vsa_ref[g, kt_diag], causal)

        @pl.when(kt_diag % 2 == 0)
        def _():
            online_update(sa_ref, vsa_ref[g, kt_diag], causal)

        o_s = normalized(acc_ref[...])

        outs = []
        for r in range(Q_PER_KV):
            h = Q_PER_KV * g + r
            cs = slice(r * tq, (r + 1) * tq)
            gc = gatest_ref[N_BRANCH * h:N_BRANCH * h + 1, :]
            gs = gatest_ref[N_BRANCH * h + 1:N_BRANCH * h + 2, :]
            gw = gatest_ref[N_BRANCH * h + 2:N_BRANCH * h + 3, :]
            outs.append(gc * o_c[:, cs] + gs * o_s[:, cs] + gw * o_w[:, cs])
        for pr in range(Q_PER_KV // 2):
            c0 = (Q_PER_KV * g + 2 * pr) * HEAD_DIM
            o_ref[:, c0:c0 + LANES] = jnp.concatenate(outs[2 * pr:2 * pr + 2], axis=0).T.astype(BF16)


def _nsa_attn_prompt(qt, gatest, kc, vct, ksa, vsa, kwa, vwa, batch, t_seq):
    d = qt.shape[1]
    tq = min(ATT_TILE, t_seq)
    nq = t_seq // tq
    n_cmp = t_seq // CMP_BLOCK
    featmaj = lambda rows: pl.BlockSpec((None, rows, tq), lambda b, i: (b, 0, i))
    seq = lambda shape: pl.BlockSpec((None,) + shape[1:], lambda b, i: (b, 0, 0, 0, 0))
    return pl.pallas_call(
        _nsa_attn_prompt_kernel,
        grid=(batch, nq),
        in_specs=[featmaj(d), featmaj(LANES),
                  pl.BlockSpec((None, n_cmp, KV_DIM), lambda b, i: (b, 0, 0)),
                  pl.BlockSpec((None, KV_DIM, n_cmp), lambda b, i: (b, 0, 0)),
                  seq(ksa.shape), seq(vsa.shape), seq(kwa.shape), seq(vwa.shape)],
        out_specs=pl.BlockSpec((tq, d), lambda b, i: (b * nq + i, 0)),
        out_shape=jax.ShapeDtypeStruct((batch * t_seq, d), BF16),
        scratch_shapes=[pltpu.VMEM((LANES, Q_PER_KV * tq), F32), pltpu.VMEM((1, Q_PER_KV * tq), F32),
                        pltpu.VMEM((ksa.shape[3], Q_PER_KV * tq), F32), pltpu.VMEM((ksa.shape[3], Q_PER_KV * tq), F32)],
        compiler_params=_cparams("parallel", "arbitrary"),
        name="nsa_attn_prompt",
    )(qt, gatest, kc, vct, ksa, vsa, kwa, vwa)


def _compress_sample_kernel(pt_ref, *refs, n_pages, page, t_new, n_cmp):
    del pt_ref
    pages = refs[:n_pages]
    new_ref, pe_ref, w1_ref, w2t_ref, kc_ref, vc_ref = refs[n_pages:n_pages + 6]
    bufs = refs[n_pages + 6:]
    n_pad = bufs[0].shape[0] // CMP_BLOCK
    past = n_pages * page
    for c, buf_ref in enumerate(bufs):
        cs = slice(c * LANES, (c + 1) * LANES)
        for p in range(n_pages):
            buf_ref[p * page:(p + 1) * page, :] = pages[p][cs, :].T
        buf_ref[past:past + t_new, :] = new_ref[:, cs]
        buf_ref[past + t_new:, :] = jnp.zeros((buf_ref.shape[0] - past - t_new, LANES), F32)
    n_out = kc_ref.shape[0]
    for kv, out_ref in ((0, kc_ref), (1, vc_ref)):
        def load(l, hf, kv=kv):
            return bufs[2 * kv + hf][pl.ds(l, n_pad, stride=CMP_BLOCK), :]
        y = _compress_rows(load, pe_ref, w1_ref, w2t_ref, kv, n_pad)
        valid = lax.broadcasted_iota(jnp.int32, (n_pad, HEAD_DIM), 0) < n_cmp
        for g in range(KV_HEADS):
            out_ref[0:n_pad, g * HEAD_DIM:(g + 1) * HEAD_DIM] = jnp.where(
                valid, y[g * n_pad:(g + 1) * n_pad], 0.0).astype(BF16)
        out_ref[n_pad:, :] = jnp.zeros((n_out - n_pad, KV_DIM), BF16)


def _page_specs(layer, n_pages, page, feat_block):
    half = 2 * KV_DIM
    return [pl.BlockSpec((None, None, half, page),
                         functools.partial(lambda b, pt, p: (layer, pt[b, p], feat_block, 0), p=p))
            for p in range(n_pages)]


def _compress_sample(page_table, cache_t, layer, rows_new, pe2, w1, w2t, t_new):
    dec_b, n_pages = page_table.shape
    page = cache_t.shape[-1]
    past = n_pages * page
    l_pad = -(-(past + t_new) // SEL_BLOCK) * SEL_BLOCK
    n_cmp = l_pad // CMP_BLOCK
    n_pad = -(-n_cmp // SUBLANES) * SUBLANES
    n_out = LANES
    half = 2 * KV_DIM
    kern = functools.partial(_compress_sample_kernel, n_pages=n_pages, page=page, t_new=t_new, n_cmp=n_cmp)
    grid_spec = pltpu.PrefetchScalarGridSpec(
        num_scalar_prefetch=1,
        grid=(dec_b,),
        in_specs=_page_specs(layer, n_pages, page, 0) + [
            pl.BlockSpec((t_new, half), lambda b, pt: (b, 0)),
            pl.BlockSpec(pe2.shape, lambda b, pt: (0, 0, 0)),
            pl.BlockSpec(w1.shape, lambda b, pt: (0, 0, 0)),
            pl.BlockSpec(w2t.shape, lambda b, pt: (0, 0, 0))],
        out_specs=[pl.BlockSpec((None, n_out, KV_DIM), lambda b, pt: (b, 0, 0))] * 2,
        scratch_shapes=[pltpu.VMEM((n_pad * CMP_BLOCK, LANES), F32)] * (half // LANES),
    )
    return pl.pallas_call(
        kern,
        grid_spec=grid_spec,
        out_shape=[jax.ShapeDtypeStruct((dec_b, n_out, KV_DIM), BF16)] * 2,
        compiler_params=_cparams("arbitrary"),
        name="nsa_compress_sample",
    )(page_table, *([cache_t] * n_pages), rows_new, pe2, w1, w2t), n_cmp


def _nsa_attn_sample_kernel(pt_ref, *refs, n_pages, page, t_new, n_cmp):
    del pt_ref
    pages = refs[:n_pages]
    (kc_ref, vc_ref, q_ref, new_ref, wnew_ref, win_ref, gates_ref,
     o_ref, nwin_ref, kst_ref, vst_ref) = refs[n_pages:]
    past = n_pages * page
    n_buf = win_ref.shape[1]
    n_sel = n_cmp // 2
    gq = Q_PER_KV * t_new
    rq = KV_HEADS * gq
    nk = past + LANES
    nw = n_buf + LANES

    for p in range(n_pages):
        blk = pages[p][...]
        kst_ref[:, p * page:(p + 1) * page] = blk[:KV_DIM].astype(BF16)
        vst_ref[:, p * page:(p + 1) * page] = blk[KV_DIM:].astype(BF16)
    pad = jnp.zeros((LANES - t_new, KV_DIM), F32)
    new = new_ref[...]
    wnew = wnew_ref[...]
    knew = jnp.concatenate([new[:, :KV_DIM], pad], axis=0).astype(BF16)
    vnew = jnp.concatenate([new[:, KV_DIM:], pad], axis=0).astype(BF16)
    kwnew = jnp.concatenate([wnew[:, :KV_DIM], pad], axis=0).astype(BF16)
    vwnew = jnp.concatenate([wnew[:, KV_DIM:], pad], axis=0).astype(BF16)
    win = win_ref[...]
    kwt = win[:KV_DIM].astype(BF16)
    vwt = win[KV_DIM:].astype(BF16)

    rolled = pltpu.roll(win, n_buf - t_new, 1)
    wnew_t = jnp.concatenate([jnp.zeros((LANES - t_new, 2 * KV_DIM), F32), wnew], axis=0).T
    tail_lane = lax.broadcasted_iota(jnp.int32, (2 * KV_DIM, LANES), 1)
    nwin_ref[:, 0:n_buf - LANES] = rolled[:, 0:n_buf - LANES]
    nwin_ref[:, n_buf - LANES:] = jnp.where(tail_lane >= LANES - t_new, wnew_t, rolled[:, n_buf - LANES:])

    q = q_ref[...]
    lane_g = lax.broadcasted_iota(jnp.int32, (t_new, KV_DIM), 1) // HEAD_DIM
    pieces = []
    for g in range(KV_HEADS):
        for j in range(Q_PER_KV):
            qj = jnp.concatenate([q[:, (Q_PER_KV * gg + j) * HEAD_DIM:(Q_PER_KV * gg + j + 1) * HEAD_DIM]
                                  for gg in range(KV_HEADS)], axis=1)
            pieces.append(jnp.where(lane_g == g, qj, jnp.zeros_like(qj)))
    qbd = jnp.concatenate(pieces, axis=0)
    t_row = past + lax.broadcasted_iota(jnp.int32, (rq, 1), 0) % t_new

    def softmax(s, mask):
        s = jnp.where(mask, s, NEG)
        e = jnp.where(mask, jnp.exp(s - jnp.max(s, axis=1, keepdims=True)), 0.0)
        return e / jnp.maximum(jnp.sum(e, axis=1, keepdims=True), 1e-30)

    ncl = kc_ref.shape[0]
    sc = _dot_nt(qbd, kc_ref[...])
    cidx = lax.broadcasted_iota(jnp.int32, (rq, ncl), 1)
    pc = softmax(sc, (cidx < n_cmp) & ((cidx + 1) * CMP_BLOCK - 1 <= t_row))
    oc = _dot(pc.astype(BF16), vc_ref[...])

    groups = []
    for g in range(KV_HEADS):
        pg = pc[g * gq:g * gq + t_new]
        for j in range(1, Q_PER_KV):
            pg = pg + pc[g * gq + j * t_new:g * gq + (j + 1) * t_new]
        groups.append(pg)
    pg = jnp.concatenate(groups, axis=0)
    nr = KV_HEADS * t_new
    imp = pg + pltpu.roll(pg, ncl - 1, 1)
    lane = lax.broadcasted_iota(jnp.int32, (nr, ncl), 1)
    blk = lane // 2
    cur = (past + lax.broadcasted_iota(jnp.int32, (nr, ncl), 0) % t_new) // SEL_BLOCK
    imp = jnp.where(blk > cur, -1.0, imp)
    imp = jnp.where(blk == 0, FORCED_FIRST, imp)
    imp = jnp.where(blk == cur, FORCED_CUR, imp)
    rank = jnp.zeros((nr, ncl), jnp.int32)
    for bp in range(n_sel):
        other = imp[:, 2 * bp:2 * bp + 1]
        rank = rank + ((other > imp) | ((other == imp) & (bp < blk))).astype(jnp.int32)
    sel = jnp.where((rank < TOP_N) & (lane % 2 == 0) & (blk < n_sel), 1.0, 0.0)
    sel_rows = jnp.concatenate([sel[g * t_new:(g + 1) * t_new] for g in range(KV_HEADS) for _ in range(Q_PER_KV)],
                               axis=0).astype(BF16)
    expand = jnp.where(lax.broadcasted_iota(jnp.int32, (ncl, nk), 0)
                       == 2 * (lax.broadcasted_iota(jnp.int32, (ncl, nk), 1) // SEL_BLOCK), 1.0, 0.0).astype(BF16)
    selmask = _dot(sel_rows, expand) > 0.5

    kpos = lax.broadcasted_iota(jnp.int32, (rq, nk), 1)
    ss = jnp.concatenate([_dot(qbd, kst_ref[...]), _dot_nt(qbd, knew)], axis=1)
    ps = softmax(ss, selmask & (kpos <= t_row)).astype(BF16)
    os_ = _dot_nt(ps[:, :past], vst_ref[...]) + _dot(ps[:, past:], vnew)

    wpos = past - n_buf + lax.broadcasted_iota(jnp.int32, (rq, nw), 1)
    sw = jnp.concatenate([_dot(qbd, kwt), _dot_nt(qbd, kwnew)], axis=1)
    pw = softmax(sw, (wpos <= t_row) & (wpos > t_row - WINDOW)).astype(BF16)
    ow = _dot_nt(pw[:, :n_buf], vwt) + _dot(pw[:, n_buf:], vwnew)

    gates = gates_ref[...]
    for g in range(KV_HEADS):
        for j in range(Q_PER_KV):
            h = Q_PER_KV * g + j
            rs = slice(g * gq + j * t_new, g * gq + (j + 1) * t_new)
            cs = slice(g * HEAD_DIM, (g + 1) * HEAD_DIM)
            gc = gates[:, N_BRANCH * h:N_BRANCH * h + 1]
            gs = gates[:, N_BRANCH * h + 1:N_BRANCH * h + 2]
            gw = gates[:, N_BRANCH * h + 2:N_BRANCH * h + 3]
            o_ref[:, h * HEAD_DIM:(h + 1) * HEAD_DIM] = (
                gc * oc[rs, cs] + gs * os_[rs, cs] + gw * ow[rs, cs]).astype(BF16)


def _nsa_attn_sample(page_table, cache_t, win_t, layer, kc, vc, q, rows_new, wrows_new, gates, t_new, n_cmp):
    dec_b, n_pages = page_table.shape
    page = cache_t.shape[-1]
    past = n_pages * page
    n_buf = win_t.shape[-1]
    half = 2 * KV_DIM
    d = q.shape[1]
    ncl = kc.shape[1]
    kern = functools.partial(_nsa_attn_sample_kernel, n_pages=n_pages, page=page, t_new=t_new, n_cmp=n_cmp)
    grid_spec = pltpu.PrefetchScalarGridSpec(
        num_scalar_prefetch=1,
        grid=(dec_b,),
        in_specs=_page_specs(layer, n_pages, page, 1) + [
            pl.BlockSpec((None, ncl, KV_DIM), lambda b, pt: (b, 0, 0)),
            pl.BlockSpec((None, ncl, KV_DIM), lambda b, pt: (b, 0, 0)),
            pl.BlockSpec((t_new, d), lambda b, pt: (b, 0)),
            pl.BlockSpec((t_new, half), lambda b, pt: (b, 1)),
            pl.BlockSpec((t_new, half), lambda b, pt: (b, 0)),
            pl.BlockSpec((None, None, half, n_buf), lambda b, pt: (layer, b, 0, 0)),
            pl.BlockSpec((t_new, LANES), lambda b, pt: (b, 0))],
        out_specs=[pl.BlockSpec((t_new, d), lambda b, pt: (b, 0)),
                   pl.BlockSpec((None, half, n_buf), lambda b, pt: (b, 0, 0))],
        scratch_shapes=[pltpu.VMEM((KV_DIM, past), BF16), pltpu.VMEM((KV_DIM, past), BF16)],
    )
    return pl.pallas_call(
        kern,
        grid_spec=grid_spec,
        out_shape=[jax.ShapeDtypeStruct((dec_b * t_new, d), BF16), jax.ShapeDtypeStruct((dec_b, half, n_buf), F32)],
        compiler_params=_cparams("arbitrary"),
        name="nsa_attn_sample",
    )(page_table, *([cache_t] * n_pages), kc, vc, q, rows_new, wrows_new, win_t, gates)


def _outproj_kernel(x_ref, o_ref, w_ref, y_ref):
    y_ref[...] = x_ref[...] + _dot(o_ref[...], w_ref[...])


def _outproj(x, o, w):
    m, d = x.shape
    tm = min(ROW_TILE, m)
    row = pl.BlockSpec((tm, d), lambda i: (i, 0))
    return pl.pallas_call(
        _outproj_kernel,
        grid=(m // tm,),
        in_specs=[row, row, _full(w.shape)],
        out_specs=row,
        out_shape=jax.ShapeDtypeStruct((m, d), F32),
        compiler_params=_cparams("parallel"),
        name="mixer_outproj",
    )(x, o, w)


def _mlp_kernel(x_ref, g_ref, w1_ref, w2_ref, y_ref, xn_ref, acc_ref):
    j = pl.program_id(1)

    @pl.when(j == 0)
    def _():
        x = x_ref[...]
        xn_ref[...] = _rms(x, g_ref[...]).astype(BF16)
        acc_ref[...] = x

    h = jnp.maximum(_dot(xn_ref[...], w1_ref[...]), 0.0)
    acc_ref[...] += _dot((h * h).astype(BF16), w2_ref[...])

    @pl.when(j == pl.num_programs(1) - 1)
    def _():
        y_ref[...] = acc_ref[...]


def _mlp(x, g, w1, w2):
    m, d = x.shape
    ff = w1.shape[1]
    tm, tf = min(ROW_TILE, m), 1024
    row = pl.BlockSpec((tm, d), lambda i, j: (i, 0))
    return pl.pallas_call(
        _mlp_kernel,
        grid=(m // tm, ff // tf),
        in_specs=[row, pl.BlockSpec((1, d), lambda i, j: (0, 0)),
                  pl.BlockSpec((d, tf), lambda i, j: (0, j)), pl.BlockSpec((tf, d), lambda i, j: (j, 0))],
        out_specs=row,
        out_shape=jax.ShapeDtypeStruct((m, d), F32),
        scratch_shapes=[pltpu.VMEM((tm, d), BF16), pltpu.VMEM((tm, d), F32)],
        compiler_params=_cparams("parallel", "arbitrary"),
        name="sqrelu_mlp",
    )(x, g, w1, w2)


def _ple_kernel(x_ref, p_ref, g_ref, wg_ref, wp_ref, gf_ref, y_ref, *, final_norm):
    x = x_ref[...]
    gate = jax.nn.sigmoid(_dot(_rms(x, g_ref[...]).astype(BF16), wg_ref[...]))
    y = x + _dot(p_ref[...].astype(BF16), wp_ref[...]) * gate
    if final_norm:
        y = _rms(y, gf_ref[...])
    y_ref[...] = y


def _ple(x, p_all, layer, g, wg, wp, gf, final_norm):
    m, d = x.shape
    tm = min(ROW_TILE, m)
    row = pl.BlockSpec((tm, d), lambda i: (i, 0))
    return pl.pallas_call(
        functools.partial(_ple_kernel, final_norm=final_norm),
        grid=(m // tm,),
        in_specs=[row, pl.BlockSpec((None, tm, p_all.shape[-1]), lambda i: (layer, i, 0)), _full((1, d)),
                  _full(wg.shape), _full(wp.shape), _full((1, d))],
        out_specs=row,
        out_shape=jax.ShapeDtypeStruct((m, d), F32),
        compiler_params=_cparams("parallel"),
        name="ple_gate",
    )(x, p_all, g, wg, wp, gf)


def _hg_inproj_kernel(x_ref, g_ref, lbl_ref, w_ref, q_ref, f_ref, k_ref, v_ref, gs_ref, xn_ref, *, layer):
    j = pl.program_id(1)

    @pl.when(j == 0)
    def _():
        xn_ref[...] = _rms(x_ref[...], g_ref[...]).astype(BF16)

    y = _dot(xn_ref[...], w_ref[...])

    def put(ref, val):
        for h in range(HG_HEADS):
            ref[h] = val[:, h * HG_DK:(h + 1) * HG_DK]

    @pl.when(j == 0)
    def _():
        put(q_ref, jax.nn.silu(y) * HG_DK ** -0.5)

    @pl.when(j == 1)
    def _():
        lg = lbl_ref[...]
        e = jnp.exp(lg - jnp.max(lg, axis=0, keepdims=True))
        w = e / jnp.sum(e, axis=0, keepdims=True)
        cs = w[0:1]
        for r in range(1, layer + 1):
            cs = cs + w[r:r + 1]
        lb = cs - w[0:1]
        put(f_ref, jnp.log(lb + (1.0 - lb) * jax.nn.sigmoid(y)))
        put(k_ref, (1.0 - lb) * jax.nn.sigmoid(-y))

    @pl.when(j == 2)
    def _():
        put(v_ref, y)

    @pl.when(j == 3)
    def _():
        put(gs_ref, jax.nn.silu(y))


def _hg_inproj(x, g, lb_logits, w, layer):
    m, d = x.shape
    tm = min(ROW_TILE, m)
    head_major = pl.BlockSpec((HG_HEADS, tm, HG_DK), lambda i, j: (0, i, 0))
    return pl.pallas_call(
        functools.partial(_hg_inproj_kernel, layer=layer),
        grid=(m // tm, 4),
        in_specs=[pl.BlockSpec((tm, d), lambda i, j: (i, 0)), pl.BlockSpec((1, d), lambda i, j: (0, 0)),
                  pl.BlockSpec(lb_logits.shape, lambda i, j: (0, 0)), pl.BlockSpec((d, d), lambda i, j: (0, j))],
        out_specs=[head_major] * 5,
        out_shape=[jax.ShapeDtypeStruct((HG_HEADS, m, HG_DK), F32)] * 5,
        scratch_shapes=[pltpu.VMEM((tm, d), BF16)],
        compiler_params=_cparams("parallel", "arbitrary"),
        name="hgrn_inproj",
    )(x, g, lb_logits, w)


def _cumsum_rows(f):
    c = f.shape[0]
    row = lax.broadcasted_iota(jnp.int32, f.shape, 0)
    s = 1
    while s < c:
        f = f + jnp.where(row >= s, pltpu.roll(f, s, 0), 0.0)
        s *= 2
    return f


def _hg_intra(q, k, v, b, o, sub):
    c = q.shape[0]
    ones = jnp.ones((HG_DK, HG_DV), BF16)
    trow = lax.broadcasted_iota(jnp.int32, (sub, HG_DK), 0)
    terms = []
    for r0 in range(0, c, sub):
        qi, ki, bi = q[r0:r0 + sub], k[r0:r0 + sub], b[r0:r0 + sub]
        for s in range(sub):
            causal = trow >= s
            dec = jnp.exp(jnp.where(causal, bi - bi[s:s + 1], 0.0))
            terms.append(jnp.where(causal, qi * ki[s:s + 1] * dec, 0.0).astype(BF16))
    a = _dot(jnp.concatenate(terms, axis=0), ones)
    outs = []
    for r0 in range(0, c, sub):
        oi = o[r0:r0 + sub]
        if r0 > 0:
            bs = b[r0 - 1:r0]
            qp = (q[r0:r0 + sub] * jnp.exp(b[r0:r0 + sub] - bs)).astype(BF16)
            kp = (k[:r0] * jnp.exp(bs - b[:r0])).astype(BF16)
            oi = oi + _dot(_dot_nt(qp, kp).astype(BF16), v[:r0].astype(BF16))
        for s in range(sub):
            oi = oi + a[(r0 + s) * sub:(r0 + s + 1) * sub] * v[r0 + s:r0 + s + 1]
        outs.append(oi)
    return jnp.concatenate(outs, axis=0) if len(outs) > 1 else outs[0]


def _hg_out(o, ng, gs):
    return (_rms(o, ng) * gs).astype(BF16)


def _hg_scan_prompt_kernel(q_ref, f_ref, k_ref, v_ref, gs_ref, ng_ref, o_ref, s_ref, st_ref):
    c = pl.program_id(1)
    rows = q_ref.shape[1]

    @pl.when(c == 0)
    def _():
        st_ref[...] = jnp.zeros(st_ref.shape, F32)

    def one_head(h):
        st = st_ref[h]
        for r0 in range(0, rows, HG_CHUNK):
            rs = pl.ds(r0, HG_CHUNK)
            q, k, v = q_ref[h, rs, :], k_ref[h, rs, :], v_ref[h, rs, :]
            b = _cumsum_rows(f_ref[h, rs, :])
            o = _dot_nt((q * jnp.exp(b)).astype(BF16), st.astype(BF16))
            o = _hg_intra(q, k, v, b, o, HG_SUB)
            o_ref[h, rs, :] = _hg_out(o, ng_ref[...], gs_ref[h, rs, :])
            bl = b[HG_CHUNK - 1:HG_CHUNK]
            kd = (k * jnp.exp(bl - b)).astype(BF16)
            st = st * jnp.exp(bl) + lax.dot_general(v.astype(BF16), kd, (((0,), (0,)), ((), ())),
                                                    preferred_element_type=F32)
        st_ref[h] = st

        @pl.when(c == pl.num_programs(1) - 1)
        def _():
            s_ref[h] = st.T

    def heads(it, carry):
        for hh in range(HG_UNROLL):
            one_head(it * HG_UNROLL + hh)
        return carry

    lax.fori_loop(0, HG_HEADS // HG_UNROLL, heads, 0)


def _hg_scan_prompt(q, f, k, v, gs, ng, batch, t_seq):
    m = q.shape[1]
    rows = 2 * HG_CHUNK
    nc = t_seq // rows
    blk = pl.BlockSpec((HG_HEADS, rows, HG_DK), lambda b, c: (0, b * nc + c, 0))
    return pl.pallas_call(
        _hg_scan_prompt_kernel,
        grid=(batch, nc),
        in_specs=[blk] * 5 + [pl.BlockSpec((1, HG_DV), lambda b, c: (0, 0))],
        out_specs=[blk, pl.BlockSpec((None, HG_HEADS, HG_DK, HG_DV), lambda b, c: (b, 0, 0, 0))],
        out_shape=[jax.ShapeDtypeStruct((HG_HEADS, m, HG_DV), BF16),
                   jax.ShapeDtypeStruct((batch, HG_HEADS, HG_DK, HG_DV), F32)],
        scratch_shapes=[pltpu.VMEM((HG_HEADS, HG_DV, HG_DK), F32)],
        compiler_params=_cparams("parallel", "arbitrary"),
        name="hgrn_scan_prompt",
    )(q, f, k, v, gs, ng)


def _hg_step_sample_kernel(q_ref, f_ref, k_ref, v_ref, gs_ref, ng_ref, s0_ref, o_ref, s1_ref, *, t_new):
    n_seq = s0_ref.shape[0]

    def one(sq, h):
        rs = pl.ds(pl.multiple_of(sq * t_new, t_new), t_new)
        q, k, v = q_ref[h, rs, :], k_ref[h, rs, :], v_ref[h, rs, :]
        b = _cumsum_rows(f_ref[h, rs, :])
        s0 = s0_ref[sq, h]
        o = _dot((q * jnp.exp(b)).astype(BF16), s0.astype(BF16))
        o = _hg_intra(q, k, v, b, o, t_new)
        o_ref[h, rs, :] = _hg_out(o, ng_ref[...], gs_ref[h, rs, :])
        bl = b[t_new - 1:t_new]
        kd = k * jnp.exp(bl - b)
        ext = jnp.concatenate([kd, jnp.broadcast_to(jnp.exp(bl), (t_new, HG_DK)),
                               jnp.zeros((HG_DK - 2 * t_new, HG_DK), F32)], axis=0).T
        s1_ref[sq, h] = s0 * ext[:, t_new:t_new + 1] + _dot(ext[:, :t_new].astype(BF16), v.astype(BF16))

    def body(idx, carry):
        per_seq = HG_HEADS // HG_UNROLL
        for hh in range(HG_UNROLL):
            one(idx // per_seq, (idx % per_seq) * HG_UNROLL + hh)
        return carry

    lax.fori_loop(0, n_seq * (HG_HEADS // HG_UNROLL), body, 0)


def _hg_step_sample(q, f, k, v, gs, ng, s0_all, layer, t_new):
    m = q.shape[1]
    dec_b = s0_all.shape[1]
    sb = min(8, dec_b)
    blk = pl.BlockSpec((HG_HEADS, sb * t_new, HG_DK), lambda i: (0, i, 0))
    st_in = pl.BlockSpec((None, sb, HG_HEADS, HG_DK, HG_DV), lambda i: (layer, i, 0, 0, 0))
    st_out = pl.BlockSpec((sb, HG_HEADS, HG_DK, HG_DV), lambda i: (i, 0, 0, 0))
    return pl.pallas_call(
        functools.partial(_hg_step_sample_kernel, t_new=t_new),
        grid=(dec_b // sb,),
        in_specs=[blk] * 5 + [pl.BlockSpec((1, HG_DV), lambda i: (0, 0)), st_in],
        out_specs=[blk, st_out],
        out_shape=[jax.ShapeDtypeStruct((HG_HEADS, m, HG_DV), BF16), jax.ShapeDtypeStruct(s0_all.shape[1:], F32)],
        compiler_params=_cparams("parallel"),
        name="hgrn_step_sample",
    )(q, f, k, v, gs, ng, s0_all)


def _hg_outproj_kernel(x_ref, o_ref, w_ref, y_ref):
    o = jnp.concatenate([o_ref[h] for h in range(HG_HEADS)], axis=1)
    y_ref[...] = x_ref[...] + _dot(o, w_ref[...])


def _hg_outproj(x, o, w):
    m, d = x.shape
    tm = min(ROW_TILE, m)
    row = pl.BlockSpec((tm, d), lambda i: (i, 0))
    return pl.pallas_call(
        _hg_outproj_kernel,
        grid=(m // tm,),
        in_specs=[row, pl.BlockSpec((HG_HEADS, tm, HG_DV), lambda i: (0, i, 0)), _full(w.shape)],
        out_specs=row,
        out_shape=jax.ShapeDtypeStruct((m, d), F32),
        compiler_params=_cparams("parallel"),
        name="hgrn_outproj",
    )(x, o, w)


def _feature_major_view(a):
    lead = a.shape[:-4]
    t, c, g, d = a.shape[-4:]
    n = len(lead)
    return jnp.transpose(a, tuple(range(n)) + (n + 1, n + 2, n + 3, n)).reshape(lead + (c * g * d, t))


def _token_major_view(a, c):
    lead = a.shape[:-2]
    t = a.shape[-1]
    n = len(lead)
    a = a.reshape(lead + (c, KV_HEADS, HEAD_DIM, t))
    return jnp.transpose(a, tuple(range(n)) + (n + 3, n, n + 1, n + 2))


def kernel(x_prompt, x_sample, cache_nsa_kv, state_nsa_win, state_hgrn, page_table, p_prompt, p_sample, norm_mix, norm_mlp, norm_ple, norm_final, nsa_w_in, nsa_cmp_pe, nsa_cmp_w1, nsa_cmp_w2, nsa_w_out, hg_w_in, hg_lb_logits, hg_norm, hg_w_out, mlp_w1, mlp_w2, ple_w_proj, ple_w_gate):
    batch, t_p, d = x_prompt.shape
    dec_b, t_s, _ = x_sample.shape
    depth = p_prompt.shape[0]
    page = cache_nsa_kv.shape[2]
    past = page_table.shape[1] * page

    xp = x_prompt.reshape(batch * t_p, d)
    xs = x_sample.reshape(dec_b * t_s, d)
    pp = p_prompt.reshape(depth, batch * t_p, -1)
    ps = p_sample.reshape(depth, dec_b * t_s, -1)
    row = lambda a: a.reshape(1, -1)
    cache_t = _feature_major_view(cache_nsa_kv)
    win_t = _feature_major_view(state_nsa_win)
    wl = min(WINDOW, t_p)

    tabs_p = _rope_tables(t_p, 0, t_p)
    tabs_s = _rope_tables(min(ROW_TILE, dec_b * t_s), past, t_s)

    kv_p, kv_s, win_p, win_s, st_p, st_s = [], [], [], [], [], []
    for i in range(depth):
        g_mix = row(norm_mix[i])
        if i % 2 == 0:
            a = i // 2
            n_in = nsa_w_in.shape[2]
            w_t = jnp.pad(jnp.transpose(nsa_w_in[a]).astype(BF16),
                          ((0, Q_DIM + 6 * KV_DIM + LANES - n_in), (0, 0)))
            pe2 = jnp.tile(nsa_cmp_pe[a], (1, 1, LANES // HEAD_DIM))
            cw1 = nsa_cmp_w1[a].reshape(2, CMP_BLOCK * HEAD_DIM, -1).astype(BF16)
            cw2t = jnp.swapaxes(nsa_cmp_w2[a], 1, 2).astype(BF16)
            w_out = nsa_w_out[a].astype(BF16)

            qt, gatest, cmp_rows, rows_t, wrows_t, ksa, vsa, kwa, vwa = _nsa_inproj_prompt(
                xp, g_mix, tabs_p, w_t, batch, t_p)
            kc, vct = _compress_prompt(cmp_rows, pe2, cw1, cw2t, batch, t_p)
            o = _nsa_attn_prompt(qt, gatest, kc, vct, ksa, vsa, kwa, vwa, batch, t_p)
            xp = _outproj(xp, o, w_out)
            kv_p.append(rows_t)
            win_p.append(wrows_t[:, :, t_p - wl:])

            q, gates, rows, wrows = _nsa_inproj_sample(xs, g_mix, tabs_s, w_t)
            (kc, vc), n_cmp = _compress_sample(page_table, cache_t, a, rows, pe2, cw1, cw2t, t_s)
            o, new_win_t = _nsa_attn_sample(page_table, cache_t, win_t, a, kc, vc, q, rows, wrows, gates, t_s, n_cmp)
            xs = _outproj(xs, o, w_out)
            kv_s.append(rows.reshape(dec_b, t_s, 4, KV_HEADS, HEAD_DIM))
            win_s.append(new_win_t)
        else:
            r = i // 2
            w_in = hg_w_in[r].astype(BF16)
            w_out = hg_w_out[r].astype(BF16)
            ng = row(hg_norm[r])
            q, f, k, v, gs = _hg_inproj(xp, g_mix, hg_lb_logits, w_in, r)
            o, s_new = _hg_scan_prompt(q, f, k, v, gs, ng, batch, t_p)
            xp = _hg_outproj(xp, o, w_out)
            st_p.append(s_new)
            q, f, k, v, gs = _hg_inproj(xs, g_mix, hg_lb_logits, w_in, r)
            o, s_new = _hg_step_sample(q, f, k, v, gs, ng, state_hgrn, r, t_s)
            xs = _hg_outproj(xs, o, w_out)
            st_s.append(s_new)

        w1 = mlp_w1[i].astype(BF16)
        w2 = mlp_w2[i].astype(BF16)
        wgate = ple_w_gate[i].astype(BF16)
        wproj = ple_w_proj[i].astype(BF16)
        last = i == depth - 1
        xp = _mlp(xp, row(norm_mlp[i]), w1, w2)
        xs = _mlp(xs, row(norm_mlp[i]), w1, w2)
        xp = _ple(xp, pp, i, row(norm_ple[i]), wgate, wproj, row(norm_final), last)
        xs = _ple(xs, ps, i, row(norm_ple[i]), wgate, wproj, row(norm_final), last)

    return (xp.reshape(batch, t_p, d), xs.reshape(dec_b, t_s, d),
            _token_major_view(jnp.stack(kv_p), 4), jnp.stack(kv_s),
            _token_major_view(jnp.stack(win_p), 2), _token_major_view(jnp.stack(win_s), 2),
            jnp.stack(st_p), jnp.stack(st_s))
```

```python
import functools

import jax
import jax.numpy as jnp
from jax import lax
from jax.experimental import pallas as pl
from jax.experimental.pallas import tpu as pltpu

F32 = jnp.float32
BF16 = jnp.bfloat16

RMS_EPS = 1e-6
ROPE_THETA = 10000.0
N_HEADS = 16
HEAD_DIM = 64
KV_HEADS = 4
Q_PER_KV = N_HEADS // KV_HEADS
Q_DIM = N_HEADS * HEAD_DIM
KV_DIM = KV_HEADS * HEAD_DIM
N_BRANCH = 3
CMP_BLOCK = 32
SEL_BLOCK = 64
TOP_N = 16
WINDOW = 512
FORCED_CUR = 3e4
FORCED_FIRST = 2e4
HG_HEADS = 8
HG_DK = 128
HG_DV = 128
HG_CHUNK = 64
HG_SUB = 16
HG_UNROLL_PROMPT = 4
HG_UNROLL_SAMPLE = 8
NEG = -1e30
LOG2_E = 1.4426950408889634

LANES = 128
SUBLANES = 8
ROW_TILE = 512
ATT_TILE = 256
VMEM_LIMIT = 56 * 1024 * 1024


def _cparams(*sem):
    return pltpu.CompilerParams(dimension_semantics=sem, vmem_limit_bytes=VMEM_LIMIT)


def _full(shape):
    return pl.BlockSpec(shape, lambda *_: (0,) * len(shape))


def _rms(x, g):
    return x * lax.rsqrt(jnp.mean(x * x, axis=-1, keepdims=True) + RMS_EPS) * g


def _dot(a, b):
    return jnp.dot(a, b, preferred_element_type=F32)


def _dot_nt(a, b):
    return lax.dot_general(a, b, (((1,), (1,)), ((), ())), preferred_element_type=F32)


def _rope_table_kernel(inv_ref, invc_ref, cos_ref, sin_ref, cost_ref, sint_ref, *, pos0, period):
    i = pl.program_id(0)
    tm = cos_ref.shape[0]
    half = HEAD_DIM // 2
    row = i * tm + lax.broadcasted_iota(jnp.int32, (tm, LANES), 0)
    ang = (pos0 + row % period).astype(F32) * inv_ref[...]
    lane = lax.broadcasted_iota(jnp.int32, (tm, LANES), 1)
    cos_ref[...] = jnp.cos(ang)
    s = jnp.sin(ang)
    sin_ref[...] = jnp.where(lane % HEAD_DIM < half, -s, s)
    col = i * tm + lax.broadcasted_iota(jnp.int32, (HEAD_DIM, tm), 1)
    ang_t = (pos0 + col % period).astype(F32) * jnp.concatenate([invc_ref[...]] * (tm // LANES), axis=1)
    feat = lax.broadcasted_iota(jnp.int32, (HEAD_DIM, tm), 0)
    cost_ref[...] = jnp.cos(ang_t)
    st = jnp.sin(ang_t)
    sint_ref[...] = jnp.where(feat < half, -st, st)


def _rope_tables(rows, pos0, period):
    half = HEAD_DIM // 2
    inv = ROPE_THETA ** (-jnp.arange(half, dtype=F32) / half)
    inv_row = jnp.tile(inv, LANES // half)[None, :]
    inv_col = jnp.broadcast_to(jnp.tile(inv, HEAD_DIM // half)[:, None], (HEAD_DIM, LANES))
    kern = functools.partial(_rope_table_kernel, pos0=pos0, period=period)
    tm = min(ROW_TILE, rows)
    tok = pl.BlockSpec((tm, LANES), lambda i: (i, 0))
    feat = pl.BlockSpec((HEAD_DIM, tm), lambda i: (0, i))
    return pl.pallas_call(
        kern,
        grid=(rows // tm,),
        in_specs=[_full((1, LANES)), _full((HEAD_DIM, LANES))],
        out_specs=[tok, tok, feat, feat],
        out_shape=[jax.ShapeDtypeStruct((rows, LANES), F32)] * 2 + [jax.ShapeDtypeStruct((HEAD_DIM, rows), F32)] * 2,
        compiler_params=_cparams("parallel"),
        name="rope_tables",
    )(inv_row, inv_col)


def _rope_tok(y, cos, sin):
    lane = lax.broadcasted_iota(jnp.int32, y.shape, 1)
    first = lane % HEAD_DIM < HEAD_DIM // 2
    sw = jnp.where(first, pltpu.roll(y, LANES - HEAD_DIM // 2, 1), pltpu.roll(y, HEAD_DIM // 2, 1))
    return y * cos + sw * sin


def _rope_feat(y, cost, sint):
    return y * cost + pltpu.roll(y, HEAD_DIM // 2, 0) * sint


def _nsa_inproj_prompt_kernel(x_ref, g_ref, cos_ref, sin_ref, cost_ref, sint_ref, w_ref,
                              qt_ref, gatest_ref, cmp_ref, rowst_ref, wrowst_ref,
                              ksa_ref, vsa_ref, kwa_ref, vwa_ref):
    tm = x_ref.shape[0]
    tk = ksa_ref.shape[2]
    i = pl.program_id(1)
    xn = _rms(x_ref[...], g_ref[...]).astype(BF16)
    cos, sin = cos_ref[...], sin_ref[...]
    cost, sint = cost_ref[...], sint_ref[...]
    scale = HEAD_DIM ** -0.5 * LOG2_E

    qt = _dot_nt(w_ref[0:Q_DIM, :], xn)
    for h in range(N_HEADS):
        rs = slice(h * HEAD_DIM, (h + 1) * HEAD_DIM)
        qt_ref[rs, :] = (_rope_feat(qt[rs], cost, sint) * scale).astype(BF16)
    g0 = Q_DIM + 6 * KV_DIM
    gatest_ref[...] = jax.nn.sigmoid(_dot_nt(w_ref[g0:g0 + LANES, :], xn))

    ycmp = _dot_nt(xn, w_ref[Q_DIM:Q_DIM + 2 * KV_DIM, :])
    for c in range(2 * KV_DIM // LANES):
        cs = slice(c * LANES, (c + 1) * LANES)
        cmp_ref[:, cs] = _rope_tok(ycmp[:, cs], cos, sin) if c * LANES < KV_DIM else ycmp[:, cs]

    lane = lax.broadcasted_iota(jnp.int32, (tm, LANES), 1)
    lo = lane < HEAD_DIM
    pos = i * tm + lax.broadcasted_iota(jnp.int32, (tm, LANES), 0)
    sel_bias = jnp.where(lane - HEAD_DIM == pos // SEL_BLOCK, NEG, 0.0)
    for part, ref, fill in ((2, ksa_ref, sel_bias), (4, kwa_ref, 0.0)):
        yk = _dot_nt(xn, w_ref[Q_DIM + part * KV_DIM:Q_DIM + (part + 1) * KV_DIM, :])
        for c in range(KV_DIM // LANES):
            y = _rope_tok(yk[:, c * LANES:(c + 1) * LANES], cos, sin)
            pair = (jnp.where(lo, y, fill).astype(BF16), jnp.where(lo, pltpu.roll(y, HEAD_DIM, 1), fill).astype(BF16))
            for gg in range(2):
                for j in range(tm // tk):
                    ref[2 * c + gg, j] = pair[gg][j * tk:(j + 1) * tk, :]

    yt = _dot_nt(w_ref[Q_DIM:Q_DIM + 6 * KV_DIM, :], xn)
    ones = jnp.ones((HEAD_DIM, tm), F32)
    for part in range(6):
        for g in range(KV_HEADS):
            r0 = part * KV_DIM + g * HEAD_DIM
            y = yt[r0:r0 + HEAD_DIM, :]
            if part % 2 == 0:
                y = _rope_feat(y, cost, sint)
            if part < 4:
                rowst_ref[r0:r0 + HEAD_DIM, :] = y
            else:
                wrowst_ref[r0 - 4 * KV_DIM:r0 - 4 * KV_DIM + HEAD_DIM, :] = y
            if part in (3, 5):
                ref = vsa_ref if part == 3 else vwa_ref
                ya = jnp.concatenate([y, ones], axis=0).astype(BF16)
                for j in range(tm // tk):
                    ref[g, j] = ya[:, j * tk:(j + 1) * tk]


def _nsa_inproj_prompt(x, g, tabs, w, batch, t_seq):
    m, d = x.shape
    tm = min(ROW_TILE, t_seq)
    tk = min(ATT_TILE, t_seq)
    nt = t_seq // tm
    cos, sin, cost, sint = tabs
    row = lambda cols: pl.BlockSpec((tm, cols), lambda b, i: (b * nt + i, 0))
    tok_tab = pl.BlockSpec((tm, LANES), lambda b, i: (i, 0))
    feat_tab = pl.BlockSpec((HEAD_DIM, tm), lambda b, i: (0, i))
    featmaj = lambda rows: pl.BlockSpec((None, rows, tm), lambda b, i: (b, 0, i))
    ktiles = pl.BlockSpec((None, KV_HEADS, tm // tk, tk, LANES), lambda b, i: (b, 0, i, 0, 0))
    vtiles = pl.BlockSpec((None, KV_HEADS, tm // tk, LANES, tk), lambda b, i: (b, 0, i, 0, 0))
    kshape = jax.ShapeDtypeStruct((batch, KV_HEADS, t_seq // tk, tk, LANES), BF16)
    vshape = jax.ShapeDtypeStruct((batch, KV_HEADS, t_seq // tk, LANES, tk), BF16)
    return pl.pallas_call(
        _nsa_inproj_prompt_kernel,
        grid=(batch, nt),
        in_specs=[row(d), _full((1, d)), tok_tab, tok_tab, feat_tab, feat_tab, _full(w.shape)],
        out_specs=[featmaj(Q_DIM), featmaj(LANES), row(2 * KV_DIM), featmaj(4 * KV_DIM), featmaj(2 * KV_DIM),
                   ktiles, vtiles, ktiles, vtiles],
        out_shape=[jax.ShapeDtypeStruct((batch, Q_DIM, t_seq), BF16), jax.ShapeDtypeStruct((batch, LANES, t_seq), F32),
                   jax.ShapeDtypeStruct((m, 2 * KV_DIM), F32),
                   jax.ShapeDtypeStruct((batch, 4 * KV_DIM, t_seq), F32),
                   jax.ShapeDtypeStruct((batch, 2 * KV_DIM, t_seq), F32), kshape, vshape, kshape, vshape],
        compiler_params=_cparams("parallel", "parallel"),
        name="nsa_inproj_prompt",
    )(x, g, cos, sin, cost, sint, w)


def _nsa_inproj_sample_kernel(x_ref, g_ref, cos_ref, sin_ref, w_ref, q_ref, gates_ref, rows_ref, wrows_ref):
    xn = _rms(x_ref[...], g_ref[...]).astype(BF16)
    cos, sin = cos_ref[...], sin_ref[...]
    scale = HEAD_DIM ** -0.5
    yq = _dot_nt(xn, w_ref[0:Q_DIM, :])
    for c in range(Q_DIM // LANES):
        cs = slice(c * LANES, (c + 1) * LANES)
        q_ref[:, cs] = (_rope_tok(yq[:, cs], cos, sin) * scale).astype(BF16)
    g0 = Q_DIM + 6 * KV_DIM
    gates_ref[...] = jax.nn.sigmoid(_dot_nt(xn, w_ref[g0:g0 + LANES, :]))
    ykv = _dot_nt(xn, w_ref[Q_DIM:g0, :])
    for c in range(6 * KV_DIM // LANES):
        cs = slice(c * LANES, (c + 1) * LANES)
        part = c * LANES // KV_DIM
        y = _rope_tok(ykv[:, cs], cos, sin) if part % 2 == 0 else ykv[:, cs]
        if part < 4:
            rows_ref[:, cs] = y
        else:
            wrows_ref[:, c * LANES - 4 * KV_DIM:(c + 1) * LANES - 4 * KV_DIM] = y


def _nsa_inproj_sample(x, g, tabs, w):
    m, d = x.shape
    tm = min(ROW_TILE, m)
    cos, sin = tabs[0], tabs[1]
    n_tab = cos.shape[0] // tm
    row = lambda cols: pl.BlockSpec((tm, cols), lambda i: (i, 0))
    tab = pl.BlockSpec((tm, LANES), lambda i: (i % n_tab, 0))
    return pl.pallas_call(
        _nsa_inproj_sample_kernel,
        grid=(m // tm,),
        in_specs=[row(d), _full((1, d)), tab, tab, _full(w.shape)],
        out_specs=[row(d), row(LANES), row(4 * KV_DIM), row(2 * KV_DIM)],
        out_shape=[jax.ShapeDtypeStruct((m, d), BF16), jax.ShapeDtypeStruct((m, LANES), F32),
                   jax.ShapeDtypeStruct((m, 4 * KV_DIM), F32), jax.ShapeDtypeStruct((m, 2 * KV_DIM), F32)],
        compiler_params=_cparams("parallel"),
        name="nsa_inproj_sample",
    )(x, g, cos, sin, w)


def _compress_rows(load, pe_ref, w1_ref, w2t_ref, kv, n, transposed=False):
    lane = lax.broadcasted_iota(jnp.int32, (n, LANES), 1)
    lo = lane < HEAD_DIM
    acc = jnp.zeros((KV_HEADS * n, w1_ref.shape[-1]), F32)
    lq = 4
    for l0 in range(0, CMP_BLOCK, lq):
        halves = [[], []]
        for l in range(l0, l0 + lq, 2):
            for hf in range(2):
                a = load(l, hf) + pe_ref[kv, l:l + 1, :]
                b = load(l + 1, hf) + pe_ref[kv, l + 1:l + 2, :]
                even = jnp.where(lo, a, pltpu.roll(b, HEAD_DIM, 1))
                odd = jnp.where(lo, pltpu.roll(a, HEAD_DIM, 1), b)
                halves[hf].append((even, odd))
        groups = []
        for g in range(KV_HEADS):
            hf, par = divmod(g, 2)
            groups.append(jnp.concatenate([pc[par] for pc in halves[hf]], axis=1))
        xg = jnp.concatenate(groups, axis=0).astype(BF16)
        acc = acc + _dot(xg, w1_ref[kv, l0 * HEAD_DIM:(l0 + lq) * HEAD_DIM, :])
    h = jax.nn.gelu(acc).astype(BF16)
    return _dot_nt(w2t_ref[kv], h) if transposed else _dot_nt(h, w2t_ref[kv])


def _compress_prompt_kernel(s0_ref, s1_ref, s2_ref, s3_ref, pe_ref, w1_ref, w2t_ref, kc_ref, vct_ref):
    n = kc_ref.shape[0]
    hn = n // 2
    srcs = ((s0_ref, s1_ref), (s2_ref, s3_ref))

    def loader(kv):
        def load(l, hf):
            ev = srcs[kv][hf][pl.ds(l, hn, stride=2 * CMP_BLOCK), :]
            od = srcs[kv][hf][pl.ds(CMP_BLOCK + l, hn, stride=2 * CMP_BLOCK), :]
            return jnp.concatenate([ev, od], axis=0)
        return load

    y = _compress_rows(loader(0), pe_ref, w1_ref, w2t_ref, 0, n)
    yt = _compress_rows(loader(1), pe_ref, w1_ref, w2t_ref, 1, n, transposed=True)
    for g in range(KV_HEADS):
        kc_ref[:, g * HEAD_DIM:(g + 1) * HEAD_DIM] = y[g * n:(g + 1) * n].astype(BF16)
        vct_ref[g * HEAD_DIM:(g + 1) * HEAD_DIM, :] = yt[:, g * n:(g + 1) * n].astype(BF16)


def _compress_prompt(cmp_rows, pe2, w1, w2t, batch, t_seq):
    n = t_seq // CMP_BLOCK
    chunk = lambda c: pl.BlockSpec((t_seq, LANES), lambda b: (b, c))
    return pl.pallas_call(
        _compress_prompt_kernel,
        grid=(batch,),
        in_specs=[chunk(0), chunk(1), chunk(2), chunk(3), _full(pe2.shape), _full(w1.shape), _full(w2t.shape)],
        out_specs=[pl.BlockSpec((None, n, KV_DIM), lambda b: (b, 0, 0)),
                   pl.BlockSpec((None, KV_DIM, n), lambda b: (b, 0, 0))],
        out_shape=[jax.ShapeDtypeStruct((batch, n, KV_DIM), BF16), jax.ShapeDtypeStruct((batch, KV_DIM, n), BF16)],
        compiler_params=_cparams("parallel"),
        name="nsa_compress_prompt",
    )(cmp_rows, cmp_rows, cmp_rows, cmp_rows, pe2, w1, w2t)


def _topk_not_selected(imp_t, n_live, rank_ref):
    n_sel, nq = imp_t.shape
    slabs = [imp_t[r0:r0 + SUBLANES] for r0 in range(0, n_sel, SUBLANES)]
    sub = lax.broadcasted_iota(jnp.int32, (SUBLANES, nq), 0)
    rank_ref[...] = jnp.zeros((n_sel, nq), F32)
    for c0 in range(0, n_sel, SUBLANES):
        @pl.when(c0 < n_live)
        def _():
            for v, slab in enumerate(slabs):
                r0 = v * SUBLANES
                cnt = jnp.zeros((SUBLANES, nq), F32)
                for bp in range(c0, c0 + SUBLANES):
                    other = imp_t[bp:bp + 1, :]
                    if bp < r0:
                        ahead = other >= slab
                    elif bp >= r0 + SUBLANES - 1:
                        ahead = other > slab
                    else:
                        ahead = (other > slab) | ((other == slab) & (bp - r0 < sub))
                    cnt = cnt + jnp.where(ahead, 1.0, 0.0)
                rank_ref[r0:r0 + SUBLANES, :] += cnt
    return jnp.where(rank_ref[...] < TOP_N, 0.0, 1.0)


def _nsa_attn_prompt_kernel(qt_ref, gatest_ref, kc_ref, vct_ref, ksa_ref, vsa_ref, kwa_ref, vwa_ref, o_ref,
                            acc_ref, m_ref, accw_ref, mw_ref, sa_ref, sb_ref, wb_ref, cb_ref, cv_ref, rank_ref):
    tq = qt_ref.shape[1]
    tk = ksa_ref.shape[2]
    n_cmp = kc_ref.shape[0]
    n_sel = n_cmp // 2
    rq = Q_PER_KV * tq
    i = pl.program_id(1)
    t0 = i * tq
    t_col = t0 + lax.broadcasted_iota(jnp.int32, (1, rq), 1) % tq
    key_row = lax.broadcasted_iota(jnp.int32, (tk, rq), 0)
    n_wt = -(-(WINDOW - 1) // tk) + 1
    kt_diag = t0 // tk
    wkt = [kt_diag - j for j in range(n_wt)]

    for j, kt in enumerate(wkt):
        kpos = kt * tk + key_row
        wb_ref[j] = jnp.where((kpos >= 0) & (kpos <= t_col) & (kpos > t_col - WINDOW), 0.0, NEG)
    nrow = lax.broadcasted_iota(jnp.int32, (n_cmp, rq), 0)
    cblk = jnp.where(nrow < n_sel, 2 * nrow, 2 * (nrow - n_sel) + 1)
    cvis = (cblk + 1) * CMP_BLOCK - 1 <= t_col
    cb_ref[...] = jnp.where(cvis, 0.0, NEG)
    cv_ref[...] = jnp.where(cvis, 1.0, 0.0)

    def online_update(s_ref, vt, bias, m_ref, acc_ref):
        s = s_ref[...]
        if bias is not None:
            s = s + bias
        m_old = m_ref[...]
        m_new = jnp.maximum(m_old, jnp.max(s, axis=0, keepdims=True))
        p = jnp.exp2(s - m_new)
        acc_ref[...] = acc_ref[...] * jnp.exp2(m_old - m_new) + _dot(vt, p.astype(BF16))
        m_ref[...] = m_new

    def normalized(acc):
        return acc[:HEAD_DIM] / jnp.maximum(acc[HEAD_DIM:HEAD_DIM + 1], 1e-30)

    for g in range(KV_HEADS):
        qg = jnp.concatenate(
            [qt_ref[(Q_PER_KV * g + r) * HEAD_DIM:(Q_PER_KV * g + r + 1) * HEAD_DIM, :] for r in range(Q_PER_KV)],
            axis=1)

        sc = _dot(kc_ref[:, g * HEAD_DIM:(g + 1) * HEAD_DIM], qg) + cb_ref[...]
        e = jnp.exp2(sc - jnp.max(sc, axis=0, keepdims=True)) * cv_ref[...]
        pc = e / jnp.maximum(jnp.sum(e, axis=0, keepdims=True), 1e-30)
        o_c = _dot(vct_ref[g * HEAD_DIM:(g + 1) * HEAD_DIM, :], pc.astype(BF16))

        pq = pc[:, 0:tq]
        for r in range(1, Q_PER_KV):
            pq = pq + pc[:, r * tq:(r + 1) * tq]
        imp = pq[:n_sel] + pq[n_sel:]
        blk = lax.broadcasted_iota(jnp.int32, (n_sel, tq), 0)
        cur = (t0 + lax.broadcasted_iota(jnp.int32, (n_sel, tq), 1)) // SEL_BLOCK
        imp = jnp.where(blk > cur, -1.0, imp)
        imp = jnp.where(blk == 0, FORCED_FIRST, imp)
        imp = jnp.where(blk == cur, FORCED_CUR, imp)
        n_live = (t0 + tq - 1) // SEL_BLOCK + 1
        notsel = _topk_not_selected(imp, n_live, rank_ref)
        if n_sel < HEAD_DIM:
            notsel = jnp.concatenate([notsel, jnp.zeros((HEAD_DIM - n_sel, tq), F32)], axis=0)
        qa = jnp.concatenate([qg, jnp.concatenate([notsel.astype(BF16)] * Q_PER_KV, axis=1)], axis=0)

        for ref in (m_ref, mw_ref):
            ref[...] = jnp.full((1, rq), NEG, F32)
        for ref in (acc_ref, accw_ref):
            ref[...] = jnp.zeros((LANES, rq), F32)
        slots = [sa_ref, sb_ref]
        qw = jnp.concatenate([qg, jnp.zeros_like(qg)], axis=0)

        slots[0][...] = _dot(kwa_ref[g, wkt[0]], qw)
        for j, kt in enumerate(wkt):
            if j + 1 < n_wt:
                slots[1 - j % 2][...] = _dot(kwa_ref[g, jnp.maximum(wkt[j + 1], 0)], qw)
            else:
                slots[1 - j % 2][...] = _dot(ksa_ref[g, 0], qa)
            online_update(slots[j % 2], vwa_ref[g, jnp.maximum(kt, 0)], wb_ref[j], mw_ref, accw_ref)
        o_w = normalized(accw_ref[...])

        s_even, s_odd = slots[n_wt % 2], slots[1 - n_wt % 2]

        def sel_pair(j, carry):
            s_odd[...] = _dot(ksa_ref[g, 2 * j + 1], qa)
            online_update(s_even, vsa_ref[g, 2 * j], None, m_ref, acc_ref)
            s_even[...] = _dot(ksa_ref[g, 2 * j + 2], qa)
            online_update(s_odd, vsa_ref[g, 2 * j + 1], None, m_ref, acc_ref)
            return carry

        lax.fori_loop(0, kt_diag // 2, sel_pair, 0)

        @pl.when(kt_diag % 2 == 1)
        def _():
            s_odd[...] = _dot(ksa_ref[g, kt_diag], qa)
            online_update(s_even, vsa_ref[g, kt_diag - 1], None, m_ref, acc_ref)
            online_update(s_odd, vsa_ref[g, kt_diag], wb_ref[0], m_ref, acc_ref)

        @pl.when(kt_diag % 2 == 0)
        def _():
            online_update(s_even, vsa_ref[g, kt_diag], wb_ref[0], m_ref, acc_ref)

        o_s = normalized(acc_ref[...])

        outs = []
        for r in range(Q_PER_KV):
            h = Q_PER_KV * g + r
            cs = slice(r * tq, (r + 1) * tq)
            gc = gatest_ref[N_BRANCH * h:N_BRANCH * h + 1, :]
            gs = gatest_ref[N_BRANCH * h + 1:N_BRANCH * h + 2, :]
            gw = gatest_ref[N_BRANCH * h + 2:N_BRANCH * h + 3, :]
            outs.append(gc * o_c[:, cs] + gs * o_s[:, cs] + gw * o_w[:, cs])
        for pr in range(Q_PER_KV // 2):
            c0 = (Q_PER_KV * g + 2 * pr) * HEAD_DIM
            o_ref[:, c0:c0 + LANES] = jnp.concatenate(outs[2 * pr:2 * pr + 2], axis=0).T.astype(BF16)


def _nsa_attn_prompt(qt, gatest, kc, vct, ksa, vsa, kwa, vwa, batch, t_seq):
    d = qt.shape[1]
    tq = min(ATT_TILE, t_seq)
    nq = t_seq // tq
    n_cmp = t_seq // CMP_BLOCK
    tk = ksa.shape[3]
    featmaj = lambda rows: pl.BlockSpec((None, rows, tq), lambda b, i: (b, 0, i))
    seq = lambda shape: pl.BlockSpec((None,) + shape[1:], lambda b, i: (b, 0, 0, 0, 0))
    return pl.pallas_call(
        _nsa_attn_prompt_kernel,
        grid=(batch, nq),
        in_specs=[featmaj(d), featmaj(LANES),
                  pl.BlockSpec((None, n_cmp, KV_DIM), lambda b, i: (b, 0, 0)),
                  pl.BlockSpec((None, KV_DIM, n_cmp), lambda b, i: (b, 0, 0)),
                  seq(ksa.shape), seq(vsa.shape), seq(kwa.shape), seq(vwa.shape)],
        out_specs=pl.BlockSpec((tq, d), lambda b, i: (b * nq + i, 0)),
        out_shape=jax.ShapeDtypeStruct((batch * t_seq, d), BF16),
        scratch_shapes=[pltpu.VMEM((LANES, Q_PER_KV * tq), F32), pltpu.VMEM((1, Q_PER_KV * tq), F32),
                        pltpu.VMEM((LANES, Q_PER_KV * tq), F32), pltpu.VMEM((1, Q_PER_KV * tq), F32),
                        pltpu.VMEM((tk, Q_PER_KV * tq), F32), pltpu.VMEM((tk, Q_PER_KV * tq), F32),
                        pltpu.VMEM((-(-(WINDOW - 1) // tk) + 1, tk, Q_PER_KV * tq), F32),
                        pltpu.VMEM((n_cmp, Q_PER_KV * tq), F32), pltpu.VMEM((n_cmp, Q_PER_KV * tq), F32),
                        pltpu.VMEM((n_cmp // 2, tq), F32)],
        compiler_params=_cparams("parallel", "arbitrary"),
        name="nsa_attn_prompt",
    )(qt, gatest, kc, vct, ksa, vsa, kwa, vwa)


def _compress_sample_kernel(pt_ref, *refs, n_pages, page, t_new, n_cmp):
    del pt_ref
    pages = refs[:n_pages]
    new_ref, pe_ref, w1_ref, w2t_ref, kc_ref, vc_ref = refs[n_pages:n_pages + 6]
    bufs = refs[n_pages + 6:]
    n_pad = bufs[0].shape[1]
    n_past = n_pages * page // CMP_BLOCK
    grp = SUBLANES * CMP_BLOCK
    ppg = grp // page
    tt = lax.broadcasted_iota(jnp.int32, (grp, grp), 0)
    perm_t = jnp.where(lax.broadcasted_iota(jnp.int32, (grp, grp), 1)
                       == (tt % CMP_BLOCK) * SUBLANES + tt // CMP_BLOCK, 1.0, 0.0).astype(BF16)
    for c, buf_ref in enumerate(bufs):
        cs = slice(c * LANES, (c + 1) * LANES)
        for pp in range(n_pages // ppg):
            xt = jnp.concatenate([pages[pp * ppg + j][cs, :] for j in range(ppg)], axis=1)
            xp = _dot(xt.astype(BF16), perm_t).T
            for l in range(CMP_BLOCK):
                buf_ref[l, pp * SUBLANES:(pp + 1) * SUBLANES, :] = xp[l * SUBLANES:(l + 1) * SUBLANES, :]
        buf_ref[:, n_past:, :] = jnp.zeros((CMP_BLOCK, n_pad - n_past, LANES), F32)
        for l in range(t_new):
            buf_ref[l, n_past:n_past + 1, :] = new_ref[l:l + 1, cs]
    n_out = kc_ref.shape[0]
    for kv, out_ref in ((0, kc_ref), (1, vc_ref)):
        def load(l, hf, kv=kv):
            return bufs[2 * kv + hf][l]
        y = _compress_rows(load, pe_ref, w1_ref, w2t_ref, kv, n_pad)
        valid = lax.broadcasted_iota(jnp.int32, (n_pad, HEAD_DIM), 0) < n_cmp
        for g in range(KV_HEADS):
            out_ref[0:n_pad, g * HEAD_DIM:(g + 1) * HEAD_DIM] = jnp.where(
                valid, y[g * n_pad:(g + 1) * n_pad], 0.0).astype(BF16)
        out_ref[n_pad:, :] = jnp.zeros((n_out - n_pad, KV_DIM), BF16)


def _page_specs(layer, n_pages, page, feat_block):
    half = 2 * KV_DIM
    return [pl.BlockSpec((None, None, half, page),
                         functools.partial(lambda b, pt, p: (layer, pt[b, p], feat_block, 0), p=p))
            for p in range(n_pages)]


def _compress_sample(page_table, cache_t, layer, rows_new, pe2, w1, w2t, t_new):
    dec_b, n_pages = page_table.shape
    page = cache_t.shape[-1]
    past = n_pages * page
    l_pad = -(-(past + t_new) // SEL_BLOCK) * SEL_BLOCK
    n_cmp = l_pad // CMP_BLOCK
    n_pad = -(-n_cmp // SUBLANES) * SUBLANES
    n_out = LANES
    half = 2 * KV_DIM
    kern = functools.partial(_compress_sample_kernel, n_pages=n_pages, page=page, t_new=t_new, n_cmp=n_cmp)
    grid_spec = pltpu.PrefetchScalarGridSpec(
        num_scalar_prefetch=1,
        grid=(dec_b,),
        in_specs=_page_specs(layer, n_pages, page, 0) + [
            pl.BlockSpec((t_new, half), lambda b, pt: (b, 0)),
            pl.BlockSpec(pe2.shape, lambda b, pt: (0, 0, 0)),
            pl.BlockSpec(w1.shape, lambda b, pt: (0, 0, 0)),
            pl.BlockSpec(w2t.shape, lambda b, pt: (0, 0, 0))],
        out_specs=[pl.BlockSpec((None, n_out, KV_DIM), lambda b, pt: (b, 0, 0))] * 2,
        scratch_shapes=[pltpu.VMEM((CMP_BLOCK, n_pad, LANES), F32)] * (half // LANES),
    )
    return pl.pallas_call(
        kern,
        grid_spec=grid_spec,
        out_shape=[jax.ShapeDtypeStruct((dec_b, n_out, KV_DIM), BF16)] * 2,
        compiler_params=_cparams("arbitrary"),
        name="nsa_compress_sample",
    )(page_table, *([cache_t] * n_pages), rows_new, pe2, w1, w2t), n_cmp


def _nsa_attn_sample_kernel(pt_ref, *refs, n_pages, page, t_new, n_cmp):
    del pt_ref
    pages = refs[:n_pages]
    (kc_ref, vc_ref, q_ref, new_ref, wnew_ref, win_ref, gates_ref,
     o_ref, nwin_ref, kst_ref, vst_ref) = refs[n_pages:]
    past = n_pages * page
    n_buf = win_ref.shape[1]
    n_sel = n_cmp // 2
    gq = Q_PER_KV * t_new
    rq = KV_HEADS * gq
    nk = past + LANES
    nw = n_buf + LANES

    for p in range(n_pages):
        blk = pages[p][...]
        kst_ref[:, p * page:(p + 1) * page] = blk[:KV_DIM].astype(BF16)
        vst_ref[:, p * page:(p + 1) * page] = blk[KV_DIM:].astype(BF16)
    pad = jnp.zeros((LANES - t_new, KV_DIM), F32)
    new = new_ref[...]
    wnew = wnew_ref[...]
    knew = jnp.concatenate([new[:, :KV_DIM], pad], axis=0).astype(BF16)
    vnew = jnp.concatenate([new[:, KV_DIM:], pad], axis=0).astype(BF16)
    kwnew = jnp.concatenate([wnew[:, :KV_DIM], pad], axis=0).astype(BF16)
    vwnew = jnp.concatenate([wnew[:, KV_DIM:], pad], axis=0).astype(BF16)
    win = win_ref[...]
    kwt = win[:KV_DIM].astype(BF16)
    vwt = win[KV_DIM:].astype(BF16)

    rolled = pltpu.roll(win, n_buf - t_new, 1)
    wnew_t = jnp.concatenate([jnp.zeros((LANES - t_new, 2 * KV_DIM), F32), wnew], axis=0).T
    tail_lane = lax.broadcasted_iota(jnp.int32, (2 * KV_DIM, LANES), 1)
    nwin_ref[:, 0:n_buf - LANES] = rolled[:, 0:n_buf - LANES]
    nwin_ref[:, n_buf - LANES:] = jnp.where(tail_lane >= LANES - t_new, wnew_t, rolled[:, n_buf - LANES:])

    q = q_ref[...]
    lane_g = lax.broadcasted_iota(jnp.int32, (t_new, KV_DIM), 1) // HEAD_DIM
    pieces = []
    for g in range(KV_HEADS):
        for j in range(Q_PER_KV):
            qj = jnp.concatenate([q[:, (Q_PER_KV * gg + j) * HEAD_DIM:(Q_PER_KV * gg + j + 1) * HEAD_DIM]
                                  for gg in range(KV_HEADS)], axis=1)
            pieces.append(jnp.where(lane_g == g, qj, jnp.zeros_like(qj)))
    qbd = jnp.concatenate(pieces, axis=0)
    t_row = past + lax.broadcasted_iota(jnp.int32, (rq, 1), 0) % t_new

    def softmax(s, mask):
        s = jnp.where(mask, s, NEG)
        e = jnp.where(mask, jnp.exp(s - jnp.max(s, axis=1, keepdims=True)), 0.0)
        return e / jnp.maximum(jnp.sum(e, axis=1, keepdims=True), 1e-30)

    ncl = kc_ref.shape[0]
    sc = _dot_nt(qbd, kc_ref[...])
    cidx = lax.broadcasted_iota(jnp.int32, (rq, ncl), 1)
    pc = softmax(sc, (cidx < n_cmp) & ((cidx + 1) * CMP_BLOCK - 1 <= t_row))
    oc = _dot(pc.astype(BF16), vc_ref[...])

    groups = []
    for g in range(KV_HEADS):
        pg = pc[g * gq:g * gq + t_new]
        for j in range(1, Q_PER_KV):
            pg = pg + pc[g * gq + j * t_new:g * gq + (j + 1) * t_new]
        groups.append(pg)
    pg = jnp.concatenate(groups, axis=0)
    nr = KV_HEADS * t_new
    imp = pg + pltpu.roll(pg, ncl - 1, 1)
    lane = lax.broadcasted_iota(jnp.int32, (nr, ncl), 1)
    blk = lane // 2
    cur = (past + lax.broadcasted_iota(jnp.int32, (nr, ncl), 0) % t_new) // SEL_BLOCK
    imp = jnp.where(blk > cur, -1.0, imp)
    imp = jnp.where(blk == 0, FORCED_FIRST, imp)
    imp = jnp.where(blk == cur, FORCED_CUR, imp)
    rank = jnp.zeros((nr, ncl), jnp.int32)
    for bp in range(n_sel):
        other = imp[:, 2 * bp:2 * bp + 1]
        rank = rank + ((other > imp) | ((other == imp) & (bp < blk))).astype(jnp.int32)
    sel = jnp.where((rank < TOP_N) & (lane % 2 == 0) & (blk < n_sel), 1.0, 0.0)
    sel_rows = jnp.concatenate([sel[g * t_new:(g + 1) * t_new] for g in range(KV_HEADS) for _ in range(Q_PER_KV)],
                               axis=0).astype(BF16)
    expand = jnp.where(lax.broadcasted_iota(jnp.int32, (ncl, nk), 0)
                       == 2 * (lax.broadcasted_iota(jnp.int32, (ncl, nk), 1) // SEL_BLOCK), 1.0, 0.0).astype(BF16)
    selmask = _dot(sel_rows, expand) > 0.5

    kpos = lax.broadcasted_iota(jnp.int32, (rq, nk), 1)
    ss = jnp.concatenate([_dot(qbd, kst_ref[...]), _dot_nt(qbd, knew)], axis=1)
    ps = softmax(ss, selmask & (kpos <= t_row)).astype(BF16)
    os_ = _dot_nt(ps[:, :past], vst_ref[...]) + _dot(ps[:, past:], vnew)

    wpos = past - n_buf + lax.broadcasted_iota(jnp.int32, (rq, nw), 1)
    sw = jnp.concatenate([_dot(qbd, kwt), _dot_nt(qbd, kwnew)], axis=1)
    pw = softmax(sw, (wpos <= t_row) & (wpos > t_row - WINDOW)).astype(BF16)
    ow = _dot_nt(pw[:, :n_buf], vwt) + _dot(pw[:, n_buf:], vwnew)

    gates = gates_ref[...]
    for g in range(KV_HEADS):
        for j in range(Q_PER_KV):
            h = Q_PER_KV * g + j
            rs = slice(g * gq + j * t_new, g * gq + (j + 1) * t_new)
            cs = slice(g * HEAD_DIM, (g + 1) * HEAD_DIM)
            gc = gates[:, N_BRANCH * h:N_BRANCH * h + 1]
            gs = gates[:, N_BRANCH * h + 1:N_BRANCH * h + 2]
            gw = gates[:, N_BRANCH * h + 2:N_BRANCH * h + 3]
            o_ref[:, h * HEAD_DIM:(h + 1) * HEAD_DIM] = (
                gc * oc[rs, cs] + gs * os_[rs, cs] + gw * ow[rs, cs]).astype(BF16)


def _nsa_attn_sample(page_table, cache_t, win_t, layer, kc, vc, q, rows_new, wrows_new, gates, t_new, n_cmp):
    dec_b, n_pages = page_table.shape
    page = cache_t.shape[-1]
    past = n_pages * page
    n_buf = win_t.shape[-1]
    half = 2 * KV_DIM
    d = q.shape[1]
    ncl = kc.shape[1]
    kern = functools.partial(_nsa_attn_sample_kernel, n_pages=n_pages, page=page, t_new=t_new, n_cmp=n_cmp)
    grid_spec = pltpu.PrefetchScalarGridSpec(
        num_scalar_prefetch=1,
        grid=(dec_b,),
        in_specs=_page_specs(layer, n_pages, page, 1) + [
            pl.BlockSpec((None, ncl, KV_DIM), lambda b, pt: (b, 0, 0)),
            pl.BlockSpec((None, ncl, KV_DIM), lambda b, pt: (b, 0, 0)),
            pl.BlockSpec((t_new, d), lambda b, pt: (b, 0)),
            pl.BlockSpec((t_new, half), lambda b, pt: (b, 1)),
            pl.BlockSpec((t_new, half), lambda b, pt: (b, 0)),
            pl.BlockSpec((None, None, half, n_buf), lambda b, pt: (layer, b, 0, 0)),
            pl.BlockSpec((t_new, LANES), lambda b, pt: (b, 0))],
        out_specs=[pl.BlockSpec((t_new, d), lambda b, pt: (b, 0)),
                   pl.BlockSpec((None, half, n_buf), lambda b, pt: (b, 0, 0))],
        scratch_shapes=[pltpu.VMEM((KV_DIM, past), BF16), pltpu.VMEM((KV_DIM, past), BF16)],
    )
    return pl.pallas_call(
        kern,
        grid_spec=grid_spec,
        out_shape=[jax.ShapeDtypeStruct((dec_b * t_new, d), BF16), jax.ShapeDtypeStruct((dec_b, half, n_buf), F32)],
        compiler_params=_cparams("arbitrary"),
        name="nsa_attn_sample",
    )(page_table, *([cache_t] * n_pages), kc, vc, q, rows_new, wrows_new, win_t, gates)


def _outproj_kernel(x_ref, o_ref, w_ref, y_ref):
    y_ref[...] = x_ref[...] + _dot(o_ref[...], w_ref[...])


def _outproj(x, o, w):
    m, d = x.shape
    tm = min(ROW_TILE, m)
    row = pl.BlockSpec((tm, d), lambda i: (i, 0))
    return pl.pallas_call(
        _outproj_kernel,
        grid=(m // tm,),
        in_specs=[row, row, _full(w.shape)],
        out_specs=row,
        out_shape=jax.ShapeDtypeStruct((m, d), F32),
        compiler_params=_cparams("parallel"),
        name="mixer_outproj",
    )(x, o, w)


def _mlp_kernel(x_ref, g_ref, w1_ref, w2_ref, y_ref, xn_ref, acc_ref):
    j = pl.program_id(1)

    @pl.when(j == 0)
    def _():
        x = x_ref[...]
        xn_ref[...] = _rms(x, g_ref[...]).astype(BF16)
        acc_ref[...] = x

    h = jnp.maximum(_dot(xn_ref[...], w1_ref[...]), 0.0)
    acc_ref[...] += _dot((h * h).astype(BF16), w2_ref[...])

    @pl.when(j == pl.num_programs(1) - 1)
    def _():
        y_ref[...] = acc_ref[...]


def _mlp(x, g, w1, w2):
    m, d = x.shape
    ff = w1.shape[1]
    tm, tf = min(ROW_TILE, m), 1024
    row = pl.BlockSpec((tm, d), lambda i, j: (i, 0))
    return pl.pallas_call(
        _mlp_kernel,
        grid=(m // tm, ff // tf),
        in_specs=[row, pl.BlockSpec((1, d), lambda i, j: (0, 0)),
                  pl.BlockSpec((d, tf), lambda i, j: (0, j)), pl.BlockSpec((tf, d), lambda i, j: (j, 0))],
        out_specs=row,
        out_shape=jax.ShapeDtypeStruct((m, d), F32),
        scratch_shapes=[pltpu.VMEM((tm, d), BF16), pltpu.VMEM((tm, d), F32)],
        compiler_params=_cparams("parallel", "arbitrary"),
        name="sqrelu_mlp",
    )(x, g, w1, w2)


def _ple_kernel(x_ref, p_ref, g_ref, wg_ref, wp_ref, gf_ref, y_ref, *, final_norm):
    x = x_ref[...]
    gate = jax.nn.sigmoid(_dot(_rms(x, g_ref[...]).astype(BF16), wg_ref[...]))
    y = x + _dot(p_ref[...].astype(BF16), wp_ref[...]) * gate
    if final_norm:
        y = _rms(y, gf_ref[...])
    y_ref[...] = y


def _ple(x, p_all, layer, g, wg, wp, gf, final_norm):
    m, d = x.shape
    tm = min(ROW_TILE, m)
    row = pl.BlockSpec((tm, d), lambda i: (i, 0))
    return pl.pallas_call(
        functools.partial(_ple_kernel, final_norm=final_norm),
        grid=(m // tm,),
        in_specs=[row, pl.BlockSpec((None, tm, p_all.shape[-1]), lambda i: (layer, i, 0)), _full((1, d)),
                  _full(wg.shape), _full(wp.shape), _full((1, d))],
        out_specs=row,
        out_shape=jax.ShapeDtypeStruct((m, d), F32),
        compiler_params=_cparams("parallel"),
        name="ple_gate",
    )(x, p_all, g, wg, wp, gf)


def _hg_inproj_kernel(x_ref, g_ref, lbl_ref, w_ref, q_ref, f_ref, k_ref, v_ref, gs_ref, xn_ref, *, layer):
    j = pl.program_id(1)

    @pl.when(j == 0)
    def _():
        xn_ref[...] = _rms(x_ref[...], g_ref[...]).astype(BF16)

    y = _dot(xn_ref[...], w_ref[...])

    def put(ref, val):
        for h in range(HG_HEADS):
            ref[h] = val[:, h * HG_DK:(h + 1) * HG_DK]

    @pl.when(j == 0)
    def _():
        put(q_ref, jax.nn.silu(y) * HG_DK ** -0.5)

    @pl.when(j == 1)
    def _():
        lg = lbl_ref[...]
        e = jnp.exp(lg - jnp.max(lg, axis=0, keepdims=True))
        w = e / jnp.sum(e, axis=0, keepdims=True)
        cs = w[0:1]
        for r in range(1, layer + 1):
            cs = cs + w[r:r + 1]
        lb = cs - w[0:1]
        sg = jax.nn.sigmoid(y)
        put(f_ref, jnp.log(lb + (1.0 - lb) * sg))
        put(k_ref, (1.0 - lb) * (1.0 - sg))

    @pl.when(j == 2)
    def _():
        put(v_ref, y)

    @pl.when(j == 3)
    def _():
        put(gs_ref, jax.nn.silu(y))


def _hg_inproj(x, g, lb_logits, w, layer):
    m, d = x.shape
    tm = min(ROW_TILE, m)
    head_major = pl.BlockSpec((HG_HEADS, tm, HG_DK), lambda i, j: (0, i, 0))
    return pl.pallas_call(
        functools.partial(_hg_inproj_kernel, layer=layer),
        grid=(m // tm, 4),
        in_specs=[pl.BlockSpec((tm, d), lambda i, j: (i, 0)), pl.BlockSpec((1, d), lambda i, j: (0, 0)),
                  pl.BlockSpec(lb_logits.shape, lambda i, j: (0, 0)), pl.BlockSpec((d, d), lambda i, j: (0, j))],
        out_specs=[head_major] * 5,
        out_shape=[jax.ShapeDtypeStruct((HG_HEADS, m, HG_DK), F32)] * 5,
        scratch_shapes=[pltpu.VMEM((tm, d), BF16)],
        compiler_params=_cparams("parallel", "arbitrary"),
        name="hgrn_inproj",
    )(x, g, lb_logits, w)


def _cumsum_rows(f):
    c = f.shape[0]
    row = lax.broadcasted_iota(jnp.int32, f.shape, 0)
    s = 1
    while s < c:
        f = f + jnp.where(row >= s, pltpu.roll(f, s, 0), 0.0)
        s *= 2
    return f


def _hg_intra(q, k, v, b, o, sub):
    c = q.shape[0]
    ones = jnp.ones((HG_DK, HG_DV), BF16)
    trow = lax.broadcasted_iota(jnp.int32, (sub, HG_DK), 0)
    terms = []
    for r0 in range(0, c, sub):
        qi, ki, bi = q[r0:r0 + sub], k[r0:r0 + sub], b[r0:r0 + sub]
        for s in range(sub):
            causal = trow >= s
            dec = jnp.exp(jnp.where(causal, bi - bi[s:s + 1], 0.0))
            terms.append(jnp.where(causal, qi * ki[s:s + 1] * dec, 0.0).astype(BF16))
    a = _dot(jnp.concatenate(terms, axis=0), ones)
    outs = []
    for r0 in range(0, c, sub):
        oi = o[r0:r0 + sub]
        if r0 > 0:
            bs = b[r0 - 1:r0]
            qp = (q[r0:r0 + sub] * jnp.exp(b[r0:r0 + sub] - bs)).astype(BF16)
            kp = (k[:r0] * jnp.exp(bs - b[:r0])).astype(BF16)
            oi = oi + _dot(_dot_nt(qp, kp).astype(BF16), v[:r0].astype(BF16))
        for s in range(sub):
            oi = oi + a[(r0 + s) * sub:(r0 + s + 1) * sub] * v[r0 + s:r0 + s + 1]
        outs.append(oi)
    return jnp.concatenate(outs, axis=0) if len(outs) > 1 else outs[0]


def _hg_out(o, ng, gs):
    return (_rms(o, ng) * gs).astype(BF16)


def _hg_scan_prompt_kernel(q_ref, f_ref, k_ref, v_ref, gs_ref, ng_ref, o_ref, s_ref, st_ref):
    c = pl.program_id(1)
    rows = q_ref.shape[1]

    @pl.when(c == 0)
    def _():
        st_ref[...] = jnp.zeros(st_ref.shape, F32)

    def one_head(h):
        st = st_ref[h]
        for r0 in range(0, rows, HG_CHUNK):
            rs = pl.ds(r0, HG_CHUNK)
            q, k, v = q_ref[h, rs, :], k_ref[h, rs, :], v_ref[h, rs, :]
            b = _cumsum_rows(f_ref[h, rs, :])
            o = _dot_nt((q * jnp.exp(b)).astype(BF16), st.astype(BF16))
            o = _hg_intra(q, k, v, b, o, HG_SUB)
            o_ref[h, rs, :] = _hg_out(o, ng_ref[...], gs_ref[h, rs, :])
            bl = b[HG_CHUNK - 1:HG_CHUNK]
            kd = (k * jnp.exp(bl - b)).astype(BF16)
            st = st * jnp.exp(bl) + lax.dot_general(v.astype(BF16), kd, (((0,), (0,)), ((), ())),
                                                    preferred_element_type=F32)
        st_ref[h] = st

        @pl.when(c == pl.num_programs(1) - 1)
        def _():
            s_ref[h] = st.T

    def heads(it, carry):
        for hh in range(HG_UNROLL_PROMPT):
            one_head(it * HG_UNROLL_PROMPT + hh)
        return carry

    lax.fori_loop(0, HG_HEADS // HG_UNROLL_PROMPT, heads, 0)


def _hg_scan_prompt(q, f, k, v, gs, ng, batch, t_seq):
    m = q.shape[1]
    rows = 2 * HG_CHUNK
    nc = t_seq // rows
    blk = pl.BlockSpec((HG_HEADS, rows, HG_DK), lambda b, c: (0, b * nc + c, 0))
    return pl.pallas_call(
        _hg_scan_prompt_kernel,
        grid=(batch, nc),
        in_specs=[blk] * 5 + [pl.BlockSpec((1, HG_DV), lambda b, c: (0, 0))],
        out_specs=[blk, pl.BlockSpec((None, HG_HEADS, HG_DK, HG_DV), lambda b, c: (b, 0, 0, 0))],
        out_shape=[jax.ShapeDtypeStruct((HG_HEADS, m, HG_DV), BF16),
                   jax.ShapeDtypeStruct((batch, HG_HEADS, HG_DK, HG_DV), F32)],
        scratch_shapes=[pltpu.VMEM((HG_HEADS, HG_DV, HG_DK), F32)],
        compiler_params=_cparams("parallel", "arbitrary"),
        name="hgrn_scan_prompt",
    )(q, f, k, v, gs, ng)


def _hg_step_sample_kernel(q_ref, f_ref, k_ref, v_ref, gs_ref, ng_ref, s0_ref, o_ref, s1_ref, *, t_new):
    n_seq = s0_ref.shape[0]

    def one(sq, h):
        rs = pl.ds(pl.multiple_of(sq * t_new, t_new), t_new)
        q, k, v = q_ref[h, rs, :], k_ref[h, rs, :], v_ref[h, rs, :]
        b = _cumsum_rows(f_ref[h, rs, :])
        s0 = s0_ref[sq, h]
        o = _dot((q * jnp.exp(b)).astype(BF16), s0.astype(BF16))
        o = _hg_intra(q, k, v, b, o, t_new)
        o_ref[h, rs, :] = _hg_out(o, ng_ref[...], gs_ref[h, rs, :])
        bl = b[t_new - 1:t_new]
        kd = k * jnp.exp(bl - b)
        ext = jnp.concatenate([kd, jnp.broadcast_to(jnp.exp(bl), (t_new, HG_DK)),
                               jnp.zeros((HG_DK - 2 * t_new, HG_DK), F32)], axis=0).T
        s1_ref[sq, h] = s0 * ext[:, t_new:t_new + 1] + _dot(ext[:, :t_new].astype(BF16), v.astype(BF16))

    def body(idx, carry):
        per_seq = HG_HEADS // HG_UNROLL_SAMPLE
        for hh in range(HG_UNROLL_SAMPLE):
            one(idx // per_seq, (idx % per_seq) * HG_UNROLL_SAMPLE + hh)
        return carry

    lax.fori_loop(0, n_seq * (HG_HEADS // HG_UNROLL_SAMPLE), body, 0)


def _hg_step_sample(q, f, k, v, gs, ng, s0_all, layer, t_new):
    m = q.shape[1]
    dec_b = s0_all.shape[1]
    sb = min(8, dec_b)
    blk = pl.BlockSpec((HG_HEADS, sb * t_new, HG_DK), lambda i: (0, i, 0))
    st_in = pl.BlockSpec((None, sb, HG_HEADS, HG_DK, HG_DV), lambda i: (layer, i, 0, 0, 0))
    st_out = pl.BlockSpec((sb, HG_HEADS, HG_DK, HG_DV), lambda i: (i, 0, 0, 0))
    return pl.pallas_call(
        functools.partial(_hg_step_sample_kernel, t_new=t_new),
        grid=(dec_b // sb,),
        in_specs=[blk] * 5 + [pl.BlockSpec((1, HG_DV), lambda i: (0, 0)), st_in],
        out_specs=[blk, st_out],
        out_shape=[jax.ShapeDtypeStruct((HG_HEADS, m, HG_DV), BF16), jax.ShapeDtypeStruct(s0_all.shape[1:], F32)],
        compiler_params=_cparams("parallel"),
        name="hgrn_step_sample",
    )(q, f, k, v, gs, ng, s0_all)


def _hg_outproj_kernel(x_ref, o_ref, w_ref, y_ref):
    o = jnp.concatenate([o_ref[h] for h in range(HG_HEADS)], axis=1)
    y_ref[...] = x_ref[...] + _dot(o, w_ref[...])


def _hg_outproj(x, o, w):
    m, d = x.shape
    tm = min(ROW_TILE, m)
    row = pl.BlockSpec((tm, d), lambda i: (i, 0))
    return pl.pallas_call(
        _hg_outproj_kernel,
        grid=(m // tm,),
        in_specs=[row, pl.BlockSpec((HG_HEADS, tm, HG_DV), lambda i: (0, i, 0)), _full(w.shape)],
        out_specs=row,
        out_shape=jax.ShapeDtypeStruct((m, d), F32),
        compiler_params=_cparams("parallel"),
        name="hgrn_outproj",
    )(x, o, w)


def _feature_major_view(a):
    lead = a.shape[:-4]
    t, c, g, d = a.shape[-4:]
    n = len(lead)
    return jnp.transpose(a, tuple(range(n)) + (n + 1, n + 2, n + 3, n)).reshape(lead + (c * g * d, t))


def _token_major_view(a, c):
    lead = a.shape[:-2]
    t = a.shape[-1]
    n = len(lead)
    a = a.reshape(lead + (c, KV_HEADS, HEAD_DIM, t))
    return jnp.transpose(a, tuple(range(n)) + (n + 3, n, n + 1, n + 2))


def kernel(x_prompt, x_sample, cache_nsa_kv, state_nsa_win, state_hgrn, page_table, p_prompt, p_sample, norm_mix, norm_mlp, norm_ple, norm_final, nsa_w_in, nsa_cmp_pe, nsa_cmp_w1, nsa_cmp_w2, nsa_w_out, hg_w_in, hg_lb_logits, hg_norm, hg_w_out, mlp_w1, mlp_w2, ple_w_proj, ple_w_gate):
    batch, t_p, d = x_prompt.shape
    dec_b, t_s, _ = x_sample.shape
    depth = p_prompt.shape[0]
    page = cache_nsa_kv.shape[2]
    past = page_table.shape[1] * page

    xp = x_prompt.reshape(batch * t_p, d)
    xs = x_sample.reshape(dec_b * t_s, d)
    pp = p_prompt.reshape(depth, batch * t_p, -1)
    ps = p_sample.reshape(depth, dec_b * t_s, -1)
    row = lambda a: a.reshape(1, -1)
    cache_t = _feature_major_view(cache_nsa_kv)
    win_t = _feature_major_view(state_nsa_win)
    wl = min(WINDOW, t_p)

    tabs_p = _rope_tables(t_p, 0, t_p)
    tabs_s = _rope_tables(min(ROW_TILE, dec_b * t_s), past, t_s)

    kv_p, kv_s, win_p, win_s, st_p, st_s = [], [], [], [], [], []
    for i in range(depth):
        g_mix = row(norm_mix[i])
        if i % 2 == 0:
            a = i // 2
            n_in = nsa_w_in.shape[2]
            w_t = jnp.pad(jnp.transpose(nsa_w_in[a]).astype(BF16),
                          ((0, Q_DIM + 6 * KV_DIM + LANES - n_in), (0, 0)))
            pe2 = jnp.tile(nsa_cmp_pe[a], (1, 1, LANES // HEAD_DIM))
            cw1 = nsa_cmp_w1[a].reshape(2, CMP_BLOCK * HEAD_DIM, -1).astype(BF16)
            cw2t = jnp.swapaxes(nsa_cmp_w2[a], 1, 2).astype(BF16)
            w_out = nsa_w_out[a].astype(BF16)

            qt, gatest, cmp_rows, rows_t, wrows_t, ksa, vsa, kwa, vwa = _nsa_inproj_prompt(
                xp, g_mix, tabs_p, w_t, batch, t_p)
            kc, vct = _compress_prompt(cmp_rows, pe2, cw1, cw2t, batch, t_p)
            o = _nsa_attn_prompt(qt, gatest, kc, vct, ksa, vsa, kwa, vwa, batch, t_p)
            xp = _outproj(xp, o, w_out)
            kv_p.append(rows_t)
            win_p.append(wrows_t[:, :, t_p - wl:])

            q, gates, rows, wrows = _nsa_inproj_sample(xs, g_mix, tabs_s, w_t)
            (kc, vc), n_cmp = _compress_sample(page_table, cache_t, a, rows, pe2, cw1, cw2t, t_s)
            o, new_win_t = _nsa_attn_sample(page_table, cache_t, win_t, a, kc, vc, q, rows, wrows, gates, t_s, n_cmp)
            xs = _outproj(xs, o, w_out)
            kv_s.append(rows.reshape(dec_b, t_s, 4, KV_HEADS, HEAD_DIM))
            win_s.append(new_win_t)
        else:
            r = i // 2
            w_in = hg_w_in[r].astype(BF16)
            w_out = hg_w_out[r].astype(BF16)
            ng = row(hg_norm[r])
            q, f, k, v, gs = _hg_inproj(xp, g_mix, hg_lb_logits, w_in, r)
            o, s_new = _hg_scan_prompt(q, f, k, v, gs, ng, batch, t_p)
            xp = _hg_outproj(xp, o, w_out)
            st_p.append(s_new)
            q, f, k, v, gs = _hg_inproj(xs, g_mix, hg_lb_logits, w_in, r)
            o, s_new = _hg_step_sample(q, f, k, v, gs, ng, state_hgrn, r, t_s)
            xs = _hg_outproj(xs, o, w_out)
            st_s.append(s_new)

        w1 = mlp_w1[i].astype(BF16)
        w2 = mlp_w2[i].astype(BF16)
        wgate = ple_w_gate[i].astype(BF16)
        wproj = ple_w_proj[i].astype(BF16)
        last = i == depth - 1
        xp = _mlp(xp, row(norm_mlp[i]), w1, w2)
        xs = _mlp(xs, row(norm_mlp[i]), w1, w2)
        xp = _ple(xp, pp, i, row(norm_ple[i]), wgate, wproj, row(norm_final), last)
        xs = _ple(xs, ps, i, row(norm_ple[i]), wgate, wproj, row(norm_final), last)

    return (xp.reshape(batch, t_p, d), xs.reshape(dec_b, t_s, d),
            _token_major_view(jnp.stack(kv_p), 4), jnp.stack(kv_s),
            _token_major_view(jnp.stack(win_p), 2), _token_major_view(jnp.stack(win_s), 2),
            jnp.stack(st_p), jnp.stack(st_s))
```

```python
import functools

import jax
import jax.numpy as jnp
from jax import lax
from jax.experimental import pallas as pl
from jax.experimental.pallas import tpu as pltpu

F32 = jnp.float32
BF16 = jnp.bfloat16

RMS_EPS = 1e-6
ROPE_THETA = 10000.0
N_HEADS = 16
HEAD_DIM = 64
KV_HEADS = 4
Q_PER_KV = N_HEADS // KV_HEADS
Q_DIM = N_HEADS * HEAD_DIM
KV_DIM = KV_HEADS * HEAD_DIM
N_BRANCH = 3
CMP_BLOCK = 32
SEL_BLOCK = 64
TOP_N = 16
WINDOW = 512
FORCED_CUR = 3e4
FORCED_FIRST = 2e4
HG_HEADS = 8
HG_DK = 128
HG_DV = 128
HG_CHUNK = 64
HG_SUB = 16
HG_UNROLL_PROMPT = 4
HG_UNROLL_SAMPLE = 8
NEG = -1e30
LOG2_E = 1.4426950408889634

LANES = 128
SUBLANES = 8
ROW_TILE = 512
ATT_TILE = 256
VMEM_LIMIT = 56 * 1024 * 1024


def _cparams(*sem):
    return pltpu.CompilerParams(dimension_semantics=sem, vmem_limit_bytes=VMEM_LIMIT)


def _full(shape):
    return pl.BlockSpec(shape, lambda *_: (0,) * len(shape))


def _rms(x, g):
    return x * lax.rsqrt(jnp.mean(x * x, axis=-1, keepdims=True) + RMS_EPS) * g


def _dot(a, b):
    return jnp.dot(a, b, preferred_element_type=F32)


def _dot_nt(a, b):
    return lax.dot_general(a, b, (((1,), (1,)), ((), ())), preferred_element_type=F32)


def _rope_table_kernel(inv_ref, invc_ref, cos_ref, sin_ref, cost_ref, sint_ref, *, pos0, period):
    i = pl.program_id(0)
    tm = cos_ref.shape[0]
    half = HEAD_DIM // 2
    row = i * tm + lax.broadcasted_iota(jnp.int32, (tm, LANES), 0)
    ang = (pos0 + row % period).astype(F32) * inv_ref[...]
    lane = lax.broadcasted_iota(jnp.int32, (tm, LANES), 1)
    cos_ref[...] = jnp.cos(ang)
    s = jnp.sin(ang)
    sin_ref[...] = jnp.where(lane % HEAD_DIM < half, -s, s)
    col = i * tm + lax.broadcasted_iota(jnp.int32, (HEAD_DIM, tm), 1)
    ang_t = (pos0 + col % period).astype(F32) * jnp.concatenate([invc_ref[...]] * (tm // LANES), axis=1)
    feat = lax.broadcasted_iota(jnp.int32, (HEAD_DIM, tm), 0)
    cost_ref[...] = jnp.cos(ang_t)
    st = jnp.sin(ang_t)
    sint_ref[...] = jnp.where(feat < half, -st, st)


def _rope_tables(rows, pos0, period):
    half = HEAD_DIM // 2
    inv = ROPE_THETA ** (-jnp.arange(half, dtype=F32) / half)
    inv_row = jnp.tile(inv, LANES // half)[None, :]
    inv_col = jnp.broadcast_to(jnp.tile(inv, HEAD_DIM // half)[:, None], (HEAD_DIM, LANES))
    kern = functools.partial(_rope_table_kernel, pos0=pos0, period=period)
    tm = min(ROW_TILE, rows)
    tok = pl.BlockSpec((tm, LANES), lambda i: (i, 0))
    feat = pl.BlockSpec((HEAD_DIM, tm), lambda i: (0, i))
    return pl.pallas_call(
        kern,
        grid=(rows // tm,),
        in_specs=[_full((1, LANES)), _full((HEAD_DIM, LANES))],
        out_specs=[tok, tok, feat, feat],
        out_shape=[jax.ShapeDtypeStruct((rows, LANES), F32)] * 2 + [jax.ShapeDtypeStruct((HEAD_DIM, rows), F32)] * 2,
        compiler_params=_cparams("parallel"),
        name="rope_tables",
    )(inv_row, inv_col)


def _rope_tok(y, cos, sin):
    lane = lax.broadcasted_iota(jnp.int32, y.shape, 1)
    first = lane % HEAD_DIM < HEAD_DIM // 2
    sw = jnp.where(first, pltpu.roll(y, LANES - HEAD_DIM // 2, 1), pltpu.roll(y, HEAD_DIM // 2, 1))
    return y * cos + sw * sin


def _rope_feat(y, cost, sint):
    return y * cost + pltpu.roll(y, HEAD_DIM // 2, 0) * sint


def _nsa_inproj_prompt_kernel(x_ref, g_ref, cos_ref, sin_ref, cost_ref, sint_ref, w_ref,
                              qt_ref, gatest_ref, cmp_ref, rowst_ref, wrowst_ref,
                              ksa_ref, vsa_ref, kwa_ref, vwa_ref):
    tm = x_ref.shape[0]
    tk = ksa_ref.shape[2]
    i = pl.program_id(1)
    xn = _rms(x_ref[...], g_ref[...]).astype(BF16)
    cos, sin = cos_ref[...], sin_ref[...]
    cost, sint = cost_ref[...], sint_ref[...]
    scale = HEAD_DIM ** -0.5 * LOG2_E

    qt = _dot_nt(w_ref[0:Q_DIM, :], xn)
    for h in range(N_HEADS):
        rs = slice(h * HEAD_DIM, (h + 1) * HEAD_DIM)
        qt_ref[rs, :] = (_rope_feat(qt[rs], cost, sint) * scale).astype(BF16)
    g0 = Q_DIM + 6 * KV_DIM
    gatest_ref[...] = jax.nn.sigmoid(_dot_nt(w_ref[g0:g0 + LANES, :], xn))

    ycmp = _dot_nt(xn, w_ref[Q_DIM:Q_DIM + 2 * KV_DIM, :])
    for c in range(2 * KV_DIM // LANES):
        cs = slice(c * LANES, (c + 1) * LANES)
        cmp_ref[:, cs] = _rope_tok(ycmp[:, cs], cos, sin) if c * LANES < KV_DIM else ycmp[:, cs]

    lane = lax.broadcasted_iota(jnp.int32, (tm, LANES), 1)
    lo = lane < HEAD_DIM
    pos = i * tm + lax.broadcasted_iota(jnp.int32, (tm, LANES), 0)
    sel_bias = jnp.where(lane - HEAD_DIM == pos // SEL_BLOCK, NEG, 0.0)
    for part, ref, fill in ((2, ksa_ref, sel_bias), (4, kwa_ref, 0.0)):
        yk = _dot_nt(xn, w_ref[Q_DIM + part * KV_DIM:Q_DIM + (part + 1) * KV_DIM, :])
        for c in range(KV_DIM // LANES):
            y = _rope_tok(yk[:, c * LANES:(c + 1) * LANES], cos, sin)
            pair = (jnp.where(lo, y, fill).astype(BF16), jnp.where(lo, pltpu.roll(y, HEAD_DIM, 1), fill).astype(BF16))
            for gg in range(2):
                for j in range(tm // tk):
                    ref[2 * c + gg, j] = pair[gg][j * tk:(j + 1) * tk, :]

    yt = _dot_nt(w_ref[Q_DIM:Q_DIM + 6 * KV_DIM, :], xn)
    ones = jnp.ones((HEAD_DIM, tm), F32)
    for part in range(6):
        for g in range(KV_HEADS):
            r0 = part * KV_DIM + g * HEAD_DIM
            y = yt[r0:r0 + HEAD_DIM, :]
            if part % 2 == 0:
                y = _rope_feat(y, cost, sint)
            if part < 4:
                rowst_ref[r0:r0 + HEAD_DIM, :] = y
            else:
                wrowst_ref[r0 - 4 * KV_DIM:r0 - 4 * KV_DIM + HEAD_DIM, :] = y
            if part in (3, 5):
                ref = vsa_ref if part == 3 else vwa_ref
                ya = jnp.concatenate([y, ones], axis=0).astype(BF16)
                for j in range(tm // tk):
                    ref[g, j] = ya[:, j * tk:(j + 1) * tk]


def _nsa_inproj_prompt(x, g, tabs, w, batch, t_seq):
    m, d = x.shape
    tm = min(ROW_TILE, t_seq)
    tk = min(ATT_TILE, t_seq)
    nt = t_seq // tm
    cos, sin, cost, sint = tabs
    row = lambda cols: pl.BlockSpec((tm, cols), lambda b, i: (b * nt + i, 0))
    tok_tab = pl.BlockSpec((tm, LANES), lambda b, i: (i, 0))
    feat_tab = pl.BlockSpec((HEAD_DIM, tm), lambda b, i: (0, i))
    featmaj = lambda rows: pl.BlockSpec((None, rows, tm), lambda b, i: (b, 0, i))
    ktiles = pl.BlockSpec((None, KV_HEADS, tm // tk, tk, LANES), lambda b, i: (b, 0, i, 0, 0))
    vtiles = pl.BlockSpec((None, KV_HEADS, tm // tk, LANES, tk), lambda b, i: (b, 0, i, 0, 0))
    kshape = jax.ShapeDtypeStruct((batch, KV_HEADS, t_seq // tk, tk, LANES), BF16)
    vshape = jax.ShapeDtypeStruct((batch, KV_HEADS, t_seq // tk, LANES, tk), BF16)
    return pl.pallas_call(
        _nsa_inproj_prompt_kernel,
        grid=(batch, nt),
        in_specs=[row(d), _full((1, d)), tok_tab, tok_tab, feat_tab, feat_tab, _full(w.shape)],
        out_specs=[featmaj(Q_DIM), featmaj(LANES), row(2 * KV_DIM), featmaj(4 * KV_DIM), featmaj(2 * KV_DIM),
                   ktiles, vtiles, ktiles, vtiles],
        out_shape=[jax.ShapeDtypeStruct((batch, Q_DIM, t_seq), BF16), jax.ShapeDtypeStruct((batch, LANES, t_seq), F32),
                   jax.ShapeDtypeStruct((m, 2 * KV_DIM), F32),
                   jax.ShapeDtypeStruct((batch, 4 * KV_DIM, t_seq), F32),
                   jax.ShapeDtypeStruct((batch, 2 * KV_DIM, t_seq), F32), kshape, vshape, kshape, vshape],
        compiler_params=_cparams("parallel", "parallel"),
        name="nsa_inproj_prompt",
    )(x, g, cos, sin, cost, sint, w)


def _nsa_inproj_sample_kernel(x_ref, g_ref, cos_ref, sin_ref, w_ref, q_ref, gates_ref, rows_ref, wrows_ref):
    xn = _rms(x_ref[...], g_ref[...]).astype(BF16)
    cos, sin = cos_ref[...], sin_ref[...]
    scale = HEAD_DIM ** -0.5
    yq = _dot_nt(xn, w_ref[0:Q_DIM, :])
    for c in range(Q_DIM // LANES):
        cs = slice(c * LANES, (c + 1) * LANES)
        q_ref[:, cs] = (_rope_tok(yq[:, cs], cos, sin) * scale).astype(BF16)
    g0 = Q_DIM + 6 * KV_DIM
    gates_ref[...] = jax.nn.sigmoid(_dot_nt(xn, w_ref[g0:g0 + LANES, :]))
    ykv = _dot_nt(xn, w_ref[Q_DIM:g0, :])
    for c in range(6 * KV_DIM // LANES):
        cs = slice(c * LANES, (c + 1) * LANES)
        part = c * LANES // KV_DIM
        y = _rope_tok(ykv[:, cs], cos, sin) if part % 2 == 0 else ykv[:, cs]
        if part < 4:
            rows_ref[:, cs] = y
        else:
            wrows_ref[:, c * LANES - 4 * KV_DIM:(c + 1) * LANES - 4 * KV_DIM] = y


def _nsa_inproj_sample(x, g, tabs, w):
    m, d = x.shape
    tm = min(ROW_TILE, m)
    cos, sin = tabs[0], tabs[1]
    n_tab = cos.shape[0] // tm
    row = lambda cols: pl.BlockSpec((tm, cols), lambda i: (i, 0))
    tab = pl.BlockSpec((tm, LANES), lambda i: (i % n_tab, 0))
    return pl.pallas_call(
        _nsa_inproj_sample_kernel,
        grid=(m // tm,),
        in_specs=[row(d), _full((1, d)), tab, tab, _full(w.shape)],
        out_specs=[row(d), row(LANES), row(4 * KV_DIM), row(2 * KV_DIM)],
        out_shape=[jax.ShapeDtypeStruct((m, d), BF16), jax.ShapeDtypeStruct((m, LANES), F32),
                   jax.ShapeDtypeStruct((m, 4 * KV_DIM), F32), jax.ShapeDtypeStruct((m, 2 * KV_DIM), F32)],
        compiler_params=_cparams("parallel"),
        name="nsa_inproj_sample",
    )(x, g, cos, sin, w)


def _compress_rows(load, pe_ref, w1_ref, w2t_ref, kv, n, transposed=False):
    lane = lax.broadcasted_iota(jnp.int32, (n, LANES), 1)
    lo = lane < HEAD_DIM
    acc = jnp.zeros((KV_HEADS * n, w1_ref.shape[-1]), F32)
    lq = 4
    for l0 in range(0, CMP_BLOCK, lq):
        halves = [[], []]
        for l in range(l0, l0 + lq, 2):
            for hf in range(2):
                a = load(l, hf) + pe_ref[kv, l:l + 1, :]
                b = load(l + 1, hf) + pe_ref[kv, l + 1:l + 2, :]
                even = jnp.where(lo, a, pltpu.roll(b, HEAD_DIM, 1))
                odd = jnp.where(lo, pltpu.roll(a, HEAD_DIM, 1), b)
                halves[hf].append((even, odd))
        groups = []
        for g in range(KV_HEADS):
            hf, par = divmod(g, 2)
            groups.append(jnp.concatenate([pc[par] for pc in halves[hf]], axis=1))
        xg = jnp.concatenate(groups, axis=0).astype(BF16)
        acc = acc + _dot(xg, w1_ref[kv, l0 * HEAD_DIM:(l0 + lq) * HEAD_DIM, :])
    h = jax.nn.gelu(acc).astype(BF16)
    return _dot_nt(w2t_ref[kv], h) if transposed else _dot_nt(h, w2t_ref[kv])


def _compress_prompt_kernel(s0_ref, s1_ref, s2_ref, s3_ref, pe_ref, w1_ref, w2t_ref, kc_ref, vct_ref):
    n = kc_ref.shape[0]
    hn = n // 2
    srcs = ((s0_ref, s1_ref), (s2_ref, s3_ref))

    def loader(kv):
        def load(l, hf):
            ev = srcs[kv][hf][pl.ds(l, hn, stride=2 * CMP_BLOCK), :]
            od = srcs[kv][hf][pl.ds(CMP_BLOCK + l, hn, stride=2 * CMP_BLOCK), :]
            return jnp.concatenate([ev, od], axis=0)
        return load

    y = _compress_rows(loader(0), pe_ref, w1_ref, w2t_ref, 0, n)
    yt = _compress_rows(loader(1), pe_ref, w1_ref, w2t_ref, 1, n, transposed=True)
    for g in range(KV_HEADS):
        kc_ref[:, g * HEAD_DIM:(g + 1) * HEAD_DIM] = y[g * n:(g + 1) * n].astype(BF16)
        vct_ref[g * HEAD_DIM:(g + 1) * HEAD_DIM, :] = yt[:, g * n:(g + 1) * n].astype(BF16)


def _compress_prompt(cmp_rows, pe2, w1, w2t, batch, t_seq):
    n = t_seq // CMP_BLOCK
    chunk = lambda c: pl.BlockSpec((t_seq, LANES), lambda b: (b, c))
    return pl.pallas_call(
        _compress_prompt_kernel,
        grid=(batch,),
        in_specs=[chunk(0), chunk(1), chunk(2), chunk(3), _full(pe2.shape), _full(w1.shape), _full(w2t.shape)],
        out_specs=[pl.BlockSpec((None, n, KV_DIM), lambda b: (b, 0, 0)),
                   pl.BlockSpec((None, KV_DIM, n), lambda b: (b, 0, 0))],
        out_shape=[jax.ShapeDtypeStruct((batch, n, KV_DIM), BF16), jax.ShapeDtypeStruct((batch, KV_DIM, n), BF16)],
        compiler_params=_cparams("parallel"),
        name="nsa_compress_prompt",
    )(cmp_rows, cmp_rows, cmp_rows, cmp_rows, pe2, w1, w2t)


def _topk_not_selected(imp_t, n_live, rank_ref):
    n_sel, nq = imp_t.shape
    slabs = [imp_t[r0:r0 + SUBLANES] for r0 in range(0, n_sel, SUBLANES)]
    sub = lax.broadcasted_iota(jnp.int32, (SUBLANES, nq), 0)
    rank_ref[...] = jnp.zeros((n_sel, nq), F32)
    for c0 in range(0, n_sel, SUBLANES):
        @pl.when(c0 < n_live)
        def _():
            for v, slab in enumerate(slabs):
                r0 = v * SUBLANES
                cnt = jnp.zeros((SUBLANES, nq), F32)
                for bp in range(c0, c0 + SUBLANES):
                    other = imp_t[bp:bp + 1, :]
                    if bp < r0:
                        ahead = other >= slab
                    elif bp >= r0 + SUBLANES - 1:
                        ahead = other > slab
                    else:
                        ahead = (other > slab) | ((other == slab) & (bp - r0 < sub))
                    cnt = cnt + jnp.where(ahead, 1.0, 0.0)
                rank_ref[r0:r0 + SUBLANES, :] += cnt
    return jnp.where(rank_ref[...] < TOP_N, 0.0, 1.0)


def _nsa_attn_prompt_kernel(qt_ref, gatest_ref, kc_ref, vct_ref, ksa_ref, vsa_ref, kwa_ref, vwa_ref, o_ref,
                            acc_ref, m_ref, accw_ref, mw_ref, sa_ref, sb_ref, wb_ref, cb_ref, cv_ref, rank_ref):
    tq = qt_ref.shape[1]
    tk = ksa_ref.shape[2]
    n_cmp = kc_ref.shape[0]
    n_sel = n_cmp // 2
    rq = Q_PER_KV * tq
    i = pl.program_id(1)
    t0 = i * tq
    t_col = t0 + lax.broadcasted_iota(jnp.int32, (1, rq), 1) % tq
    key_row = lax.broadcasted_iota(jnp.int32, (tk, rq), 0)
    n_wt = -(-(WINDOW - 1) // tk) + 1
    kt_diag = t0 // tk
    wkt = [kt_diag - j for j in range(n_wt)]

    for j, kt in enumerate(wkt):
        kpos = kt * tk + key_row
        wb_ref[j] = jnp.where((kpos >= 0) & (kpos <= t_col) & (kpos > t_col - WINDOW), 0.0, NEG)
    nrow = lax.broadcasted_iota(jnp.int32, (n_cmp, rq), 0)
    cblk = jnp.where(nrow < n_sel, 2 * nrow, 2 * (nrow - n_sel) + 1)
    cvis = (cblk + 1) * CMP_BLOCK - 1 <= t_col
    cb_ref[...] = jnp.where(cvis, 0.0, NEG)
    cv_ref[...] = jnp.where(cvis, 1.0, 0.0)

    def online_update(s_ref, vt, bias, m_ref, acc_ref):
        s = s_ref[...]
        if bias is not None:
            s = s + bias
        m_old = m_ref[...]
        m_new = jnp.maximum(m_old, jnp.max(s, axis=0, keepdims=True))
        p = jnp.exp2(s - m_new)
        acc_ref[...] = acc_ref[...] * jnp.exp2(m_old - m_new) + _dot(vt, p.astype(BF16))
        m_ref[...] = m_new

    def normalized(acc):
        return acc[:HEAD_DIM] / jnp.maximum(acc[HEAD_DIM:HEAD_DIM + 1], 1e-30)

    for g in range(KV_HEADS):
        qg = jnp.concatenate(
            [qt_ref[(Q_PER_KV * g + r) * HEAD_DIM:(Q_PER_KV * g + r + 1) * HEAD_DIM, :] for r in range(Q_PER_KV)],
            axis=1)

        sc = _dot(kc_ref[:, g * HEAD_DIM:(g + 1) * HEAD_DIM], qg) + cb_ref[...]
        e = jnp.exp2(sc - jnp.max(sc, axis=0, keepdims=True)) * cv_ref[...]
        pc = e / jnp.maximum(jnp.sum(e, axis=0, keepdims=True), 1e-30)
        o_c = _dot(vct_ref[g * HEAD_DIM:(g + 1) * HEAD_DIM, :], pc.astype(BF16))

        pq = pc[:, 0:tq]
        for r in range(1, Q_PER_KV):
            pq = pq + pc[:, r * tq:(r + 1) * tq]
        imp = pq[:n_sel] + pq[n_sel:]
        blk = lax.broadcasted_iota(jnp.int32, (n_sel, tq), 0)
        cur = (t0 + lax.broadcasted_iota(jnp.int32, (n_sel, tq), 1)) // SEL_BLOCK
        imp = jnp.where(blk > cur, -1.0, imp)
        imp = jnp.where(blk == 0, FORCED_FIRST, imp)
        imp = jnp.where(blk == cur, FORCED_CUR, imp)
        n_live = (t0 + tq - 1) // SEL_BLOCK + 1
        notsel = _topk_not_selected(imp, n_live, rank_ref)
        if n_sel < HEAD_DIM:
            notsel = jnp.concatenate([notsel, jnp.zeros((HEAD_DIM - n_sel, tq), F32)], axis=0)
        qa = jnp.concatenate([qg, jnp.concatenate([notsel.astype(BF16)] * Q_PER_KV, axis=1)], axis=0)

        for ref in (m_ref, mw_ref):
            ref[...] = jnp.full((1, rq), NEG, F32)
        for ref in (acc_ref, accw_ref):
            ref[...] = jnp.zeros((LANES, rq), F32)
        slots = [sa_ref, sb_ref]
        qw = jnp.concatenate([qg, jnp.zeros_like(qg)], axis=0)

        slots[0][...] = _dot(kwa_ref[g, wkt[0]], qw)
        for j, kt in enumerate(wkt):
            if j + 1 < n_wt:
                slots[1 - j % 2][...] = _dot(kwa_ref[g, jnp.maximum(wkt[j + 1], 0)], qw)
            else:
                slots[1 - j % 2][...] = _dot(ksa_ref[g, 0], qa)
            online_update(slots[j % 2], vwa_ref[g, jnp.maximum(kt, 0)], wb_ref[j], mw_ref, accw_ref)
        o_w = normalized(accw_ref[...])

        s_even, s_odd = slots[n_wt % 2], slots[1 - n_wt % 2]

        def sel_pair(j, carry):
            s_odd[...] = _dot(ksa_ref[g, 2 * j + 1], qa)
            online_update(s_even, vsa_ref[g, 2 * j], None, m_ref, acc_ref)
            s_even[...] = _dot(ksa_ref[g, 2 * j + 2], qa)
            online_update(s_odd, vsa_ref[g, 2 * j + 1], None, m_ref, acc_ref)
            return carry

        lax.fori_loop(0, kt_diag // 2, sel_pair, 0)

        @pl.when(kt_diag % 2 == 1)
        def _():
            s_odd[...] = _dot(ksa_ref[g, kt_diag], qa)
            online_update(s_even, vsa_ref[g, kt_diag - 1], None, m_ref, acc_ref)
            online_update(s_odd, vsa_ref[g, kt_diag], wb_ref[0], m_ref, acc_ref)

        @pl.when(kt_diag % 2 == 0)
        def _():
            online_update(s_even, vsa_ref[g, kt_diag], wb_ref[0], m_ref, acc_ref)

        o_s = normalized(acc_ref[...])

        outs = []
        for r in range(Q_PER_KV):
            h = Q_PER_KV * g + r
            cs = slice(r * tq, (r + 1) * tq)
            gc = gatest_ref[N_BRANCH * h:N_BRANCH * h + 1, :]
            gs = gatest_ref[N_BRANCH * h + 1:N_BRANCH * h + 2, :]
            gw = gatest_ref[N_BRANCH * h + 2:N_BRANCH * h + 3, :]
            outs.append(gc * o_c[:, cs] + gs * o_s[:, cs] + gw * o_w[:, cs])
        for pr in range(Q_PER_KV // 2):
            c0 = (Q_PER_KV * g + 2 * pr) * HEAD_DIM
            o_ref[:, c0:c0 + LANES] = jnp.concatenate(outs[2 * pr:2 * pr + 2], axis=0).T.astype(BF16)


def _nsa_attn_prompt(qt, gatest, kc, vct, ksa, vsa, kwa, vwa, batch, t_seq):
    d = qt.shape[1]
    tq = min(ATT_TILE, t_seq)
    nq = t_seq // tq
    n_cmp = t_seq // CMP_BLOCK
    tk = ksa.shape[3]
    featmaj = lambda rows: pl.BlockSpec((None, rows, tq), lambda b, i: (b, 0, i))
    seq = lambda shape: pl.BlockSpec((None,) + shape[1:], lambda b, i: (b, 0, 0, 0, 0))
    return pl.pallas_call(
        _nsa_attn_prompt_kernel,
        grid=(batch, nq),
        in_specs=[featmaj(d), featmaj(LANES),
                  pl.BlockSpec((None, n_cmp, KV_DIM), lambda b, i: (b, 0, 0)),
                  pl.BlockSpec((None, KV_DIM, n_cmp), lambda b, i: (b, 0, 0)),
                  seq(ksa.shape), seq(vsa.shape), seq(kwa.shape), seq(vwa.shape)],
        out_specs=pl.BlockSpec((tq, d), lambda b, i: (b * nq + i, 0)),
        out_shape=jax.ShapeDtypeStruct((batch * t_seq, d), BF16),
        scratch_shapes=[pltpu.VMEM((LANES, Q_PER_KV * tq), F32), pltpu.VMEM((1, Q_PER_KV * tq), F32),
                        pltpu.VMEM((LANES, Q_PER_KV * tq), F32), pltpu.VMEM((1, Q_PER_KV * tq), F32),
                        pltpu.VMEM((tk, Q_PER_KV * tq), F32), pltpu.VMEM((tk, Q_PER_KV * tq), F32),
                        pltpu.VMEM((-(-(WINDOW - 1) // tk) + 1, tk, Q_PER_KV * tq), F32),
                        pltpu.VMEM((n_cmp, Q_PER_KV * tq), F32), pltpu.VMEM((n_cmp, Q_PER_KV * tq), F32),
                        pltpu.VMEM((n_cmp // 2, tq), F32)],
        compiler_params=_cparams("parallel", "arbitrary"),
        name="nsa_attn_prompt",
    )(qt, gatest, kc, vct, ksa, vsa, kwa, vwa)


def _compress_sample_kernel(pt_ref, *refs, n_pages, page, t_new, n_cmp):
    del pt_ref
    pages = refs[:n_pages]
    new_ref, pe_ref, w1_ref, w2t_ref, kc_ref, vc_ref = refs[n_pages:n_pages + 6]
    bufs = refs[n_pages + 6:]
    n_pad = bufs[0].shape[1]
    n_past = n_pages * page // CMP_BLOCK
    grp = SUBLANES * CMP_BLOCK
    ppg = grp // page
    tt = lax.broadcasted_iota(jnp.int32, (grp, grp), 0)
    perm_t = jnp.where(lax.broadcasted_iota(jnp.int32, (grp, grp), 1)
                       == (tt % CMP_BLOCK) * SUBLANES + tt // CMP_BLOCK, 1.0, 0.0).astype(BF16)
    for c, buf_ref in enumerate(bufs):
        cs = slice(c * LANES, (c + 1) * LANES)
        for pp in range(n_pages // ppg):
            xt = jnp.concatenate([pages[pp * ppg + j][cs, :] for j in range(ppg)], axis=1)
            xp = _dot(xt.astype(BF16), perm_t).T
            for l in range(CMP_BLOCK):
                buf_ref[l, pp * SUBLANES:(pp + 1) * SUBLANES, :] = xp[l * SUBLANES:(l + 1) * SUBLANES, :]
        buf_ref[:, n_past:, :] = jnp.zeros((CMP_BLOCK, n_pad - n_past, LANES), F32)
        for l in range(t_new):
            buf_ref[l, n_past:n_past + 1, :] = new_ref[l:l + 1, cs]
    n_out = kc_ref.shape[0]
    for kv, out_ref in ((0, kc_ref), (1, vc_ref)):
        def load(l, hf, kv=kv):
            return bufs[2 * kv + hf][l]
        y = _compress_rows(load, pe_ref, w1_ref, w2t_ref, kv, n_pad)
        valid = lax.broadcasted_iota(jnp.int32, (n_pad, HEAD_DIM), 0) < n_cmp
        for g in range(KV_HEADS):
            out_ref[0:n_pad, g * HEAD_DIM:(g + 1) * HEAD_DIM] = jnp.where(
                valid, y[g * n_pad:(g + 1) * n_pad], 0.0).astype(BF16)
        out_ref[n_pad:, :] = jnp.zeros((n_out - n_pad, KV_DIM), BF16)


def _page_specs(layer, n_pages, page, feat_block):
    half = 2 * KV_DIM
    return [pl.BlockSpec((None, None, half, page),
                         functools.partial(lambda b, pt, p: (layer, pt[b, p], feat_block, 0), p=p))
            for p in range(n_pages)]


def _compress_sample(page_table, cache_t, layer, rows_new, pe2, w1, w2t, t_new):
    dec_b, n_pages = page_table.shape
    page = cache_t.shape[-1]
    past = n_pages * page
    l_pad = -(-(past + t_new) // SEL_BLOCK) * SEL_BLOCK
    n_cmp = l_pad // CMP_BLOCK
    n_pad = -(-n_cmp // SUBLANES) * SUBLANES
    n_out = LANES
    half = 2 * KV_DIM
    kern = functools.partial(_compress_sample_kernel, n_pages=n_pages, page=page, t_new=t_new, n_cmp=n_cmp)
    grid_spec = pltpu.PrefetchScalarGridSpec(
        num_scalar_prefetch=1,
        grid=(dec_b,),
        in_specs=_page_specs(layer, n_pages, page, 0) + [
            pl.BlockSpec((t_new, half), lambda b, pt: (b, 0)),
            pl.BlockSpec(pe2.shape, lambda b, pt: (0, 0, 0)),
            pl.BlockSpec(w1.shape, lambda b, pt: (0, 0, 0)),
            pl.BlockSpec(w2t.shape, lambda b, pt: (0, 0, 0))],
        out_specs=[pl.BlockSpec((None, n_out, KV_DIM), lambda b, pt: (b, 0, 0))] * 2,
        scratch_shapes=[pltpu.VMEM((CMP_BLOCK, n_pad, LANES), F32)] * (half // LANES),
    )
    return pl.pallas_call(
        kern,
        grid_spec=grid_spec,
        out_shape=[jax.ShapeDtypeStruct((dec_b, n_out, KV_DIM), BF16)] * 2,
        compiler_params=_cparams("arbitrary"),
        name="nsa_compress_sample",
    )(page_table, *([cache_t] * n_pages), rows_new, pe2, w1, w2t), n_cmp


def _nsa_attn_sample_kernel(pt_ref, *refs, n_pages, page, t_new, n_cmp):
    del pt_ref
    pages = refs[:n_pages]
    (kc_ref, vc_ref, q_ref, new_ref, wnew_ref, win_ref, gates_ref,
     o_ref, nwin_ref, kst_ref, vst_ref) = refs[n_pages:]
    past = n_pages * page
    n_buf = win_ref.shape[1]
    n_sel = n_cmp // 2
    gq = Q_PER_KV * t_new
    rq = KV_HEADS * gq
    nk = past + LANES
    nw = n_buf + LANES

    for p in range(n_pages):
        blk = pages[p][...]
        kst_ref[:, p * page:(p + 1) * page] = blk[:KV_DIM].astype(BF16)
        vst_ref[:, p * page:(p + 1) * page] = blk[KV_DIM:].astype(BF16)
    pad = jnp.zeros((LANES - t_new, KV_DIM), F32)
    new = new_ref[...]
    wnew = wnew_ref[...]
    knew = jnp.concatenate([new[:, :KV_DIM], pad], axis=0).astype(BF16)
    vnew = jnp.concatenate([new[:, KV_DIM:], pad], axis=0).astype(BF16)
    kwnew = jnp.concatenate([wnew[:, :KV_DIM], pad], axis=0).astype(BF16)
    vwnew = jnp.concatenate([wnew[:, KV_DIM:], pad], axis=0).astype(BF16)
    win = win_ref[...]
    kwt = win[:KV_DIM].astype(BF16)
    vwt = win[KV_DIM:].astype(BF16)

    rolled = pltpu.roll(win, n_buf - t_new, 1)
    wnew_t = jnp.concatenate([jnp.zeros((LANES - t_new, 2 * KV_DIM), F32), wnew], axis=0).T
    tail_lane = lax.broadcasted_iota(jnp.int32, (2 * KV_DIM, LANES), 1)
    nwin_ref[:, 0:n_buf - LANES] = rolled[:, 0:n_buf - LANES]
    nwin_ref[:, n_buf - LANES:] = jnp.where(tail_lane >= LANES - t_new, wnew_t, rolled[:, n_buf - LANES:])

    q = q_ref[...]
    lane_g = lax.broadcasted_iota(jnp.int32, (t_new, KV_DIM), 1) // HEAD_DIM
    pieces = []
    for g in range(KV_HEADS):
        for j in range(Q_PER_KV):
            qj = jnp.concatenate([q[:, (Q_PER_KV * gg + j) * HEAD_DIM:(Q_PER_KV * gg + j + 1) * HEAD_DIM]
                                  for gg in range(KV_HEADS)], axis=1)
            pieces.append(jnp.where(lane_g == g, qj, jnp.zeros_like(qj)))
    qbd = jnp.concatenate(pieces, axis=0)
    t_row = past + lax.broadcasted_iota(jnp.int32, (rq, 1), 0) % t_new

    def softmax(s, mask):
        s = jnp.where(mask, s, NEG)
        e = jnp.where(mask, jnp.exp(s - jnp.max(s, axis=1, keepdims=True)), 0.0)
        return e / jnp.maximum(jnp.sum(e, axis=1, keepdims=True), 1e-30)

    ncl = kc_ref.shape[0]
    sc = _dot_nt(qbd, kc_ref[...])
    cidx = lax.broadcasted_iota(jnp.int32, (rq, ncl), 1)
    pc = softmax(sc, (cidx < n_cmp) & ((cidx + 1) * CMP_BLOCK - 1 <= t_row))
    oc = _dot(pc.astype(BF16), vc_ref[...])

    groups = []
    for g in range(KV_HEADS):
        pg = pc[g * gq:g * gq + t_new]
        for j in range(1, Q_PER_KV):
            pg = pg + pc[g * gq + j * t_new:g * gq + (j + 1) * t_new]
        groups.append(pg)
    pg = jnp.concatenate(groups, axis=0)
    nr = KV_HEADS * t_new
    imp = pg + pltpu.roll(pg, ncl - 1, 1)
    lane = lax.broadcasted_iota(jnp.int32, (nr, ncl), 1)
    blk = lane // 2
    cur = (past + lax.broadcasted_iota(jnp.int32, (nr, ncl), 0) % t_new) // SEL_BLOCK
    imp = jnp.where(blk > cur, -1.0, imp)
    imp = jnp.where(blk == 0, FORCED_FIRST, imp)
    imp = jnp.where(blk == cur, FORCED_CUR, imp)
    rank = jnp.zeros((nr, ncl), jnp.int32)
    for bp in range(n_sel):
        other = imp[:, 2 * bp:2 * bp + 1]
        rank = rank + ((other > imp) | ((other == imp) & (bp < blk))).astype(jnp.int32)
    sel = jnp.where((rank < TOP_N) & (lane % 2 == 0) & (blk < n_sel), 1.0, 0.0)
    sel_rows = jnp.concatenate([sel[g * t_new:(g + 1) * t_new] for g in range(KV_HEADS) for _ in range(Q_PER_KV)],
                               axis=0).astype(BF16)
    expand = jnp.where(lax.broadcasted_iota(jnp.int32, (ncl, nk), 0)
                       == 2 * (lax.broadcasted_iota(jnp.int32, (ncl, nk), 1) // SEL_BLOCK), 1.0, 0.0).astype(BF16)
    selmask = _dot(sel_rows, expand) > 0.5

    kpos = lax.broadcasted_iota(jnp.int32, (rq, nk), 1)
    ss = jnp.concatenate([_dot(qbd, kst_ref[...]), _dot_nt(qbd, knew)], axis=1)
    ps = softmax(ss, selmask & (kpos <= t_row)).astype(BF16)
    os_ = _dot_nt(ps[:, :past], vst_ref[...]) + _dot(ps[:, past:], vnew)

    wpos = past - n_buf + lax.broadcasted_iota(jnp.int32, (rq, nw), 1)
    sw = jnp.concatenate([_dot(qbd, kwt), _dot_nt(qbd, kwnew)], axis=1)
    pw = softmax(sw, (wpos <= t_row) & (wpos > t_row - WINDOW)).astype(BF16)
    ow = _dot_nt(pw[:, :n_buf], vwt) + _dot(pw[:, n_buf:], vwnew)

    gates = gates_ref[...]
    for g in range(KV_HEADS):
        for j in range(Q_PER_KV):
            h = Q_PER_KV * g + j
            rs = slice(g * gq + j * t_new, g * gq + (j + 1) * t_new)
            cs = slice(g * HEAD_DIM, (g + 1) * HEAD_DIM)
            gc = gates[:, N_BRANCH * h:N_BRANCH * h + 1]
            gs = gates[:, N_BRANCH * h + 1:N_BRANCH * h + 2]
            gw = gates[:, N_BRANCH * h + 2:N_BRANCH * h + 3]
            o_ref[:, h * HEAD_DIM:(h + 1) * HEAD_DIM] = (
                gc * oc[rs, cs] + gs * os_[rs, cs] + gw * ow[rs, cs]).astype(BF16)


def _nsa_attn_sample(page_table, cache_t, win_t, layer, kc, vc, q, rows_new, wrows_new, gates, t_new, n_cmp):
    dec_b, n_pages = page_table.shape
    page = cache_t.shape[-1]
    past = n_pages * page
    n_buf = win_t.shape[-1]
    half = 2 * KV_DIM
    d = q.shape[1]
    ncl = kc.shape[1]
    kern = functools.partial(_nsa_attn_sample_kernel, n_pages=n_pages, page=page, t_new=t_new, n_cmp=n_cmp)
    grid_spec = pltpu.PrefetchScalarGridSpec(
        num_scalar_prefetch=1,
        grid=(dec_b,),
        in_specs=_page_specs(layer, n_pages, page, 1) + [
            pl.BlockSpec((None, ncl, KV_DIM), lambda b, pt: (b, 0, 0)),
            pl.BlockSpec((None, ncl, KV_DIM), lambda b, pt: (b, 0, 0)),
            pl.BlockSpec((t_new, d), lambda b, pt: (b, 0)),
            pl.BlockSpec((t_new, half), lambda b, pt: (b, 1)),
            pl.BlockSpec((t_new, half), lambda b, pt: (b, 0)),
            pl.BlockSpec((None, None, half, n_buf), lambda b, pt: (layer, b, 0, 0)),
            pl.BlockSpec((t_new, LANES), lambda b, pt: (b, 0))],
        out_specs=[pl.BlockSpec((t_new, d), lambda b, pt: (b, 0)),
                   pl.BlockSpec((None, half, n_buf), lambda b, pt: (b, 0, 0))],
        scratch_shapes=[pltpu.VMEM((KV_DIM, past), BF16), pltpu.VMEM((KV_DIM, past), BF16)],
    )
    return pl.pallas_call(
        kern,
        grid_spec=grid_spec,
        out_shape=[jax.ShapeDtypeStruct((dec_b * t_new, d), BF16), jax.ShapeDtypeStruct((dec_b, half, n_buf), F32)],
        compiler_params=_cparams("arbitrary"),
        name="nsa_attn_sample",
    )(page_table, *([cache_t] * n_pages), kc, vc, q, rows_new, wrows_new, win_t, gates)


def _outproj_kernel(x_ref, o_ref, w_ref, y_ref):
    y_ref[...] = x_ref[...] + _dot(o_ref[...], w_ref[...])


def _outproj(x, o, w):
    m, d = x.shape
    tm = min(ROW_TILE, m)
    row = pl.BlockSpec((tm, d), lambda i: (i, 0))
    return pl.pallas_call(
        _outproj_kernel,
        grid=(m // tm,),
        in_specs=[row, row, _full(w.shape)],
        out_specs=row,
        out_shape=jax.ShapeDtypeStruct((m, d), F32),
        compiler_params=_cparams("parallel"),
        name="mixer_outproj",
    )(x, o, w)


def _mlp_kernel(x_ref, g_ref, w1_ref, w2_ref, y_ref, xn_ref, acc_ref):
    j = pl.program_id(1)

    @pl.when(j == 0)
    def _():
        x = x_ref[...]
        xn_ref[...] = _rms(x, g_ref[...]).astype(BF16)
        acc_ref[...] = x

    h = jnp.maximum(_dot(xn_ref[...], w1_ref[...]), 0.0)
    acc_ref[...] += _dot((h * h).astype(BF16), w2_ref[...])

    @pl.when(j == pl.num_programs(1) - 1)
    def _():
        y_ref[...] = acc_ref[...]


def _mlp(x, g, w1, w2):
    m, d = x.shape
    ff = w1.shape[1]
    tm, tf = min(ROW_TILE, m), 1024
    row = pl.BlockSpec((tm, d), lambda i, j: (i, 0))
    return pl.pallas_call(
        _mlp_kernel,
        grid=(m // tm, ff // tf),
        in_specs=[row, pl.BlockSpec((1, d), lambda i, j: (0, 0)),
                  pl.BlockSpec((d, tf), lambda i, j: (0, j)), pl.BlockSpec((tf, d), lambda i, j: (j, 0))],
        out_specs=row,
        out_shape=jax.ShapeDtypeStruct((m, d), F32),
        scratch_shapes=[pltpu.VMEM((tm, d), BF16), pltpu.VMEM((tm, d), F32)],
        compiler_params=_cparams("parallel", "arbitrary"),
        name="sqrelu_mlp",
    )(x, g, w1, w2)


def _ple_kernel(x_ref, p_ref, g_ref, wg_ref, wp_ref, gf_ref, y_ref, *, final_norm):
    x = x_ref[...]
    gate = jax.nn.sigmoid(_dot(_rms(x, g_ref[...]).astype(BF16), wg_ref[...]))
    y = x + _dot(p_ref[...].astype(BF16), wp_ref[...]) * gate
    if final_norm:
        y = _rms(y, gf_ref[...])
    y_ref[...] = y


def _ple(x, p_all, layer, g, wg, wp, gf, final_norm):
    m, d = x.shape
    tm = min(ROW_TILE, m)
    row = pl.BlockSpec((tm, d), lambda i: (i, 0))
    return pl.pallas_call(
        functools.partial(_ple_kernel, final_norm=final_norm),
        grid=(m // tm,),
        in_specs=[row, pl.BlockSpec((None, tm, p_all.shape[-1]), lambda i: (layer, i, 0)), _full((1, d)),
                  _full(wg.shape), _full(wp.shape), _full((1, d))],
        out_specs=row,
        out_shape=jax.ShapeDtypeStruct((m, d), F32),
        compiler_params=_cparams("parallel"),
        name="ple_gate",
    )(x, p_all, g, wg, wp, gf)


def _hg_inproj_kernel(x_ref, g_ref, lbl_ref, w_ref, q_ref, f_ref, k_ref, v_ref, gs_ref, xn_ref, *, layer):
    j = pl.program_id(1)

    @pl.when(j == 0)
    def _():
        xn_ref[...] = _rms(x_ref[...], g_ref[...]).astype(BF16)

    y = _dot(xn_ref[...], w_ref[...])

    def put(ref, val):
        for h in range(HG_HEADS):
            ref[h] = val[:, h * HG_DK:(h + 1) * HG_DK]

    @pl.when(j == 0)
    def _():
        put(q_ref, jax.nn.silu(y) * HG_DK ** -0.5)

    @pl.when(j == 1)
    def _():
        lg = lbl_ref[...]
        e = jnp.exp(lg - jnp.max(lg, axis=0, keepdims=True))
        w = e / jnp.sum(e, axis=0, keepdims=True)
        cs = w[0:1]
        for r in range(1, layer + 1):
            cs = cs + w[r:r + 1]
        lb = cs - w[0:1]
        sg = jax.nn.sigmoid(y)
        put(f_ref, jnp.log(lb + (1.0 - lb) * sg))
        put(k_ref, (1.0 - lb) * (1.0 - sg))

    @pl.when(j == 2)
    def _():
        put(v_ref, y)

    @pl.when(j == 3)
    def _():
        put(gs_ref, jax.nn.silu(y))


def _hg_inproj(x, g, lb_logits, w, layer):
    m, d = x.shape
    tm = min(ROW_TILE, m)
    head_major = pl.BlockSpec((HG_HEADS, tm, HG_DK), lambda i, j: (0, i, 0))
    return pl.pallas_call(
        functools.partial(_hg_inproj_kernel, layer=layer),
        grid=(m // tm, 4),
        in_specs=[pl.BlockSpec((tm, d), lambda i, j: (i, 0)), pl.BlockSpec((1, d), lambda i, j: (0, 0)),
                  pl.BlockSpec(lb_logits.shape, lambda i, j: (0, 0)), pl.BlockSpec((d, d), lambda i, j: (0, j))],
        out_specs=[head_major] * 5,
        out_shape=[jax.ShapeDtypeStruct((HG_HEADS, m, HG_DK), F32)] * 5,
        scratch_shapes=[pltpu.VMEM((tm, d), BF16)],
        compiler_params=_cparams("parallel", "arbitrary"),
        name="hgrn_inproj",
    )(x, g, lb_logits, w)


def _cumsum_rows(f):
    c = f.shape[0]
    row = lax.broadcasted_iota(jnp.int32, f.shape, 0)
    s = 1
    while s < c:
        f = f + jnp.where(row >= s, pltpu.roll(f, s, 0), 0.0)
        s *= 2
    return f


def _hg_intra(q, k, v, b, o, sub):
    c = q.shape[0]
    ones = jnp.ones((HG_DK, HG_DV), BF16)
    trow = lax.broadcasted_iota(jnp.int32, (sub, HG_DK), 0)
    terms = []
    for r0 in range(0, c, sub):
        qi, ki, bi = q[r0:r0 + sub], k[r0:r0 + sub], b[r0:r0 + sub]
        for s in range(sub):
            causal = trow >= s
            dec = jnp.exp(jnp.where(causal, bi - bi[s:s + 1], 0.0))
            terms.append(jnp.where(causal, qi * ki[s:s + 1] * dec, 0.0).astype(BF16))
    a = _dot(jnp.concatenate(terms, axis=0), ones)
    outs = []
    for r0 in range(0, c, sub):
        oi = o[r0:r0 + sub]
        if r0 > 0:
            bs = b[r0 - 1:r0]
            qp = (q[r0:r0 + sub] * jnp.exp(b[r0:r0 + sub] - bs)).astype(BF16)
            kp = (k[:r0] * jnp.exp(bs - b[:r0])).astype(BF16)
            oi = oi + _dot(_dot_nt(qp, kp).astype(BF16), v[:r0].astype(BF16))
        for s in range(sub):
            oi = oi + a[(r0 + s) * sub:(r0 + s + 1) * sub] * v[r0 + s:r0 + s + 1]
        outs.append(oi)
    return jnp.concatenate(outs, axis=0) if len(outs) > 1 else outs[0]


def _hg_out(o, ng, gs):
    return (_rms(o, ng) * gs).astype(BF16)


def _hg_diag_terms(q, k, b, sub):
    trow = lax.broadcasted_iota(jnp.int32, (sub, HG_DK), 0)
    terms = []
    for r0 in range(0, q.shape[0], sub):
        qi, ki, bi = q[r0:r0 + sub], k[r0:r0 + sub], b[r0:r0 + sub]
        for s in range(sub):
            causal = trow >= s
            dec = jnp.exp(jnp.where(causal, bi - bi[s:s + 1], 0.0))
            terms.append(jnp.where(causal, qi * ki[s:s + 1] * dec, 0.0).astype(BF16))
    return jnp.concatenate(terms, axis=0)


def _hg_offdiag(q, k, v, b, sub):
    c = q.shape[0]
    qps, kps, vps, spans = [], [], [], []
    off = 0
    for r0 in range(sub, c, sub):
        bs = b[r0 - 1:r0]
        qps.append(q[r0:r0 + sub] * jnp.exp(b[r0:r0 + sub] - bs))
        kps.append(k[:r0] * jnp.exp(bs - b[:r0]))
        vps.append(v[:r0])
        spans.append((off, off + r0))
        off += r0
    a = _dot_nt(jnp.concatenate(qps, axis=0).astype(BF16), jnp.concatenate(kps, axis=0).astype(BF16))
    row_blk = lax.broadcasted_iota(jnp.int32, a.shape, 0) // sub
    col = lax.broadcasted_iota(jnp.int32, a.shape, 1)
    keep = jnp.zeros(a.shape, jnp.bool_)
    for i, (lo, hi) in enumerate(spans):
        keep = keep | ((row_blk == i) & (col >= lo) & (col < hi))
    return _dot(jnp.where(keep, a, 0.0).astype(BF16), jnp.concatenate(vps, axis=0).astype(BF16))


def _hg_scan_prompt_kernel(q_ref, f_ref, k_ref, v_ref, gs_ref, ng_ref, o_ref, s_ref,
                           st_ref, b_ref, t_ref, a_ref, i_ref):
    c = pl.program_id(1)
    rows = q_ref.shape[1]
    n_chunk = rows // HG_CHUNK
    tr = HG_CHUNK * HG_SUB
    ones = jnp.ones((HG_DK, HG_DV), BF16)

    @pl.when(c == 0)
    def _():
        st_ref[...] = jnp.zeros(st_ref.shape, F32)

    def heads(it, carry):
        for hh in range(HG_UNROLL_PROMPT):
            h = it * HG_UNROLL_PROMPT + hh
            for ci in range(n_chunk):
                u = hh * n_chunk + ci
                rs = pl.ds(ci * HG_CHUNK, HG_CHUNK)
                b = _cumsum_rows(f_ref[h, rs, :])
                b_ref[u] = b
                t_ref[u * tr:(u + 1) * tr, :] = _hg_diag_terms(q_ref[h, rs, :], k_ref[h, rs, :], b, HG_SUB)
        a_ref[...] = _dot(t_ref[...], ones)
        for hh in range(HG_UNROLL_PROMPT):
            h = it * HG_UNROLL_PROMPT + hh
            for ci in range(n_chunk):
                u = hh * n_chunk + ci
                rs = pl.ds(ci * HG_CHUNK, HG_CHUNK)
                q, k, v, b = q_ref[h, rs, :], k_ref[h, rs, :], v_ref[h, rs, :], b_ref[u]
                off = _hg_offdiag(q, k, v, b, HG_SUB)
                for r0 in range(0, HG_CHUNK, HG_SUB):
                    oi = off[r0 - HG_SUB:r0] if r0 > 0 else jnp.zeros((HG_SUB, HG_DV), F32)
                    for s in range(HG_SUB):
                        a0 = u * tr + (r0 + s) * HG_SUB
                        oi = oi + a_ref[a0:a0 + HG_SUB, :] * v[r0 + s:r0 + s + 1]
                    i_ref[u, r0:r0 + HG_SUB, :] = oi
        for ci in range(n_chunk):
            rs = pl.ds(ci * HG_CHUNK, HG_CHUNK)
            for hh in range(HG_UNROLL_PROMPT):
                h = it * HG_UNROLL_PROMPT + hh
                u = hh * n_chunk + ci
                q, k, v, b = q_ref[h, rs, :], k_ref[h, rs, :], v_ref[h, rs, :], b_ref[u]
                st = st_ref[h]
                o = i_ref[u] + _dot_nt((q * jnp.exp(b)).astype(BF16), st.astype(BF16))
                o_ref[h, rs, :] = _hg_out(o, ng_ref[...], gs_ref[h, rs, :])
                bl = b[HG_CHUNK - 1:HG_CHUNK]
                kd = (k * jnp.exp(bl - b)).astype(BF16)
                st_ref[h] = st * jnp.exp(bl) + lax.dot_general(v.astype(BF16), kd, (((0,), (0,)), ((), ())),
                                                               preferred_element_type=F32)

        @pl.when(c == pl.num_programs(1) - 1)
        def _():
            for hh in range(HG_UNROLL_PROMPT):
                h = it * HG_UNROLL_PROMPT + hh
                s_ref[h] = st_ref[h].T
        return carry

    lax.fori_loop(0, HG_HEADS // HG_UNROLL_PROMPT, heads, 0)


def _hg_scan_prompt(q, f, k, v, gs, ng, batch, t_seq):
    m = q.shape[1]
    rows = 2 * HG_CHUNK
    nc = t_seq // rows
    units = HG_UNROLL_PROMPT * (rows // HG_CHUNK)
    blk = pl.BlockSpec((HG_HEADS, rows, HG_DK), lambda b, c: (0, b * nc + c, 0))
    return pl.pallas_call(
        _hg_scan_prompt_kernel,
        grid=(batch, nc),
        in_specs=[blk] * 5 + [pl.BlockSpec((1, HG_DV), lambda b, c: (0, 0))],
        out_specs=[blk, pl.BlockSpec((None, HG_HEADS, HG_DK, HG_DV), lambda b, c: (b, 0, 0, 0))],
        out_shape=[jax.ShapeDtypeStruct((HG_HEADS, m, HG_DV), BF16),
                   jax.ShapeDtypeStruct((batch, HG_HEADS, HG_DK, HG_DV), F32)],
        scratch_shapes=[pltpu.VMEM((HG_HEADS, HG_DV, HG_DK), F32),
                        pltpu.VMEM((units, HG_CHUNK, HG_DK), F32),
                        pltpu.VMEM((units * HG_CHUNK * HG_SUB, HG_DK), BF16),
                        pltpu.VMEM((units * HG_CHUNK * HG_SUB, HG_DV), F32),
                        pltpu.VMEM((units, HG_CHUNK, HG_DV), F32)],
        compiler_params=_cparams("parallel", "arbitrary"),
        name="hgrn_scan_prompt",
    )(q, f, k, v, gs, ng)


def _hg_step_sample_kernel(q_ref, f_ref, k_ref, v_ref, gs_ref, ng_ref, s0_ref, o_ref, s1_ref, *, t_new):
    n_seq = s0_ref.shape[0]

    def one(sq, h):
        rs = pl.ds(pl.multiple_of(sq * t_new, t_new), t_new)
        q, k, v = q_ref[h, rs, :], k_ref[h, rs, :], v_ref[h, rs, :]
        b = _cumsum_rows(f_ref[h, rs, :])
        s0 = s0_ref[sq, h]
        o = _dot((q * jnp.exp(b)).astype(BF16), s0.astype(BF16))
        o = _hg_intra(q, k, v, b, o, t_new)
        o_ref[h, rs, :] = _hg_out(o, ng_ref[...], gs_ref[h, rs, :])
        bl = b[t_new - 1:t_new]
        kd = k * jnp.exp(bl - b)
        ext = jnp.concatenate([kd, jnp.broadcast_to(jnp.exp(bl), (t_new, HG_DK)),
                               jnp.zeros((HG_DK - 2 * t_new, HG_DK), F32)], axis=0).T
        s1_ref[sq, h] = s0 * ext[:, t_new:t_new + 1] + _dot(ext[:, :t_new].astype(BF16), v.astype(BF16))

    def body(idx, carry):
        per_seq = HG_HEADS // HG_UNROLL_SAMPLE
        for hh in range(HG_UNROLL_SAMPLE):
            one(idx // per_seq, (idx % per_seq) * HG_UNROLL_SAMPLE + hh)
        return carry

    lax.fori_loop(0, n_seq * (HG_HEADS // HG_UNROLL_SAMPLE), body, 0)


def _hg_step_sample(q, f, k, v, gs, ng, s0_all, layer, t_new):
    m = q.shape[1]
    dec_b = s0_all.shape[1]
    sb = min(8, dec_b)
    blk = pl.BlockSpec((HG_HEADS, sb * t_new, HG_DK), lambda i: (0, i, 0))
    st_in = pl.BlockSpec((None, sb, HG_HEADS, HG_DK, HG_DV), lambda i: (layer, i, 0, 0, 0))
    st_out = pl.BlockSpec((sb, HG_HEADS, HG_DK, HG_DV), lambda i: (i, 0, 0, 0))
    return pl.pallas_call(
        functools.partial(_hg_step_sample_kernel, t_new=t_new),
        grid=(dec_b // sb,),
        in_specs=[blk] * 5 + [pl.BlockSpec((1, HG_DV), lambda i: (0, 0)), st_in],
        out_specs=[blk, st_out],
        out_shape=[jax.ShapeDtypeStruct((HG_HEADS, m, HG_DV), BF16), jax.ShapeDtypeStruct(s0_all.shape[1:], F32)],
        compiler_params=_cparams("parallel"),
        name="hgrn_step_sample",
    )(q, f, k, v, gs, ng, s0_all)


def _hg_outproj_kernel(x_ref, o_ref, w_ref, y_ref):
    o = jnp.concatenate([o_ref[h] for h in range(HG_HEADS)], axis=1)
    y_ref[...] = x_ref[...] + _dot(o, w_ref[...])


def _hg_outproj(x, o, w):
    m, d = x.shape
    tm = min(ROW_TILE, m)
    row = pl.BlockSpec((tm, d), lambda i: (i, 0))
    return pl.pallas_call(
        _hg_outproj_kernel,
        grid=(m // tm,),
        in_specs=[row, pl.BlockSpec((HG_HEADS, tm, HG_DV), lambda i: (0, i, 0)), _full(w.shape)],
        out_specs=row,
        out_shape=jax.ShapeDtypeStruct((m, d), F32),
        compiler_params=_cparams("parallel"),
        name="hgrn_outproj",
    )(x, o, w)


def _feature_major_view(a):
    lead = a.shape[:-4]
    t, c, g, d = a.shape[-4:]
    n = len(lead)
    return jnp.transpose(a, tuple(range(n)) + (n + 1, n + 2, n + 3, n)).reshape(lead + (c * g * d, t))


def _token_major_view(a, c):
    lead = a.shape[:-2]
    t = a.shape[-1]
    n = len(lead)
    a = a.reshape(lead + (c, KV_HEADS, HEAD_DIM, t))
    return jnp.transpose(a, tuple(range(n)) + (n + 3, n, n + 1, n + 2))


def kernel(x_prompt, x_sample, cache_nsa_kv, state_nsa_win, state_hgrn, page_table, p_prompt, p_sample, norm_mix, norm_mlp, norm_ple, norm_final, nsa_w_in, nsa_cmp_pe, nsa_cmp_w1, nsa_cmp_w2, nsa_w_out, hg_w_in, hg_lb_logits, hg_norm, hg_w_out, mlp_w1, mlp_w2, ple_w_proj, ple_w_gate):
    batch, t_p, d = x_prompt.shape
    dec_b, t_s, _ = x_sample.shape
    depth = p_prompt.shape[0]
    page = cache_nsa_kv.shape[2]
    past = page_table.shape[1] * page

    xp = x_prompt.reshape(batch * t_p, d)
    xs = x_sample.reshape(dec_b * t_s, d)
    pp = p_prompt.reshape(depth, batch * t_p, -1)
    ps = p_sample.reshape(depth, dec_b * t_s, -1)
    row = lambda a: a.reshape(1, -1)
    cache_t = _feature_major_view(cache_nsa_kv)
    win_t = _feature_major_view(state_nsa_win)
    wl = min(WINDOW, t_p)

    tabs_p = _rope_tables(t_p, 0, t_p)
    tabs_s = _rope_tables(min(ROW_TILE, dec_b * t_s), past, t_s)

    kv_p, kv_s, win_p, win_s, st_p, st_s = [], [], [], [], [], []
    for i in range(depth):
        g_mix = row(norm_mix[i])
        if i % 2 == 0:
            a = i // 2
            n_in = nsa_w_in.shape[2]
            w_t = jnp.pad(jnp.transpose(nsa_w_in[a]).astype(BF16),
                          ((0, Q_DIM + 6 * KV_DIM + LANES - n_in), (0, 0)))
            pe2 = jnp.tile(nsa_cmp_pe[a], (1, 1, LANES // HEAD_DIM))
            cw1 = nsa_cmp_w1[a].reshape(2, CMP_BLOCK * HEAD_DIM, -1).astype(BF16)
            cw2t = jnp.swapaxes(nsa_cmp_w2[a], 1, 2).astype(BF16)
            w_out = nsa_w_out[a].astype(BF16)

            qt, gatest, cmp_rows, rows_t, wrows_t, ksa, vsa, kwa, vwa = _nsa_inproj_prompt(
                xp, g_mix, tabs_p, w_t, batch, t_p)
            kc, vct = _compress_prompt(cmp_rows, pe2, cw1, cw2t, batch, t_p)
            o = _nsa_attn_prompt(qt, gatest, kc, vct, ksa, vsa, kwa, vwa, batch, t_p)
            xp = _outproj(xp, o, w_out)
            kv_p.append(rows_t)
            win_p.append(wrows_t[:, :, t_p - wl:])

            q, gates, rows, wrows = _nsa_inproj_sample(xs, g_mix, tabs_s, w_t)
            (kc, vc), n_cmp = _compress_sample(page_table, cache_t, a, rows, pe2, cw1, cw2t, t_s)
            o, new_win_t = _nsa_attn_sample(page_table, cache_t, win_t, a, kc, vc, q, rows, wrows, gates, t_s, n_cmp)
            xs = _outproj(xs, o, w_out)
            kv_s.append(rows.reshape(dec_b, t_s, 4, KV_HEADS, HEAD_DIM))
            win_s.append(new_win_t)
        else:
            r = i // 2
            w_in = hg_w_in[r].astype(BF16)
            w_out = hg_w_out[r].astype(BF16)
            ng = row(hg_norm[r])
            q, f, k, v, gs = _hg_inproj(xp, g_mix, hg_lb_logits, w_in, r)
            o, s_new = _hg_scan_prompt(q, f, k, v, gs, ng, batch, t_p)
            xp = _hg_outproj(xp, o, w_out)
            st_p.append(s_new)
            q, f, k, v, gs = _hg_inproj(xs, g_mix, hg_lb_logits, w_in, r)
            o, s_new = _hg_step_sample(q, f, k, v, gs, ng, state_hgrn, r, t_s)
            xs = _hg_outproj(xs, o, w_out)
            st_s.append(s_new)

        w1 = mlp_w1[i].astype(BF16)
        w2 = mlp_w2[i].astype(BF16)
        wgate = ple_w_gate[i].astype(BF16)
        wproj = ple_w_proj[i].astype(BF16)
        last = i == depth - 1
        xp = _mlp(xp, row(norm_mlp[i]), w1, w2)
        xs = _mlp(xs, row(norm_mlp[i]), w1, w2)
        xp = _ple(xp, pp, i, row(norm_ple[i]), wgate, wproj, row(norm_final), last)
        xs = _ple(xs, ps, i, row(norm_ple[i]), wgate, wproj, row(norm_final), last)

    return (xp.reshape(batch, t_p, d), xs.reshape(dec_b, t_s, d),
            _token_major_view(jnp.stack(kv_p), 4), jnp.stack(kv_s),
            _token_major_view(jnp.stack(win_p), 2), _token_major_view(jnp.stack(win_s), 2),
            jnp.stack(st_p), jnp.stack(st_s))
```

```python
import functools

import jax
import jax.numpy as jnp
from jax import lax
from jax.experimental import pallas as pl
from jax.experimental.pallas import tpu as pltpu

F32 = jnp.float32
BF16 = jnp.bfloat16

RMS_EPS = 1e-6
ROPE_THETA = 10000.0
N_HEADS = 16
HEAD_DIM = 64
KV_HEADS = 4
Q_PER_KV = N_HEADS // KV_HEADS
Q_DIM = N_HEADS * HEAD_DIM
KV_DIM = KV_HEADS * HEAD_DIM
N_BRANCH = 3
CMP_BLOCK = 32
SEL_BLOCK = 64
TOP_N = 16
WINDOW = 512
FORCED_CUR = 3e4
FORCED_FIRST = 2e4
HG_HEADS = 8
HG_DK = 128
HG_DV = 128
HG_CHUNK = 64
HG_SUB = 16
HG_UNROLL_PROMPT = 4
HG_UNROLL_SAMPLE = 8
NEG = -1e30
LOG2_E = 1.4426950408889634

LANES = 128
SUBLANES = 8
ROW_TILE = 512
ATT_TILE = 256
V_AUG_ROWS = HEAD_DIM + 2 * SUBLANES
VMEM_LIMIT = 56 * 1024 * 1024


def _cparams(*sem):
    return pltpu.CompilerParams(dimension_semantics=sem, vmem_limit_bytes=VMEM_LIMIT)


def _full(shape):
    return pl.BlockSpec(shape, lambda *_: (0,) * len(shape))


def _rms(x, g):
    return x * lax.rsqrt(jnp.mean(x * x, axis=-1, keepdims=True) + RMS_EPS) * g


def _dot(a, b):
    return jnp.dot(a, b, preferred_element_type=F32)


def _dot_nt(a, b):
    return lax.dot_general(a, b, (((1,), (1,)), ((), ())), preferred_element_type=F32)


def _rope_table_kernel(inv_ref, invc_ref, cos_ref, sin_ref, cost_ref, sint_ref, *, pos0, period):
    i = pl.program_id(0)
    tm = cos_ref.shape[0]
    half = HEAD_DIM // 2
    row = i * tm + lax.broadcasted_iota(jnp.int32, (tm, LANES), 0)
    ang = (pos0 + row % period).astype(F32) * inv_ref[...]
    lane = lax.broadcasted_iota(jnp.int32, (tm, LANES), 1)
    cos_ref[...] = jnp.cos(ang)
    s = jnp.sin(ang)
    sin_ref[...] = jnp.where(lane % HEAD_DIM < half, -s, s)
    col = i * tm + lax.broadcasted_iota(jnp.int32, (HEAD_DIM, tm), 1)
    ang_t = (pos0 + col % period).astype(F32) * jnp.concatenate([invc_ref[...]] * (tm // LANES), axis=1)
    feat = lax.broadcasted_iota(jnp.int32, (HEAD_DIM, tm), 0)
    cost_ref[...] = jnp.cos(ang_t)
    st = jnp.sin(ang_t)
    sint_ref[...] = jnp.where(feat < half, -st, st)


def _rope_tables(rows, pos0, period):
    half = HEAD_DIM // 2
    inv = ROPE_THETA ** (-jnp.arange(half, dtype=F32) / half)
    inv_row = jnp.tile(inv, LANES // half)[None, :]
    inv_col = jnp.broadcast_to(jnp.tile(inv, HEAD_DIM // half)[:, None], (HEAD_DIM, LANES))
    kern = functools.partial(_rope_table_kernel, pos0=pos0, period=period)
    tm = min(ROW_TILE, rows)
    tok = pl.BlockSpec((tm, LANES), lambda i: (i, 0))
    feat = pl.BlockSpec((HEAD_DIM, tm), lambda i: (0, i))
    return pl.pallas_call(
        kern,
        grid=(rows // tm,),
        in_specs=[_full((1, LANES)), _full((HEAD_DIM, LANES))],
        out_specs=[tok, tok, feat, feat],
        out_shape=[jax.ShapeDtypeStruct((rows, LANES), F32)] * 2 + [jax.ShapeDtypeStruct((HEAD_DIM, rows), F32)] * 2,
        compiler_params=_cparams("parallel"),
        name="rope_tables",
    )(inv_row, inv_col)


def _rope_tok(y, cos, sin):
    lane = lax.broadcasted_iota(jnp.int32, y.shape, 1)
    first = lane % HEAD_DIM < HEAD_DIM // 2
    sw = jnp.where(first, pltpu.roll(y, LANES - HEAD_DIM // 2, 1), pltpu.roll(y, HEAD_DIM // 2, 1))
    return y * cos + sw * sin


def _rope_feat(y, cost, sint):
    return y * cost + pltpu.roll(y, HEAD_DIM // 2, 0) * sint


def _nsa_inproj_prompt_kernel(x_ref, g_ref, cos_ref, sin_ref, cost_ref, sint_ref, w_ref,
                              qt_ref, gatest_ref, cmp_ref, rowst_ref, wrowst_ref,
                              ksa_ref, vsa_ref, kwa_ref, vwa_ref):
    tm = x_ref.shape[0]
    tk = ksa_ref.shape[2]
    i = pl.program_id(1)
    xn = _rms(x_ref[...], g_ref[...]).astype(BF16)
    cos, sin = cos_ref[...], sin_ref[...]
    cost, sint = cost_ref[...], sint_ref[...]
    scale = HEAD_DIM ** -0.5 * LOG2_E

    qt = _dot_nt(w_ref[0:Q_DIM, :], xn)
    for h in range(N_HEADS):
        rs = slice(h * HEAD_DIM, (h + 1) * HEAD_DIM)
        qt_ref[rs, :] = (_rope_feat(qt[rs], cost, sint) * scale).astype(BF16)
    g0 = Q_DIM + 6 * KV_DIM
    gatest_ref[...] = jax.nn.sigmoid(_dot_nt(w_ref[g0:g0 + LANES, :], xn))

    ycmp = _dot_nt(xn, w_ref[Q_DIM:Q_DIM + 2 * KV_DIM, :])
    for c in range(2 * KV_DIM // LANES):
        cs = slice(c * LANES, (c + 1) * LANES)
        cmp_ref[:, cs] = _rope_tok(ycmp[:, cs], cos, sin) if c * LANES < KV_DIM else ycmp[:, cs]

    lane = lax.broadcasted_iota(jnp.int32, (tm, LANES), 1)
    lo = lane < HEAD_DIM
    pos = i * tm + lax.broadcasted_iota(jnp.int32, (tm, LANES), 0)
    sel_bias = jnp.where(lane - HEAD_DIM == pos // SEL_BLOCK, NEG, 0.0)
    for part, ref, fill in ((2, ksa_ref, sel_bias), (4, kwa_ref, 0.0)):
        yk = _dot_nt(xn, w_ref[Q_DIM + part * KV_DIM:Q_DIM + (part + 1) * KV_DIM, :])
        for c in range(KV_DIM // LANES):
            y = _rope_tok(yk[:, c * LANES:(c + 1) * LANES], cos, sin)
            pair = (jnp.where(lo, y, fill).astype(BF16), jnp.where(lo, pltpu.roll(y, HEAD_DIM, 1), fill).astype(BF16))
            for gg in range(2):
                for j in range(tm // tk):
                    ref[2 * c + gg, j] = pair[gg][j * tk:(j + 1) * tk, :]

    yt = _dot_nt(w_ref[Q_DIM:Q_DIM + 6 * KV_DIM, :], xn)
    ones = jnp.ones((V_AUG_ROWS - HEAD_DIM, tm), F32)
    for part in range(6):
        for g in range(KV_HEADS):
            r0 = part * KV_DIM + g * HEAD_DIM
            y = yt[r0:r0 + HEAD_DIM, :]
            if part % 2 == 0:
                y = _rope_feat(y, cost, sint)
            if part < 4:
                rowst_ref[r0:r0 + HEAD_DIM, :] = y
            else:
                wrowst_ref[r0 - 4 * KV_DIM:r0 - 4 * KV_DIM + HEAD_DIM, :] = y
            if part in (3, 5):
                ref = vsa_ref if part == 3 else vwa_ref
                ya = jnp.concatenate([y, ones], axis=0).astype(BF16)
                for j in range(tm // tk):
                    ref[g, j] = ya[:, j * tk:(j + 1) * tk]


def _nsa_inproj_prompt(x, g, tabs, w, batch, t_seq):
    m, d = x.shape
    tm = min(ROW_TILE, t_seq)
    tk = min(ATT_TILE, t_seq)
    nt = t_seq // tm
    cos, sin, cost, sint = tabs
    row = lambda cols: pl.BlockSpec((tm, cols), lambda b, i: (b * nt + i, 0))
    tok_tab = pl.BlockSpec((tm, LANES), lambda b, i: (i, 0))
    feat_tab = pl.BlockSpec((HEAD_DIM, tm), lambda b, i: (0, i))
    featmaj = lambda rows: pl.BlockSpec((None, rows, tm), lambda b, i: (b, 0, i))
    ktiles = pl.BlockSpec((None, KV_HEADS, tm // tk, tk, LANES), lambda b, i: (b, 0, i, 0, 0))
    vtiles = pl.BlockSpec((None, KV_HEADS, tm // tk, V_AUG_ROWS, tk), lambda b, i: (b, 0, i, 0, 0))
    kshape = jax.ShapeDtypeStruct((batch, KV_HEADS, t_seq // tk, tk, LANES), BF16)
    vshape = jax.ShapeDtypeStruct((batch, KV_HEADS, t_seq // tk, V_AUG_ROWS, tk), BF16)
    return pl.pallas_call(
        _nsa_inproj_prompt_kernel,
        grid=(batch, nt),
        in_specs=[row(d), _full((1, d)), tok_tab, tok_tab, feat_tab, feat_tab, _full(w.shape)],
        out_specs=[featmaj(Q_DIM), featmaj(LANES), row(2 * KV_DIM), featmaj(4 * KV_DIM), featmaj(2 * KV_DIM),
                   ktiles, vtiles, ktiles, vtiles],
        out_shape=[jax.ShapeDtypeStruct((batch, Q_DIM, t_seq), BF16), jax.ShapeDtypeStruct((batch, LANES, t_seq), F32),
                   jax.ShapeDtypeStruct((m, 2 * KV_DIM), F32),
                   jax.ShapeDtypeStruct((batch, 4 * KV_DIM, t_seq), F32),
                   jax.ShapeDtypeStruct((batch, 2 * KV_DIM, t_seq), F32), kshape, vshape, kshape, vshape],
        compiler_params=_cparams("parallel", "parallel"),
        name="nsa_inproj_prompt",
    )(x, g, cos, sin, cost, sint, w)


def _nsa_inproj_sample_kernel(x_ref, g_ref, cos_ref, sin_ref, w_ref, q_ref, gates_ref, rows_ref, wrows_ref):
    xn = _rms(x_ref[...], g_ref[...]).astype(BF16)
    cos, sin = cos_ref[...], sin_ref[...]
    scale = HEAD_DIM ** -0.5
    yq = _dot_nt(xn, w_ref[0:Q_DIM, :])
    for c in range(Q_DIM // LANES):
        cs = slice(c * LANES, (c + 1) * LANES)
        q_ref[:, cs] = (_rope_tok(yq[:, cs], cos, sin) * scale).astype(BF16)
    g0 = Q_DIM + 6 * KV_DIM
    gates_ref[...] = jax.nn.sigmoid(_dot_nt(xn, w_ref[g0:g0 + LANES, :]))
    ykv = _dot_nt(xn, w_ref[Q_DIM:g0, :])
    for c in range(6 * KV_DIM // LANES):
        cs = slice(c * LANES, (c + 1) * LANES)
        part = c * LANES // KV_DIM
        y = _rope_tok(ykv[:, cs], cos, sin) if part % 2 == 0 else ykv[:, cs]
        if part < 4:
            rows_ref[:, cs] = y
        else:
            wrows_ref[:, c * LANES - 4 * KV_DIM:(c + 1) * LANES - 4 * KV_DIM] = y


def _nsa_inproj_sample(x, g, tabs, w):
    m, d = x.shape
    tm = min(ROW_TILE, m)
    cos, sin = tabs[0], tabs[1]
    n_tab = cos.shape[0] // tm
    row = lambda cols: pl.BlockSpec((tm, cols), lambda i: (i, 0))
    tab = pl.BlockSpec((tm, LANES), lambda i: (i % n_tab, 0))
    return pl.pallas_call(
        _nsa_inproj_sample_kernel,
        grid=(m // tm,),
        in_specs=[row(d), _full((1, d)), tab, tab, _full(w.shape)],
        out_specs=[row(d), row(LANES), row(4 * KV_DIM), row(2 * KV_DIM)],
        out_shape=[jax.ShapeDtypeStruct((m, d), BF16), jax.ShapeDtypeStruct((m, LANES), F32),
                   jax.ShapeDtypeStruct((m, 4 * KV_DIM), F32), jax.ShapeDtypeStruct((m, 2 * KV_DIM), F32)],
        compiler_params=_cparams("parallel"),
        name="nsa_inproj_sample",
    )(x, g, cos, sin, w)


def _compress_rows(load, pe_ref, w1_ref, w2t_ref, kv, n, transposed=False):
    lane = lax.broadcasted_iota(jnp.int32, (n, LANES), 1)
    lo = lane < HEAD_DIM
    acc = jnp.zeros((KV_HEADS * n, w1_ref.shape[-1]), F32)
    lq = 4
    for l0 in range(0, CMP_BLOCK, lq):
        halves = [[], []]
        for l in range(l0, l0 + lq, 2):
            for hf in range(2):
                a = load(l, hf) + pe_ref[kv, l:l + 1, :]
                b = load(l + 1, hf) + pe_ref[kv, l + 1:l + 2, :]
                even = jnp.where(lo, a, pltpu.roll(b, HEAD_DIM, 1))
                odd = jnp.where(lo, pltpu.roll(a, HEAD_DIM, 1), b)
                halves[hf].append((even, odd))
        groups = []
        for g in range(KV_HEADS):
            hf, par = divmod(g, 2)
            groups.append(jnp.concatenate([pc[par] for pc in halves[hf]], axis=1))
        xg = jnp.concatenate(groups, axis=0).astype(BF16)
        acc = acc + _dot(xg, w1_ref[kv, l0 * HEAD_DIM:(l0 + lq) * HEAD_DIM, :])
    h = jax.nn.gelu(acc).astype(BF16)
    return _dot_nt(w2t_ref[kv], h) if transposed else _dot_nt(h, w2t_ref[kv])


def _compress_prompt_kernel(s0_ref, s1_ref, s2_ref, s3_ref, pe_ref, w1_ref, w2t_ref, kc_ref, vct_ref):
    n = kc_ref.shape[0]
    hn = n // 2
    srcs = ((s0_ref, s1_ref), (s2_ref, s3_ref))

    def loader(kv):
        def load(l, hf):
            ev = srcs[kv][hf][pl.ds(l, hn, stride=2 * CMP_BLOCK), :]
            od = srcs[kv][hf][pl.ds(CMP_BLOCK + l, hn, stride=2 * CMP_BLOCK), :]
            return jnp.concatenate([ev, od], axis=0)
        return load

    y = _compress_rows(loader(0), pe_ref, w1_ref, w2t_ref, 0, n)
    yt = _compress_rows(loader(1), pe_ref, w1_ref, w2t_ref, 1, n, transposed=True)
    for g in range(KV_HEADS):
        kc_ref[:, g * HEAD_DIM:(g + 1) * HEAD_DIM] = y[g * n:(g + 1) * n].astype(BF16)
        vct_ref[g * HEAD_DIM:(g + 1) * HEAD_DIM, :] = yt[:, g * n:(g + 1) * n].astype(BF16)


def _compress_prompt(cmp_rows, pe2, w1, w2t, batch, t_seq):
    n = t_seq // CMP_BLOCK
    chunk = lambda c: pl.BlockSpec((t_seq, LANES), lambda b: (b, c))
    return pl.pallas_call(
        _compress_prompt_kernel,
        grid=(batch,),
        in_specs=[chunk(0), chunk(1), chunk(2), chunk(3), _full(pe2.shape), _full(w1.shape), _full(w2t.shape)],
        out_specs=[pl.BlockSpec((None, n, KV_DIM), lambda b: (b, 0, 0)),
                   pl.BlockSpec((None, KV_DIM, n), lambda b: (b, 0, 0))],
        out_shape=[jax.ShapeDtypeStruct((batch, n, KV_DIM), BF16), jax.ShapeDtypeStruct((batch, KV_DIM, n), BF16)],
        compiler_params=_cparams("parallel"),
        name="nsa_compress_prompt",
    )(cmp_rows, cmp_rows, cmp_rows, cmp_rows, pe2, w1, w2t)


def _topk_not_selected(imp_t, n_live, rank_ref):
    n_sel, nq = imp_t.shape
    slabs = [imp_t[r0:r0 + SUBLANES] for r0 in range(0, n_sel, SUBLANES)]
    sub = lax.broadcasted_iota(jnp.int32, (SUBLANES, nq), 0)
    rank_ref[...] = jnp.zeros((n_sel, nq), F32)
    for c0 in range(0, n_sel, SUBLANES):
        @pl.when(c0 < n_live)
        def _():
            for v, slab in enumerate(slabs):
                r0 = v * SUBLANES
                cnt = jnp.zeros((SUBLANES, nq), F32)
                for bp in range(c0, c0 + SUBLANES):
                    other = imp_t[bp:bp + 1, :]
                    if bp < r0:
                        ahead = other >= slab
                    elif bp >= r0 + SUBLANES - 1:
                        ahead = other > slab
                    else:
                        ahead = (other > slab) | ((other == slab) & (bp - r0 < sub))
                    cnt = cnt + jnp.where(ahead, 1.0, 0.0)
                rank_ref[r0:r0 + SUBLANES, :] += cnt
    return jnp.where(rank_ref[...] < TOP_N, 0.0, 1.0)


def _nsa_attn_prompt_kernel(qt_ref, gatest_ref, kc_ref, vct_ref, ksa_ref, vsa_ref, kwa_ref, vwa_ref, o_ref,
                            acc_ref, m_ref, accw_ref, mw_ref, sa_ref, sb_ref, wb_ref, cb_ref, cv_ref, rank_ref):
    tq = qt_ref.shape[1]
    tk = ksa_ref.shape[2]
    n_cmp = kc_ref.shape[0]
    n_sel = n_cmp // 2
    rq = Q_PER_KV * tq
    i = pl.program_id(1)
    t0 = i * tq
    t_col = t0 + lax.broadcasted_iota(jnp.int32, (1, rq), 1) % tq
    key_row = lax.broadcasted_iota(jnp.int32, (tk, rq), 0)
    n_wt = -(-(WINDOW - 1) // tk) + 1
    kt_diag = t0 // tk
    wkt = [kt_diag - j for j in range(n_wt)]

    for j, kt in enumerate(wkt):
        kpos = kt * tk + key_row
        wb_ref[j] = jnp.where((kpos >= 0) & (kpos <= t_col) & (kpos > t_col - WINDOW), 0.0, NEG)
    nrow = lax.broadcasted_iota(jnp.int32, (n_cmp, rq), 0)
    cblk = jnp.where(nrow < n_sel, 2 * nrow, 2 * (nrow - n_sel) + 1)
    cvis = (cblk + 1) * CMP_BLOCK - 1 <= t_col
    cb_ref[...] = jnp.where(cvis, 0.0, NEG)
    cv_ref[...] = jnp.where(cvis, 1.0, 0.0)

    def online_update(s_ref, vt, bias, m_ref, acc_ref):
        s = s_ref[...]
        if bias is not None:
            s = s + bias
        m_old = m_ref[...]
        m_new = jnp.maximum(m_old, jnp.max(s, axis=0, keepdims=True))
        p = jnp.exp2(s - m_new)
        acc_ref[...] = acc_ref[...] * jnp.exp2(m_old - m_new) + _dot(vt, p.astype(BF16))
        m_ref[...] = m_new

    def normalized(acc):
        return acc[:HEAD_DIM] * (1.0 / jnp.maximum(acc[HEAD_DIM:HEAD_DIM + 1], 1e-30))

    for g in range(KV_HEADS):
        qg = jnp.concatenate(
            [qt_ref[(Q_PER_KV * g + r) * HEAD_DIM:(Q_PER_KV * g + r + 1) * HEAD_DIM, :] for r in range(Q_PER_KV)],
            axis=1)

        sc = _dot(kc_ref[:, g * HEAD_DIM:(g + 1) * HEAD_DIM], qg) + cb_ref[...]
        e = jnp.exp2(sc - jnp.max(sc, axis=0, keepdims=True)) * cv_ref[...]
        pc = e * (1.0 / jnp.maximum(jnp.sum(e, axis=0, keepdims=True), 1e-30))
        o_c = _dot(vct_ref[g * HEAD_DIM:(g + 1) * HEAD_DIM, :], pc.astype(BF16))

        pq = pc[:, 0:tq]
        for r in range(1, Q_PER_KV):
            pq = pq + pc[:, r * tq:(r + 1) * tq]
        imp = pq[:n_sel] + pq[n_sel:]
        blk = lax.broadcasted_iota(jnp.int32, (n_sel, tq), 0)
        cur = (t0 + lax.broadcasted_iota(jnp.int32, (n_sel, tq), 1)) // SEL_BLOCK
        imp = jnp.where(blk > cur, -1.0, imp)
        imp = jnp.where(blk == 0, FORCED_FIRST, imp)
        imp = jnp.where(blk == cur, FORCED_CUR, imp)
        n_live = (t0 + tq - 1) // SEL_BLOCK + 1
        notsel = _topk_not_selected(imp, n_live, rank_ref)
        if n_sel < HEAD_DIM:
            notsel = jnp.concatenate([notsel, jnp.zeros((HEAD_DIM - n_sel, tq), F32)], axis=0)
        qa = jnp.concatenate([qg, jnp.concatenate([notsel.astype(BF16)] * Q_PER_KV, axis=1)], axis=0)

        for ref in (m_ref, mw_ref):
            ref[...] = jnp.full((1, rq), NEG, F32)
        for ref in (acc_ref, accw_ref):
            ref[...] = jnp.zeros((V_AUG_ROWS, rq), F32)
        slots = [sa_ref, sb_ref]
        qw = jnp.concatenate([qg, jnp.zeros_like(qg)], axis=0)

        slots[0][...] = _dot(kwa_ref[g, wkt[0]], qw)
        for j, kt in enumerate(wkt):
            if j + 1 < n_wt:
                slots[1 - j % 2][...] = _dot(kwa_ref[g, jnp.maximum(wkt[j + 1], 0)], qw)
            else:
                slots[1 - j % 2][...] = _dot(ksa_ref[g, 0], qa)
            online_update(slots[j % 2], vwa_ref[g, jnp.maximum(kt, 0)], wb_ref[j], mw_ref, accw_ref)
        o_w = normalized(accw_ref[...])

        s_even, s_odd = slots[n_wt % 2], slots[1 - n_wt % 2]

        def sel_pair(j, carry):
            s_odd[...] = _dot(ksa_ref[g, 2 * j + 1], qa)
            online_update(s_even, vsa_ref[g, 2 * j], None, m_ref, acc_ref)
            s_even[...] = _dot(ksa_ref[g, 2 * j + 2], qa)
            online_update(s_odd, vsa_ref[g, 2 * j + 1], None, m_ref, acc_ref)
            return carry

        lax.fori_loop(0, kt_diag // 2, sel_pair, 0)

        @pl.when(kt_diag % 2 == 1)
        def _():
            s_odd[...] = _dot(ksa_ref[g, kt_diag], qa)
            online_update(s_even, vsa_ref[g, kt_diag - 1], None, m_ref, acc_ref)
            online_update(s_odd, vsa_ref[g, kt_diag], wb_ref[0], m_ref, acc_ref)

        @pl.when(kt_diag % 2 == 0)
        def _():
            online_update(s_even, vsa_ref[g, kt_diag], wb_ref[0], m_ref, acc_ref)

        o_s = normalized(acc_ref[...])

        outs = []
        for r in range(Q_PER_KV):
            h = Q_PER_KV * g + r
            cs = slice(r * tq, (r + 1) * tq)
            gc = gatest_ref[N_BRANCH * h:N_BRANCH * h + 1, :]
            gs = gatest_ref[N_BRANCH * h + 1:N_BRANCH * h + 2, :]
            gw = gatest_ref[N_BRANCH * h + 2:N_BRANCH * h + 3, :]
            outs.append(gc * o_c[:, cs] + gs * o_s[:, cs] + gw * o_w[:, cs])
        for pr in range(Q_PER_KV // 2):
            c0 = (Q_PER_KV * g + 2 * pr) * HEAD_DIM
            o_ref[:, c0:c0 + LANES] = jnp.concatenate(outs[2 * pr:2 * pr + 2], axis=0).T.astype(BF16)


def _nsa_attn_prompt(qt, gatest, kc, vct, ksa, vsa, kwa, vwa, batch, t_seq):
    d = qt.shape[1]
    tq = min(ATT_TILE, t_seq)
    nq = t_seq // tq
    n_cmp = t_seq // CMP_BLOCK
    tk = ksa.shape[3]
    featmaj = lambda rows: pl.BlockSpec((None, rows, tq), lambda b, i: (b, 0, i))
    seq = lambda shape: pl.BlockSpec((None,) + shape[1:], lambda b, i: (b, 0, 0, 0, 0))
    return pl.pallas_call(
        _nsa_attn_prompt_kernel,
        grid=(batch, nq),
        in_specs=[featmaj(d), featmaj(LANES),
                  pl.BlockSpec((None, n_cmp, KV_DIM), lambda b, i: (b, 0, 0)),
                  pl.BlockSpec((None, KV_DIM, n_cmp), lambda b, i: (b, 0, 0)),
                  seq(ksa.shape), seq(vsa.shape), seq(kwa.shape), seq(vwa.shape)],
        out_specs=pl.BlockSpec((tq, d), lambda b, i: (b * nq + i, 0)),
        out_shape=jax.ShapeDtypeStruct((batch * t_seq, d), BF16),
        scratch_shapes=[pltpu.VMEM((V_AUG_ROWS, Q_PER_KV * tq), F32), pltpu.VMEM((1, Q_PER_KV * tq), F32),
                        pltpu.VMEM((V_AUG_ROWS, Q_PER_KV * tq), F32), pltpu.VMEM((1, Q_PER_KV * tq), F32),
                        pltpu.VMEM((tk, Q_PER_KV * tq), F32), pltpu.VMEM((tk, Q_PER_KV * tq), F32),
                        pltpu.VMEM((-(-(WINDOW - 1) // tk) + 1, tk, Q_PER_KV * tq), F32),
                        pltpu.VMEM((n_cmp, Q_PER_KV * tq), F32), pltpu.VMEM((n_cmp, Q_PER_KV * tq), F32),
                        pltpu.VMEM((n_cmp // 2, tq), F32)],
        compiler_params=_cparams("parallel", "arbitrary"),
        name="nsa_attn_prompt",
    )(qt, gatest, kc, vct, ksa, vsa, kwa, vwa)


def _compress_sample_kernel(pt_ref, *refs, n_pages, page, t_new, n_cmp):
    del pt_ref
    pages = refs[:n_pages]
    new_ref, pe_ref, w1_ref, w2t_ref, kc_ref, vc_ref = refs[n_pages:n_pages + 6]
    bufs = refs[n_pages + 6:]
    n_pad = bufs[0].shape[1]
    n_past = n_pages * page // CMP_BLOCK
    grp = SUBLANES * CMP_BLOCK
    ppg = grp // page
    tt = lax.broadcasted_iota(jnp.int32, (grp, grp), 0)
    perm_t = jnp.where(lax.broadcasted_iota(jnp.int32, (grp, grp), 1)
                       == (tt % CMP_BLOCK) * SUBLANES + tt // CMP_BLOCK, 1.0, 0.0).astype(BF16)
    for c, buf_ref in enumerate(bufs):
        cs = slice(c * LANES, (c + 1) * LANES)
        for pp in range(n_pages // ppg):
            xt = jnp.concatenate([pages[pp * ppg + j][cs, :] for j in range(ppg)], axis=1)
            xp = _dot(xt.astype(BF16), perm_t).T
            for l in range(CMP_BLOCK):
                buf_ref[l, pp * SUBLANES:(pp + 1) * SUBLANES, :] = xp[l * SUBLANES:(l + 1) * SUBLANES, :]
        buf_ref[:, n_past:, :] = jnp.zeros((CMP_BLOCK, n_pad - n_past, LANES), F32)
        for l in range(t_new):
            buf_ref[l, n_past:n_past + 1, :] = new_ref[l:l + 1, cs]
    n_out = kc_ref.shape[0]
    for kv, out_ref in ((0, kc_ref), (1, vc_ref)):
        def load(l, hf, kv=kv):
            return bufs[2 * kv + hf][l]
        y = _compress_rows(load, pe_ref, w1_ref, w2t_ref, kv, n_pad)
        valid = lax.broadcasted_iota(jnp.int32, (n_pad, HEAD_DIM), 0) < n_cmp
        for g in range(KV_HEADS):
            out_ref[0:n_pad, g * HEAD_DIM:(g + 1) * HEAD_DIM] = jnp.where(
                valid, y[g * n_pad:(g + 1) * n_pad], 0.0).astype(BF16)
        out_ref[n_pad:, :] = jnp.zeros((n_out - n_pad, KV_DIM), BF16)


def _page_specs(layer, n_pages, page, feat_block):
    half = 2 * KV_DIM
    return [pl.BlockSpec((None, None, half, page),
                         functools.partial(lambda b, pt, p: (layer, pt[b, p], feat_block, 0), p=p))
            for p in range(n_pages)]


def _compress_sample(page_table, cache_t, layer, rows_new, pe2, w1, w2t, t_new):
    dec_b, n_pages = page_table.shape
    page = cache_t.shape[-1]
    past = n_pages * page
    l_pad = -(-(past + t_new) // SEL_BLOCK) * SEL_BLOCK
    n_cmp = l_pad // CMP_BLOCK
    n_pad = -(-n_cmp // SUBLANES) * SUBLANES
    n_out = LANES
    half = 2 * KV_DIM
    kern = functools.partial(_compress_sample_kernel, n_pages=n_pages, page=page, t_new=t_new, n_cmp=n_cmp)
    grid_spec = pltpu.PrefetchScalarGridSpec(
        num_scalar_prefetch=1,
        grid=(dec_b,),
        in_specs=_page_specs(layer, n_pages, page, 0) + [
            pl.BlockSpec((t_new, half), lambda b, pt: (b, 0)),
            pl.BlockSpec(pe2.shape, lambda b, pt: (0, 0, 0)),
            pl.BlockSpec(w1.shape, lambda b, pt: (0, 0, 0)),
            pl.BlockSpec(w2t.shape, lambda b, pt: (0, 0, 0))],
        out_specs=[pl.BlockSpec((None, n_out, KV_DIM), lambda b, pt: (b, 0, 0))] * 2,
        scratch_shapes=[pltpu.VMEM((CMP_BLOCK, n_pad, LANES), F32)] * (half // LANES),
    )
    return pl.pallas_call(
        kern,
        grid_spec=grid_spec,
        out_shape=[jax.ShapeDtypeStruct((dec_b, n_out, KV_DIM), BF16)] * 2,
        compiler_params=_cparams("arbitrary"),
        name="nsa_compress_sample",
    )(page_table, *([cache_t] * n_pages), rows_new, pe2, w1, w2t), n_cmp


def _nsa_attn_sample_kernel(pt_ref, *refs, n_pages, page, t_new, n_cmp):
    del pt_ref
    pages = refs[:n_pages]
    (kc_ref, vc_ref, q_ref, new_ref, wnew_ref, win_ref, gates_ref,
     o_ref, nwin_ref, kst_ref, vst_ref) = refs[n_pages:]
    past = n_pages * page
    n_buf = win_ref.shape[1]
    n_sel = n_cmp // 2
    gq = Q_PER_KV * t_new
    rq = KV_HEADS * gq
    nk = past + LANES
    nw = n_buf + LANES

    for p in range(n_pages):
        blk = pages[p][...]
        kst_ref[:, p * page:(p + 1) * page] = blk[:KV_DIM].astype(BF16)
        vst_ref[:, p * page:(p + 1) * page] = blk[KV_DIM:].astype(BF16)
    pad = jnp.zeros((LANES - t_new, KV_DIM), F32)
    new = new_ref[...]
    wnew = wnew_ref[...]
    knew = jnp.concatenate([new[:, :KV_DIM], pad], axis=0).astype(BF16)
    vnew = jnp.concatenate([new[:, KV_DIM:], pad], axis=0).astype(BF16)
    kwnew = jnp.concatenate([wnew[:, :KV_DIM], pad], axis=0).astype(BF16)
    vwnew = jnp.concatenate([wnew[:, KV_DIM:], pad], axis=0).astype(BF16)
    win = win_ref[...]
    kwt = win[:KV_DIM].astype(BF16)
    vwt = win[KV_DIM:].astype(BF16)

    rolled = pltpu.roll(win, n_buf - t_new, 1)
    wnew_t = jnp.concatenate([jnp.zeros((LANES - t_new, 2 * KV_DIM), F32), wnew], axis=0).T
    tail_lane = lax.broadcasted_iota(jnp.int32, (2 * KV_DIM, LANES), 1)
    nwin_ref[:, 0:n_buf - LANES] = rolled[:, 0:n_buf - LANES]
    nwin_ref[:, n_buf - LANES:] = jnp.where(tail_lane >= LANES - t_new, wnew_t, rolled[:, n_buf - LANES:])

    q = q_ref[...]
    lane_g = lax.broadcasted_iota(jnp.int32, (t_new, KV_DIM), 1) // HEAD_DIM
    pieces = []
    for g in range(KV_HEADS):
        for j in range(Q_PER_KV):
            qj = jnp.concatenate([q[:, (Q_PER_KV * gg + j) * HEAD_DIM:(Q_PER_KV * gg + j + 1) * HEAD_DIM]
                                  for gg in range(KV_HEADS)], axis=1)
            pieces.append(jnp.where(lane_g == g, qj, jnp.zeros_like(qj)))
    qbd = jnp.concatenate(pieces, axis=0)
    t_row = past + lax.broadcasted_iota(jnp.int32, (rq, 1), 0) % t_new

    def softmax(s, mask):
        s = jnp.where(mask, s, NEG)
        e = jnp.where(mask, jnp.exp(s - jnp.max(s, axis=1, keepdims=True)), 0.0)
        return e * (1.0 / jnp.maximum(jnp.sum(e, axis=1, keepdims=True), 1e-30))

    ncl = kc_ref.shape[0]
    sc = _dot_nt(qbd, kc_ref[...])
    cidx = lax.broadcasted_iota(jnp.int32, (rq, ncl), 1)
    pc = softmax(sc, (cidx < n_cmp) & ((cidx + 1) * CMP_BLOCK - 1 <= t_row))
    oc = _dot(pc.astype(BF16), vc_ref[...])

    groups = []
    for g in range(KV_HEADS):
        pg = pc[g * gq:g * gq + t_new]
        for j in range(1, Q_PER_KV):
            pg = pg + pc[g * gq + j * t_new:g * gq + (j + 1) * t_new]
        groups.append(pg)
    pg = jnp.concatenate(groups, axis=0)
    nr = KV_HEADS * t_new
    imp = pg + pltpu.roll(pg, ncl - 1, 1)
    lane = lax.broadcasted_iota(jnp.int32, (nr, ncl), 1)
    blk = lane // 2
    cur = (past + lax.broadcasted_iota(jnp.int32, (nr, ncl), 0) % t_new) // SEL_BLOCK
    imp = jnp.where(blk > cur, -1.0, imp)
    imp = jnp.where(blk == 0, FORCED_FIRST, imp)
    imp = jnp.where(blk == cur, FORCED_CUR, imp)
    rank = jnp.zeros((nr, ncl), jnp.int32)
    for bp in range(n_sel):
        other = imp[:, 2 * bp:2 * bp + 1]
        rank = rank + ((other > imp) | ((other == imp) & (bp < blk))).astype(jnp.int32)
    sel = jnp.where((rank < TOP_N) & (lane % 2 == 0) & (blk < n_sel), 1.0, 0.0)
    sel_rows = jnp.concatenate([sel[g * t_new:(g + 1) * t_new] for g in range(KV_HEADS) for _ in range(Q_PER_KV)],
                               axis=0).astype(BF16)
    expand = jnp.where(lax.broadcasted_iota(jnp.int32, (ncl, nk), 0)
                       == 2 * (lax.broadcasted_iota(jnp.int32, (ncl, nk), 1) // SEL_BLOCK), 1.0, 0.0).astype(BF16)
    selmask = _dot(sel_rows, expand) > 0.5

    kpos = lax.broadcasted_iota(jnp.int32, (rq, nk), 1)
    ss = jnp.concatenate([_dot(qbd, kst_ref[...]), _dot_nt(qbd, knew)], axis=1)
    ps = softmax(ss, selmask & (kpos <= t_row)).astype(BF16)
    os_ = _dot_nt(ps[:, :past], vst_ref[...]) + _dot(ps[:, past:], vnew)

    wpos = past - n_buf + lax.broadcasted_iota(jnp.int32, (rq, nw), 1)
    sw = jnp.concatenate([_dot(qbd, kwt), _dot_nt(qbd, kwnew)], axis=1)
    pw = softmax(sw, (wpos <= t_row) & (wpos > t_row - WINDOW)).astype(BF16)
    ow = _dot_nt(pw[:, :n_buf], vwt) + _dot(pw[:, n_buf:], vwnew)

    gates = gates_ref[...]
    for g in range(KV_HEADS):
        for j in range(Q_PER_KV):
            h = Q_PER_KV * g + j
            rs = slice(g * gq + j * t_new, g * gq + (j + 1) * t_new)
            cs = slice(g * HEAD_DIM, (g + 1) * HEAD_DIM)
            gc = gates[:, N_BRANCH * h:N_BRANCH * h + 1]
            gs = gates[:, N_BRANCH * h + 1:N_BRANCH * h + 2]
            gw = gates[:, N_BRANCH * h + 2:N_BRANCH * h + 3]
            o_ref[:, h * HEAD_DIM:(h + 1) * HEAD_DIM] = (
                gc * oc[rs, cs] + gs * os_[rs, cs] + gw * ow[rs, cs]).astype(BF16)


def _nsa_attn_sample(page_table, cache_t, win_t, layer, kc, vc, q, rows_new, wrows_new, gates, t_new, n_cmp):
    dec_b, n_pages = page_table.shape
    page = cache_t.shape[-1]
    past = n_pages * page
    n_buf = win_t.shape[-1]
    half = 2 * KV_DIM
    d = q.shape[1]
    ncl = kc.shape[1]
    kern = functools.partial(_nsa_attn_sample_kernel, n_pages=n_pages, page=page, t_new=t_new, n_cmp=n_cmp)
    grid_spec = pltpu.PrefetchScalarGridSpec(
        num_scalar_prefetch=1,
        grid=(dec_b,),
        in_specs=_page_specs(layer, n_pages, page, 1) + [
            pl.BlockSpec((None, ncl, KV_DIM), lambda b, pt: (b, 0, 0)),
            pl.BlockSpec((None, ncl, KV_DIM), lambda b, pt: (b, 0, 0)),
            pl.BlockSpec((t_new, d), lambda b, pt: (b, 0)),
            pl.BlockSpec((t_new, half), lambda b, pt: (b, 1)),
            pl.BlockSpec((t_new, half), lambda b, pt: (b, 0)),
            pl.BlockSpec((None, None, half, n_buf), lambda b, pt: (layer, b, 0, 0)),
            pl.BlockSpec((t_new, LANES), lambda b, pt: (b, 0))],
        out_specs=[pl.BlockSpec((t_new, d), lambda b, pt: (b, 0)),
                   pl.BlockSpec((None, half, n_buf), lambda b, pt: (b, 0, 0))],
        scratch_shapes=[pltpu.VMEM((KV_DIM, past), BF16), pltpu.VMEM((KV_DIM, past), BF16)],
    )
    return pl.pallas_call(
        kern,
        grid_spec=grid_spec,
        out_shape=[jax.ShapeDtypeStruct((dec_b * t_new, d), BF16), jax.ShapeDtypeStruct((dec_b, half, n_buf), F32)],
        compiler_params=_cparams("arbitrary"),
        name="nsa_attn_sample",
    )(page_table, *([cache_t] * n_pages), kc, vc, q, rows_new, wrows_new, win_t, gates)


def _outproj_kernel(x_ref, o_ref, w_ref, y_ref):
    y_ref[...] = x_ref[...] + _dot(o_ref[...], w_ref[...])


def _outproj(x, o, w):
    m, d = x.shape
    tm = min(ROW_TILE, m)
    row = pl.BlockSpec((tm, d), lambda i: (i, 0))
    return pl.pallas_call(
        _outproj_kernel,
        grid=(m // tm,),
        in_specs=[row, row, _full(w.shape)],
        out_specs=row,
        out_shape=jax.ShapeDtypeStruct((m, d), F32),
        compiler_params=_cparams("parallel"),
        name="mixer_outproj",
    )(x, o, w)


def _mlp_kernel(x_ref, g_ref, w1_ref, w2_ref, y_ref, xn_ref, acc_ref):
    j = pl.program_id(1)

    @pl.when(j == 0)
    def _():
        x = x_ref[...]
        xn_ref[...] = _rms(x, g_ref[...]).astype(BF16)
        acc_ref[...] = x

    h = jnp.maximum(_dot(xn_ref[...], w1_ref[...]), 0.0)
    acc_ref[...] += _dot((h * h).astype(BF16), w2_ref[...])

    @pl.when(j == pl.num_programs(1) - 1)
    def _():
        y_ref[...] = acc_ref[...]


def _mlp(x, g, w1, w2):
    m, d = x.shape
    ff = w1.shape[1]
    tm, tf = min(ROW_TILE, m), 1024
    row = pl.BlockSpec((tm, d), lambda i, j: (i, 0))
    return pl.pallas_call(
        _mlp_kernel,
        grid=(m // tm, ff // tf),
        in_specs=[row, pl.BlockSpec((1, d), lambda i, j: (0, 0)),
                  pl.BlockSpec((d, tf), lambda i, j: (0, j)), pl.BlockSpec((tf, d), lambda i, j: (j, 0))],
        out_specs=row,
        out_shape=jax.ShapeDtypeStruct((m, d), F32),
        scratch_shapes=[pltpu.VMEM((tm, d), BF16), pltpu.VMEM((tm, d), F32)],
        compiler_params=_cparams("parallel", "arbitrary"),
        name="sqrelu_mlp",
    )(x, g, w1, w2)


def _ple_kernel(x_ref, p_ref, g_ref, wg_ref, wp_ref, gf_ref, y_ref, *, final_norm):
    x = x_ref[...]
    gate = jax.nn.sigmoid(_dot(_rms(x, g_ref[...]).astype(BF16), wg_ref[...]))
    y = x + _dot(p_ref[...].astype(BF16), wp_ref[...]) * gate
    if final_norm:
        y = _rms(y, gf_ref[...])
    y_ref[...] = y


def _ple(x, p_all, layer, g, wg, wp, gf, final_norm):
    m, d = x.shape
    tm = min(ROW_TILE, m)
    row = pl.BlockSpec((tm, d), lambda i: (i, 0))
    return pl.pallas_call(
        functools.partial(_ple_kernel, final_norm=final_norm),
        grid=(m // tm,),
        in_specs=[row, pl.BlockSpec((None, tm, p_all.shape[-1]), lambda i: (layer, i, 0)), _full((1, d)),
                  _full(wg.shape), _full(wp.shape), _full((1, d))],
        out_specs=row,
        out_shape=jax.ShapeDtypeStruct((m, d), F32),
        compiler_params=_cparams("parallel"),
        name="ple_gate",
    )(x, p_all, g, wg, wp, gf)


def _hg_inproj_kernel(x_ref, g_ref, lbl_ref, w_ref, q_ref, f_ref, k_ref, v_ref, gs_ref, *, layer):
    d = x_ref.shape[1]
    xn = _rms(x_ref[...], g_ref[...]).astype(BF16)

    def put(ref, val):
        for h in range(HG_HEADS):
            ref[h] = val[:, h * HG_DK:(h + 1) * HG_DK]

    lg = lbl_ref[...]
    e = jnp.exp(lg - jnp.max(lg, axis=0, keepdims=True))
    w = e / jnp.sum(e, axis=0, keepdims=True)
    cs = w[0:1]
    for r in range(1, layer + 1):
        cs = cs + w[r:r + 1]
    lb = cs - w[0:1]

    y = _dot(xn, w_ref[:, 0:d])
    put(q_ref, jax.nn.silu(y) * HG_DK ** -0.5)
    y = _dot(xn, w_ref[:, d:2 * d])
    sg = jax.nn.sigmoid(y)
    put(f_ref, jnp.log(lb + (1.0 - lb) * sg))
    put(k_ref, (1.0 - lb) * (1.0 - sg))
    put(v_ref, _dot(xn, w_ref[:, 2 * d:3 * d]))
    put(gs_ref, jax.nn.silu(_dot(xn, w_ref[:, 3 * d:4 * d])))


def _hg_inproj(x, g, lb_logits, w, layer):
    m, d = x.shape
    tm = min(ROW_TILE, m)
    head_major = pl.BlockSpec((HG_HEADS, tm, HG_DK), lambda i: (0, i, 0))
    return pl.pallas_call(
        functools.partial(_hg_inproj_kernel, layer=layer),
        grid=(m // tm,),
        in_specs=[pl.BlockSpec((tm, d), lambda i: (i, 0)), _full((1, d)), _full(lb_logits.shape), _full(w.shape)],
        out_specs=[head_major] * 5,
        out_shape=[jax.ShapeDtypeStruct((HG_HEADS, m, HG_DK), F32)] * 5,
        compiler_params=_cparams("parallel"),
        name="hgrn_inproj",
    )(x, g, lb_logits, w)


def _cumsum_rows(f):
    c = f.shape[0]
    row = lax.broadcasted_iota(jnp.int32, f.shape, 0)
    s = 1
    while s < c:
        f = f + jnp.where(row >= s, pltpu.roll(f, s, 0), 0.0)
        s *= 2
    return f


def _hg_intra(q, k, v, b, o, sub):
    c = q.shape[0]
    ones = jnp.ones((HG_DK, HG_DV), BF16)
    trow = lax.broadcasted_iota(jnp.int32, (sub, HG_DK), 0)
    terms = []
    for r0 in range(0, c, sub):
        qi, ki, bi = q[r0:r0 + sub], k[r0:r0 + sub], b[r0:r0 + sub]
        for s in range(sub):
            causal = trow >= s
            dec = jnp.exp(jnp.where(causal, bi - bi[s:s + 1], 0.0))
            terms.append(jnp.where(causal, qi * ki[s:s + 1] * dec, 0.0).astype(BF16))
    a = _dot(jnp.concatenate(terms, axis=0), ones)
    outs = []
    for r0 in range(0, c, sub):
        oi = o[r0:r0 + sub]
        if r0 > 0:
            bs = b[r0 - 1:r0]
            qp = (q[r0:r0 + sub] * jnp.exp(b[r0:r0 + sub] - bs)).astype(BF16)
            kp = (k[:r0] * jnp.exp(bs - b[:r0])).astype(BF16)
            oi = oi + _dot(_dot_nt(qp, kp).astype(BF16), v[:r0].astype(BF16))
        for s in range(sub):
            oi = oi + a[(r0 + s) * sub:(r0 + s + 1) * sub] * v[r0 + s:r0 + s + 1]
        outs.append(oi)
    return jnp.concatenate(outs, axis=0) if len(outs) > 1 else outs[0]


def _hg_out(o, ng, gs):
    return (_rms(o, ng) * gs).astype(BF16)


def _hg_diag_terms(q, k, b, sub):
    trow = lax.broadcasted_iota(jnp.int32, (sub, HG_DK), 0)
    terms = []
    for r0 in range(0, q.shape[0], sub):
        qi, ki, bi = q[r0:r0 + sub], k[r0:r0 + sub], b[r0:r0 + sub]
        for s in range(sub):
            causal = trow >= s
            dec = jnp.exp(jnp.where(causal, bi - bi[s:s + 1], 0.0))
            terms.append(jnp.where(causal, qi * ki[s:s + 1] * dec, 0.0).astype(BF16))
    return jnp.concatenate(terms, axis=0)


def _hg_offdiag(q, k, v, b, sub):
    c = q.shape[0]
    qps, kps, vps, spans = [], [], [], []
    off = 0
    for r0 in range(sub, c, sub):
        bs = b[r0 - 1:r0]
        qps.append(q[r0:r0 + sub] * jnp.exp(b[r0:r0 + sub] - bs))
        kps.append(k[:r0] * jnp.exp(bs - b[:r0]))
        vps.append(v[:r0])
        spans.append((off, off + r0))
        off += r0
    a = _dot_nt(jnp.concatenate(qps, axis=0).astype(BF16), jnp.concatenate(kps, axis=0).astype(BF16))
    row_blk = lax.broadcasted_iota(jnp.int32, a.shape, 0) // sub
    col = lax.broadcasted_iota(jnp.int32, a.shape, 1)
    keep = jnp.zeros(a.shape, jnp.bool_)
    for i, (lo, hi) in enumerate(spans):
        keep = keep | ((row_blk == i) & (col >= lo) & (col < hi))
    return _dot(jnp.where(keep, a, 0.0).astype(BF16), jnp.concatenate(vps, axis=0).astype(BF16))


def _hg_scan_prompt_kernel(q_ref, f_ref, k_ref, v_ref, gs_ref, ng_ref, o_ref, s_ref,
                           st_ref, b_ref, t_ref, a_ref, i_ref):
    c = pl.program_id(1)
    rows = q_ref.shape[1]
    n_chunk = rows // HG_CHUNK
    tr = HG_CHUNK * HG_SUB
    ones = jnp.ones((HG_DK, HG_DV), BF16)

    @pl.when(c == 0)
    def _():
        st_ref[...] = jnp.zeros(st_ref.shape, F32)

    def heads(it, carry):
        for hh in range(HG_UNROLL_PROMPT):
            h = it * HG_UNROLL_PROMPT + hh
            for ci in range(n_chunk):
                u = hh * n_chunk + ci
                rs = pl.ds(ci * HG_CHUNK, HG_CHUNK)
                b = _cumsum_rows(f_ref[h, rs, :])
                b_ref[u] = b
                t_ref[u * tr:(u + 1) * tr, :] = _hg_diag_terms(q_ref[h, rs, :], k_ref[h, rs, :], b, HG_SUB)
        a_ref[...] = _dot(t_ref[...], ones)
        for hh in range(HG_UNROLL_PROMPT):
            h = it * HG_UNROLL_PROMPT + hh
            for ci in range(n_chunk):
                u = hh * n_chunk + ci
                rs = pl.ds(ci * HG_CHUNK, HG_CHUNK)
                q, k, v, b = q_ref[h, rs, :], k_ref[h, rs, :], v_ref[h, rs, :], b_ref[u]
                off = _hg_offdiag(q, k, v, b, HG_SUB)
                for r0 in range(0, HG_CHUNK, HG_SUB):
                    oi = off[r0 - HG_SUB:r0] if r0 > 0 else jnp.zeros((HG_SUB, HG_DV), F32)
                    for s in range(HG_SUB):
                        a0 = u * tr + (r0 + s) * HG_SUB
                        oi = oi + a_ref[a0:a0 + HG_SUB, :] * v[r0 + s:r0 + s + 1]
                    i_ref[u, r0:r0 + HG_SUB, :] = oi
        for ci in range(n_chunk):
            rs = pl.ds(ci * HG_CHUNK, HG_CHUNK)
            for hh in range(HG_UNROLL_PROMPT):
                h = it * HG_UNROLL_PROMPT + hh
                u = hh * n_chunk + ci
                q, k, v, b = q_ref[h, rs, :], k_ref[h, rs, :], v_ref[h, rs, :], b_ref[u]
                st = st_ref[h]
                o = i_ref[u] + _dot_nt((q * jnp.exp(b)).astype(BF16), st.astype(BF16))
                o_ref[h, rs, :] = _hg_out(o, ng_ref[...], gs_ref[h, rs, :])
                bl = b[HG_CHUNK - 1:HG_CHUNK]
                kd = (k * jnp.exp(bl - b)).astype(BF16)
                st_ref[h] = st * jnp.exp(bl) + lax.dot_general(v.astype(BF16), kd, (((0,), (0,)), ((), ())),
                                                               preferred_element_type=F32)

        @pl.when(c == pl.num_programs(1) - 1)
        def _():
            for hh in range(HG_UNROLL_PROMPT):
                h = it * HG_UNROLL_PROMPT + hh
                s_ref[h] = st_ref[h].T
        return carry

    lax.fori_loop(0, HG_HEADS // HG_UNROLL_PROMPT, heads, 0)


def _hg_scan_prompt(q, f, k, v, gs, ng, batch, t_seq):
    m = q.shape[1]
    rows = 2 * HG_CHUNK
    nc = t_seq // rows
    units = HG_UNROLL_PROMPT * (rows // HG_CHUNK)
    blk = pl.BlockSpec((HG_HEADS, rows, HG_DK), lambda b, c: (0, b * nc + c, 0))
    return pl.pallas_call(
        _hg_scan_prompt_kernel,
        grid=(batch, nc),
        in_specs=[blk] * 5 + [pl.BlockSpec((1, HG_DV), lambda b, c: (0, 0))],
        out_specs=[blk, pl.BlockSpec((None, HG_HEADS, HG_DK, HG_DV), lambda b, c: (b, 0, 0, 0))],
        out_shape=[jax.ShapeDtypeStruct((HG_HEADS, m, HG_DV), BF16),
                   jax.ShapeDtypeStruct((batch, HG_HEADS, HG_DK, HG_DV), F32)],
        scratch_shapes=[pltpu.VMEM((HG_HEADS, HG_DV, HG_DK), F32),
                        pltpu.VMEM((units, HG_CHUNK, HG_DK), F32),
                        pltpu.VMEM((units * HG_CHUNK * HG_SUB, HG_DK), BF16),
                        pltpu.VMEM((units * HG_CHUNK * HG_SUB, HG_DV), F32),
                        pltpu.VMEM((units, HG_CHUNK, HG_DV), F32)],
        compiler_params=_cparams("parallel", "arbitrary"),
        name="hgrn_scan_prompt",
    )(q, f, k, v, gs, ng)


def _hg_step_sample_kernel(q_ref, f_ref, k_ref, v_ref, gs_ref, ng_ref, s0_ref, o_ref, s1_ref, *, t_new):
    n_seq = s0_ref.shape[0]

    def one(sq, h):
        rs = pl.ds(pl.multiple_of(sq * t_new, t_new), t_new)
        q, k, v = q_ref[h, rs, :], k_ref[h, rs, :], v_ref[h, rs, :]
        b = _cumsum_rows(f_ref[h, rs, :])
        s0 = s0_ref[sq, h]
        o = _dot((q * jnp.exp(b)).astype(BF16), s0.astype(BF16))
        o = _hg_intra(q, k, v, b, o, t_new)
        o_ref[h, rs, :] = _hg_out(o, ng_ref[...], gs_ref[h, rs, :])
        bl = b[t_new - 1:t_new]
        kd = k * jnp.exp(bl - b)
        ext = jnp.concatenate([kd, jnp.broadcast_to(jnp.exp(bl), (t_new, HG_DK)),
                               jnp.zeros((HG_DK - 2 * t_new, HG_DK), F32)], axis=0).T
        s1_ref[sq, h] = s0 * ext[:, t_new:t_new + 1] + _dot(ext[:, :t_new].astype(BF16), v.astype(BF16))

    def body(idx, carry):
        per_seq = HG_HEADS // HG_UNROLL_SAMPLE
        for hh in range(HG_UNROLL_SAMPLE):
            one(idx // per_seq, (idx % per_seq) * HG_UNROLL_SAMPLE + hh)
        return carry

    lax.fori_loop(0, n_seq * (HG_HEADS // HG_UNROLL_SAMPLE), body, 0)


def _hg_step_sample(q, f, k, v, gs, ng, s0_all, layer, t_new):
    m = q.shape[1]
    dec_b = s0_all.shape[1]
    sb = min(8, dec_b)
    blk = pl.BlockSpec((HG_HEADS, sb * t_new, HG_DK), lambda i: (0, i, 0))
    st_in = pl.BlockSpec((None, sb, HG_HEADS, HG_DK, HG_DV), lambda i: (layer, i, 0, 0, 0))
    st_out = pl.BlockSpec((sb, HG_HEADS, HG_DK, HG_DV), lambda i: (i, 0, 0, 0))
    return pl.pallas_call(
        functools.partial(_hg_step_sample_kernel, t_new=t_new),
        grid=(dec_b // sb,),
        in_specs=[blk] * 5 + [pl.BlockSpec((1, HG_DV), lambda i: (0, 0)), st_in],
        out_specs=[blk, st_out],
        out_shape=[jax.ShapeDtypeStruct((HG_HEADS, m, HG_DV), BF16), jax.ShapeDtypeStruct(s0_all.shape[1:], F32)],
        compiler_params=_cparams("parallel"),
        name="hgrn_step_sample",
    )(q, f, k, v, gs, ng, s0_all)


def _hg_outproj_kernel(x_ref, o_ref, w_ref, y_ref):
    o = jnp.concatenate([o_ref[h] for h in range(HG_HEADS)], axis=1)
    y_ref[...] = x_ref[...] + _dot(o, w_ref[...])


def _hg_outproj(x, o, w):
    m, d = x.shape
    tm = min(ROW_TILE, m)
    row = pl.BlockSpec((tm, d), lambda i: (i, 0))
    return pl.pallas_call(
        _hg_outproj_kernel,
        grid=(m // tm,),
        in_specs=[row, pl.BlockSpec((HG_HEADS, tm, HG_DV), lambda i: (0, i, 0)), _full(w.shape)],
        out_specs=row,
        out_shape=jax.ShapeDtypeStruct((m, d), F32),
        compiler_params=_cparams("parallel"),
        name="hgrn_outproj",
    )(x, o, w)


def _feature_major_view(a):
    lead = a.shape[:-4]
    t, c, g, d = a.shape[-4:]
    n = len(lead)
    return jnp.transpose(a, tuple(range(n)) + (n + 1, n + 2, n + 3, n)).reshape(lead + (c * g * d, t))


def _token_major_view(a, c):
    lead = a.shape[:-2]
    t = a.shape[-1]
    n = len(lead)
    a = a.reshape(lead + (c, KV_HEADS, HEAD_DIM, t))
    return jnp.transpose(a, tuple(range(n)) + (n + 3, n, n + 1, n + 2))


def kernel(x_prompt, x_sample, cache_nsa_kv, state_nsa_win, state_hgrn, page_table, p_prompt, p_sample, norm_mix, norm_mlp, norm_ple, norm_final, nsa_w_in, nsa_cmp_pe, nsa_cmp_w1, nsa_cmp_w2, nsa_w_out, hg_w_in, hg_lb_logits, hg_norm, hg_w_out, mlp_w1, mlp_w2, ple_w_proj, ple_w_gate):
    batch, t_p, d = x_prompt.shape
    dec_b, t_s, _ = x_sample.shape
    depth = p_prompt.shape[0]
    page = cache_nsa_kv.shape[2]
    past = page_table.shape[1] * page

    xp = x_prompt.reshape(batch * t_p, d)
    xs = x_sample.reshape(dec_b * t_s, d)
    pp = p_prompt.reshape(depth, batch * t_p, -1)
    ps = p_sample.reshape(depth, dec_b * t_s, -1)
    row = lambda a: a.reshape(1, -1)
    cache_t = _feature_major_view(cache_nsa_kv)
    win_t = _feature_major_view(state_nsa_win)
    wl = min(WINDOW, t_p)

    tabs_p = _rope_tables(t_p, 0, t_p)
    tabs_s = _rope_tables(min(ROW_TILE, dec_b * t_s), past, t_s)

    kv_p, kv_s, win_p, win_s, st_p, st_s = [], [], [], [], [], []
    for i in range(depth):
        g_mix = row(norm_mix[i])
        if i % 2 == 0:
            a = i // 2
            n_in = nsa_w_in.shape[2]
            w_t = jnp.pad(jnp.transpose(nsa_w_in[a]).astype(BF16),
                          ((0, Q_DIM + 6 * KV_DIM + LANES - n_in), (0, 0)))
            pe2 = jnp.tile(nsa_cmp_pe[a], (1, 1, LANES // HEAD_DIM))
            cw1 = nsa_cmp_w1[a].reshape(2, CMP_BLOCK * HEAD_DIM, -1).astype(BF16)
            cw2t = jnp.swapaxes(nsa_cmp_w2[a], 1, 2).astype(BF16)
            w_out = nsa_w_out[a].astype(BF16)

            qt, gatest, cmp_rows, rows_t, wrows_t, ksa, vsa, kwa, vwa = _nsa_inproj_prompt(
                xp, g_mix, tabs_p, w_t, batch, t_p)
            kc, vct = _compress_prompt(cmp_rows, pe2, cw1, cw2t, batch, t_p)
            o = _nsa_attn_prompt(qt, gatest, kc, vct, ksa, vsa, kwa, vwa, batch, t_p)
            xp = _outproj(xp, o, w_out)
            kv_p.append(rows_t)
            win_p.append(wrows_t[:, :, t_p - wl:])

            q, gates, rows, wrows = _nsa_inproj_sample(xs, g_mix, tabs_s, w_t)
            (kc, vc), n_cmp = _compress_sample(page_table, cache_t, a, rows, pe2, cw1, cw2t, t_s)
            o, new_win_t = _nsa_attn_sample(page_table, cache_t, win_t, a, kc, vc, q, rows, wrows, gates, t_s, n_cmp)
            xs = _outproj(xs, o, w_out)
            kv_s.append(rows.reshape(dec_b, t_s, 4, KV_HEADS, HEAD_DIM))
            win_s.append(new_win_t)
        else:
            r = i // 2
            w_in = hg_w_in[r].astype(BF16)
            w_out = hg_w_out[r].astype(BF16)
            ng = row(hg_norm[r])
            q, f, k, v, gs = _hg_inproj(xp, g_mix, hg_lb_logits, w_in, r)
            o, s_new = _hg_scan_prompt(q, f, k, v, gs, ng, batch, t_p)
            xp = _hg_outproj(xp, o, w_out)
            st_p.append(s_new)
            q, f, k, v, gs = _hg_inproj(xs, g_mix, hg_lb_logits, w_in, r)
            o, s_new = _hg_step_sample(q, f, k, v, gs, ng, state_hgrn, r, t_s)
            xs = _hg_outproj(xs, o, w_out)
            st_s.append(s_new)

        w1 = mlp_w1[i].astype(BF16)
        w2 = mlp_w2[i].astype(BF16)
        wgate = ple_w_gate[i].astype(BF16)
        wproj = ple_w_proj[i].astype(BF16)
        last = i == depth - 1
        xp = _mlp(xp, row(norm_mlp[i]), w1, w2)
        xs = _mlp(xs, row(norm_mlp[i]), w1, w2)
        xp = _ple(xp, pp, i, row(norm_ple[i]), wgate, wproj, row(norm_final), last)
        xs = _ple(xs, ps, i, row(norm_ple[i]), wgate, wproj, row(norm_final), last)

    return (xp.reshape(batch, t_p, d), xs.reshape(dec_b, t_s, d),
            _token_major_view(jnp.stack(kv_p), 4), jnp.stack(kv_s),
            _token_major_view(jnp.stack(win_p), 2), _token_major_view(jnp.stack(win_s), 2),
            jnp.stack(st_p), jnp.stack(st_s))
```

```python
import functools

import jax
import jax.numpy as jnp
from jax import lax
from jax.experimental import pallas as pl
from jax.experimental.pallas import tpu as pltpu

F32 = jnp.float32
BF16 = jnp.bfloat16

RMS_EPS = 1e-6
ROPE_THETA = 10000.0
N_HEADS = 16
HEAD_DIM = 64
KV_HEADS = 4
Q_PER_KV = N_HEADS // KV_HEADS
Q_DIM = N_HEADS * HEAD_DIM
KV_DIM = KV_HEADS * HEAD_DIM
N_BRANCH = 3
CMP_BLOCK = 32
SEL_BLOCK = 64
TOP_N = 16
WINDOW = 512
FORCED_CUR = 3e4
FORCED_FIRST = 2e4
HG_HEADS = 8
HG_DK = 128
HG_DV = 128
HG_CHUNK = 64
HG_SUB = 16
HG_UNROLL_PROMPT = 4
HG_UNROLL_SAMPLE = 8
NEG = -1e30
LOG2_E = 1.4426950408889634

LANES = 128
SUBLANES = 8
ROW_TILE = 512
ATT_TILE = 256
V_AUG_ROWS = HEAD_DIM + 2 * SUBLANES
VMEM_LIMIT = 56 * 1024 * 1024


def _cparams(*sem):
    return pltpu.CompilerParams(dimension_semantics=sem, vmem_limit_bytes=VMEM_LIMIT)


def _full(shape):
    return pl.BlockSpec(shape, lambda *_: (0,) * len(shape))


def _rms(x, g):
    return x * lax.rsqrt(jnp.mean(x * x, axis=-1, keepdims=True) + RMS_EPS) * g


def _dot(a, b):
    return jnp.dot(a, b, preferred_element_type=F32)


def _dot_nt(a, b):
    return lax.dot_general(a, b, (((1,), (1,)), ((), ())), preferred_element_type=F32)


def _rope_table_kernel(inv_ref, invc_ref, cos_ref, sin_ref, cost_ref, sint_ref, *, pos0, period):
    i = pl.program_id(0)
    tm = cos_ref.shape[0]
    half = HEAD_DIM // 2
    row = i * tm + lax.broadcasted_iota(jnp.int32, (tm, LANES), 0)
    ang = (pos0 + row % period).astype(F32) * inv_ref[...]
    lane = lax.broadcasted_iota(jnp.int32, (tm, LANES), 1)
    cos_ref[...] = jnp.cos(ang)
    s = jnp.sin(ang)
    sin_ref[...] = jnp.where(lane % HEAD_DIM < half, -s, s)
    col = i * tm + lax.broadcasted_iota(jnp.int32, (HEAD_DIM, tm), 1)
    ang_t = (pos0 + col % period).astype(F32) * jnp.concatenate([invc_ref[...]] * (tm // LANES), axis=1)
    feat = lax.broadcasted_iota(jnp.int32, (HEAD_DIM, tm), 0)
    cost_ref[...] = jnp.cos(ang_t)
    st = jnp.sin(ang_t)
    sint_ref[...] = jnp.where(feat < half, -st, st)


def _rope_tables(rows, pos0, period):
    half = HEAD_DIM // 2
    inv = ROPE_THETA ** (-jnp.arange(half, dtype=F32) / half)
    inv_row = jnp.tile(inv, LANES // half)[None, :]
    inv_col = jnp.broadcast_to(jnp.tile(inv, HEAD_DIM // half)[:, None], (HEAD_DIM, LANES))
    kern = functools.partial(_rope_table_kernel, pos0=pos0, period=period)
    tm = min(ROW_TILE, rows)
    tok = pl.BlockSpec((tm, LANES), lambda i: (i, 0))
    feat = pl.BlockSpec((HEAD_DIM, tm), lambda i: (0, i))
    return pl.pallas_call(
        kern,
        grid=(rows // tm,),
        in_specs=[_full((1, LANES)), _full((HEAD_DIM, LANES))],
        out_specs=[tok, tok, feat, feat],
        out_shape=[jax.ShapeDtypeStruct((rows, LANES), F32)] * 2 + [jax.ShapeDtypeStruct((HEAD_DIM, rows), F32)] * 2,
        compiler_params=_cparams("parallel"),
        name="rope_tables",
    )(inv_row, inv_col)


def _rope_tok(y, cos, sin):
    lane = lax.broadcasted_iota(jnp.int32, y.shape, 1)
    first = lane % HEAD_DIM < HEAD_DIM // 2
    sw = jnp.where(first, pltpu.roll(y, LANES - HEAD_DIM // 2, 1), pltpu.roll(y, HEAD_DIM // 2, 1))
    return y * cos + sw * sin


def _rope_feat(y, cost, sint):
    return y * cost + pltpu.roll(y, HEAD_DIM // 2, 0) * sint


def _nsa_inproj_prompt_kernel(x_ref, g_ref, cos_ref, sin_ref, cost_ref, sint_ref, w_ref,
                              qt_ref, gatest_ref, cmp_ref, rowst_ref, wrowst_ref,
                              ksa_ref, vsa_ref, kwa_ref, vwa_ref):
    tm = x_ref.shape[0]
    tk = ksa_ref.shape[2]
    i = pl.program_id(1)
    xn = _rms(x_ref[...], g_ref[...]).astype(BF16)
    cos, sin = cos_ref[...], sin_ref[...]
    cost, sint = cost_ref[...], sint_ref[...]
    scale = HEAD_DIM ** -0.5 * LOG2_E

    qt = _dot_nt(w_ref[0:Q_DIM, :], xn)
    for h in range(N_HEADS):
        rs = slice(h * HEAD_DIM, (h + 1) * HEAD_DIM)
        qt_ref[rs, :] = (_rope_feat(qt[rs], cost, sint) * scale).astype(BF16)
    g0 = Q_DIM + 6 * KV_DIM
    gatest_ref[...] = jax.nn.sigmoid(_dot_nt(w_ref[g0:g0 + LANES, :], xn))

    ycmp = _dot_nt(xn, w_ref[Q_DIM:Q_DIM + 2 * KV_DIM, :])
    for c in range(2 * KV_DIM // LANES):
        cs = slice(c * LANES, (c + 1) * LANES)
        cmp_ref[:, cs] = _rope_tok(ycmp[:, cs], cos, sin) if c * LANES < KV_DIM else ycmp[:, cs]

    lane = lax.broadcasted_iota(jnp.int32, (tm, LANES), 1)
    lo = lane < HEAD_DIM
    pos = i * tm + lax.broadcasted_iota(jnp.int32, (tm, LANES), 0)
    sel_bias = jnp.where(lane - HEAD_DIM == pos // SEL_BLOCK, NEG, 0.0)
    for part, ref, fill in ((2, ksa_ref, sel_bias), (4, kwa_ref, 0.0)):
        yk = _dot_nt(xn, w_ref[Q_DIM + part * KV_DIM:Q_DIM + (part + 1) * KV_DIM, :])
        for c in range(KV_DIM // LANES):
            y = _rope_tok(yk[:, c * LANES:(c + 1) * LANES], cos, sin)
            pair = (jnp.where(lo, y, fill).astype(BF16), jnp.where(lo, pltpu.roll(y, HEAD_DIM, 1), fill).astype(BF16))
            for gg in range(2):
                for j in range(tm // tk):
                    ref[2 * c + gg, j] = pair[gg][j * tk:(j + 1) * tk, :]

    yt = _dot_nt(w_ref[Q_DIM:Q_DIM + 6 * KV_DIM, :], xn)
    ones = jnp.ones((V_AUG_ROWS - HEAD_DIM, tm), F32)
    for part in range(6):
        for g in range(KV_HEADS):
            r0 = part * KV_DIM + g * HEAD_DIM
            y = yt[r0:r0 + HEAD_DIM, :]
            if part % 2 == 0:
                y = _rope_feat(y, cost, sint)
            if part < 4:
                rowst_ref[r0:r0 + HEAD_DIM, :] = y
            else:
                wrowst_ref[r0 - 4 * KV_DIM:r0 - 4 * KV_DIM + HEAD_DIM, :] = y
            if part in (3, 5):
                ref = vsa_ref if part == 3 else vwa_ref
                ya = jnp.concatenate([y, ones], axis=0).astype(BF16)
                for j in range(tm // tk):
                    ref[g, j] = ya[:, j * tk:(j + 1) * tk]


def _nsa_inproj_prompt(x, g, tabs, w, batch, t_seq):
    m, d = x.shape
    tm = min(ROW_TILE, t_seq)
    tk = min(ATT_TILE, t_seq)
    nt = t_seq // tm
    cos, sin, cost, sint = tabs
    row = lambda cols: pl.BlockSpec((tm, cols), lambda b, i: (b * nt + i, 0))
    tok_tab = pl.BlockSpec((tm, LANES), lambda b, i: (i, 0))
    feat_tab = pl.BlockSpec((HEAD_DIM, tm), lambda b, i: (0, i))
    featmaj = lambda rows: pl.BlockSpec((None, rows, tm), lambda b, i: (b, 0, i))
    ktiles = pl.BlockSpec((None, KV_HEADS, tm // tk, tk, LANES), lambda b, i: (b, 0, i, 0, 0))
    vtiles = pl.BlockSpec((None, KV_HEADS, tm // tk, V_AUG_ROWS, tk), lambda b, i: (b, 0, i, 0, 0))
    kshape = jax.ShapeDtypeStruct((batch, KV_HEADS, t_seq // tk, tk, LANES), BF16)
    vshape = jax.ShapeDtypeStruct((batch, KV_HEADS, t_seq // tk, V_AUG_ROWS, tk), BF16)
    return pl.pallas_call(
        _nsa_inproj_prompt_kernel,
        grid=(batch, nt),
        in_specs=[row(d), _full((1, d)), tok_tab, tok_tab, feat_tab, feat_tab, _full(w.shape)],
        out_specs=[featmaj(Q_DIM), featmaj(LANES), row(2 * KV_DIM), featmaj(4 * KV_DIM), featmaj(2 * KV_DIM),
                   ktiles, vtiles, ktiles, vtiles],
        out_shape=[jax.ShapeDtypeStruct((batch, Q_DIM, t_seq), BF16), jax.ShapeDtypeStruct((batch, LANES, t_seq), F32),
                   jax.ShapeDtypeStruct((m, 2 * KV_DIM), F32),
                   jax.ShapeDtypeStruct((batch, 4 * KV_DIM, t_seq), F32),
                   jax.ShapeDtypeStruct((batch, 2 * KV_DIM, t_seq), F32), kshape, vshape, kshape, vshape],
        compiler_params=_cparams("parallel", "parallel"),
        name="nsa_inproj_prompt",
    )(x, g, cos, sin, cost, sint, w)


def _nsa_inproj_sample_kernel(x_ref, g_ref, cos_ref, sin_ref, w_ref, q_ref, gates_ref, rows_ref, wrows_ref):
    xn = _rms(x_ref[...], g_ref[...]).astype(BF16)
    cos, sin = cos_ref[...], sin_ref[...]
    scale = HEAD_DIM ** -0.5
    yq = _dot_nt(xn, w_ref[0:Q_DIM, :])
    for c in range(Q_DIM // LANES):
        cs = slice(c * LANES, (c + 1) * LANES)
        q_ref[:, cs] = (_rope_tok(yq[:, cs], cos, sin) * scale).astype(BF16)
    g0 = Q_DIM + 6 * KV_DIM
    gates_ref[...] = jax.nn.sigmoid(_dot_nt(xn, w_ref[g0:g0 + LANES, :]))
    ykv = _dot_nt(xn, w_ref[Q_DIM:g0, :])
    for c in range(6 * KV_DIM // LANES):
        cs = slice(c * LANES, (c + 1) * LANES)
        part = c * LANES // KV_DIM
        y = _rope_tok(ykv[:, cs], cos, sin) if part % 2 == 0 else ykv[:, cs]
        if part < 4:
            rows_ref[:, cs] = y
        else:
            wrows_ref[:, c * LANES - 4 * KV_DIM:(c + 1) * LANES - 4 * KV_DIM] = y


def _nsa_inproj_sample(x, g, tabs, w):
    m, d = x.shape
    tm = min(ROW_TILE, m)
    cos, sin = tabs[0], tabs[1]
    n_tab = cos.shape[0] // tm
    row = lambda cols: pl.BlockSpec((tm, cols), lambda i: (i, 0))
    tab = pl.BlockSpec((tm, LANES), lambda i: (i % n_tab, 0))
    return pl.pallas_call(
        _nsa_inproj_sample_kernel,
        grid=(m // tm,),
        in_specs=[row(d), _full((1, d)), tab, tab, _full(w.shape)],
        out_specs=[row(d), row(LANES), row(4 * KV_DIM), row(2 * KV_DIM)],
        out_shape=[jax.ShapeDtypeStruct((m, d), BF16), jax.ShapeDtypeStruct((m, LANES), F32),
                   jax.ShapeDtypeStruct((m, 4 * KV_DIM), F32), jax.ShapeDtypeStruct((m, 2 * KV_DIM), F32)],
        compiler_params=_cparams("parallel"),
        name="nsa_inproj_sample",
    )(x, g, cos, sin, w)


def _compress_rows(load, pe_ref, w1_ref, w2t_ref, kv, n, transposed=False):
    lane = lax.broadcasted_iota(jnp.int32, (n, LANES), 1)
    lo = lane < HEAD_DIM
    acc = jnp.zeros((KV_HEADS * n, w1_ref.shape[-1]), F32)
    lq = 4
    for l0 in range(0, CMP_BLOCK, lq):
        halves = [[], []]
        for l in range(l0, l0 + lq, 2):
            for hf in range(2):
                a = load(l, hf) + pe_ref[kv, l:l + 1, :]
                b = load(l + 1, hf) + pe_ref[kv, l + 1:l + 2, :]
                even = jnp.where(lo, a, pltpu.roll(b, HEAD_DIM, 1))
                odd = jnp.where(lo, pltpu.roll(a, HEAD_DIM, 1), b)
                halves[hf].append((even, odd))
        groups = []
        for g in range(KV_HEADS):
            hf, par = divmod(g, 2)
            groups.append(jnp.concatenate([pc[par] for pc in halves[hf]], axis=1))
        xg = jnp.concatenate(groups, axis=0).astype(BF16)
        acc = acc + _dot(xg, w1_ref[kv, l0 * HEAD_DIM:(l0 + lq) * HEAD_DIM, :])
    h = jax.nn.gelu(acc).astype(BF16)
    return _dot_nt(w2t_ref[kv], h) if transposed else _dot_nt(h, w2t_ref[kv])


def _compress_prompt_kernel(s0_ref, s1_ref, s2_ref, s3_ref, pe_ref, w1_ref, w2t_ref, kc_ref, vct_ref):
    n = kc_ref.shape[0]
    hn = n // 2
    srcs = ((s0_ref, s1_ref), (s2_ref, s3_ref))

    def loader(kv):
        def load(l, hf):
            ev = srcs[kv][hf][pl.ds(l, hn, stride=2 * CMP_BLOCK), :]
            od = srcs[kv][hf][pl.ds(CMP_BLOCK + l, hn, stride=2 * CMP_BLOCK), :]
            return jnp.concatenate([ev, od], axis=0)
        return load

    y = _compress_rows(loader(0), pe_ref, w1_ref, w2t_ref, 0, n)
    yt = _compress_rows(loader(1), pe_ref, w1_ref, w2t_ref, 1, n, transposed=True)
    for g in range(KV_HEADS):
        kc_ref[:, g * HEAD_DIM:(g + 1) * HEAD_DIM] = y[g * n:(g + 1) * n].astype(BF16)
        vct_ref[g * HEAD_DIM:(g + 1) * HEAD_DIM, :] = yt[:, g * n:(g + 1) * n].astype(BF16)


def _compress_prompt(cmp_rows, pe2, w1, w2t, batch, t_seq):
    n = t_seq // CMP_BLOCK
    chunk = lambda c: pl.BlockSpec((t_seq, LANES), lambda b: (b, c))
    return pl.pallas_call(
        _compress_prompt_kernel,
        grid=(batch,),
        in_specs=[chunk(0), chunk(1), chunk(2), chunk(3), _full(pe2.shape), _full(w1.shape), _full(w2t.shape)],
        out_specs=[pl.BlockSpec((None, n, KV_DIM), lambda b: (b, 0, 0)),
                   pl.BlockSpec((None, KV_DIM, n), lambda b: (b, 0, 0))],
        out_shape=[jax.ShapeDtypeStruct((batch, n, KV_DIM), BF16), jax.ShapeDtypeStruct((batch, KV_DIM, n), BF16)],
        compiler_params=_cparams("parallel"),
        name="nsa_compress_prompt",
    )(cmp_rows, cmp_rows, cmp_rows, cmp_rows, pe2, w1, w2t)


def _topk_not_selected(imp_t, n_live, rank_ref):
    n_sel, nq = imp_t.shape
    slabs = [imp_t[r0:r0 + SUBLANES] for r0 in range(0, n_sel, SUBLANES)]
    sub = lax.broadcasted_iota(jnp.int32, (SUBLANES, nq), 0)
    rank_ref[...] = jnp.zeros((n_sel, nq), F32)
    for c0 in range(0, n_sel, SUBLANES):
        @pl.when(c0 < n_live)
        def _():
            for v, slab in enumerate(slabs):
                r0 = v * SUBLANES
                cnt = jnp.zeros((SUBLANES, nq), F32)
                for bp in range(c0, c0 + SUBLANES):
                    other = imp_t[bp:bp + 1, :]
                    if bp < r0:
                        ahead = other >= slab
                    elif bp >= r0 + SUBLANES - 1:
                        ahead = other > slab
                    else:
                        ahead = (other > slab) | ((other == slab) & (bp - r0 < sub))
                    cnt = cnt + jnp.where(ahead, 1.0, 0.0)
                rank_ref[r0:r0 + SUBLANES, :] += cnt
    return jnp.where(rank_ref[...] < TOP_N, 0.0, 1.0)


def _nsa_attn_prompt_kernel(qt_ref, gatest_ref, kc_ref, vct_ref, ksa_ref, vsa_ref, kwa_ref, vwa_ref, o_ref,
                            acc_ref, m_ref, accw_ref, mw_ref, sa_ref, sb_ref, wb_ref, cb_ref, cv_ref, rank_ref):
    tq = qt_ref.shape[1]
    tk = ksa_ref.shape[2]
    n_cmp = kc_ref.shape[0]
    n_sel = n_cmp // 2
    rq = Q_PER_KV * tq
    i = pl.program_id(1)
    t0 = i * tq
    t_col = t0 + lax.broadcasted_iota(jnp.int32, (1, rq), 1) % tq
    key_row = lax.broadcasted_iota(jnp.int32, (tk, rq), 0)
    n_wt = -(-(WINDOW - 1) // tk) + 1
    kt_diag = t0 // tk
    wkt = [kt_diag - j for j in range(n_wt)]

    for j, kt in enumerate(wkt):
        kpos = kt * tk + key_row
        wb_ref[j] = jnp.where((kpos >= 0) & (kpos <= t_col) & (kpos > t_col - WINDOW), 0.0, NEG)
    nrow = lax.broadcasted_iota(jnp.int32, (n_cmp, rq), 0)
    cblk = jnp.where(nrow < n_sel, 2 * nrow, 2 * (nrow - n_sel) + 1)
    cvis = (cblk + 1) * CMP_BLOCK - 1 <= t_col
    cb_ref[...] = jnp.where(cvis, 0.0, NEG)
    cv_ref[...] = jnp.where(cvis, 1.0, 0.0)

    def online_update(s_ref, vt, bias, m_ref, acc_ref):
        s = s_ref[...]
        if bias is not None:
            s = s + bias
        m_old = m_ref[...]
        m_new = jnp.maximum(m_old, jnp.max(s, axis=0, keepdims=True))
        p = jnp.exp2(s - m_new)
        acc_ref[...] = acc_ref[...] * jnp.exp2(m_old - m_new) + _dot(vt, p.astype(BF16))
        m_ref[...] = m_new

    def normalized(acc):
        return acc[:HEAD_DIM] * (1.0 / jnp.maximum(acc[HEAD_DIM:HEAD_DIM + 1], 1e-30))

    for g in range(KV_HEADS):
        qg = jnp.concatenate(
            [qt_ref[(Q_PER_KV * g + r) * HEAD_DIM:(Q_PER_KV * g + r + 1) * HEAD_DIM, :] for r in range(Q_PER_KV)],
            axis=1)

        sc = _dot(kc_ref[:, g * HEAD_DIM:(g + 1) * HEAD_DIM], qg) + cb_ref[...]
        e = jnp.exp2(sc - jnp.max(sc, axis=0, keepdims=True)) * cv_ref[...]
        pc = e * (1.0 / jnp.maximum(jnp.sum(e, axis=0, keepdims=True), 1e-30))
        o_c = _dot(vct_ref[g * HEAD_DIM:(g + 1) * HEAD_DIM, :], pc.astype(BF16))

        pq = pc[:, 0:tq]
        for r in range(1, Q_PER_KV):
            pq = pq + pc[:, r * tq:(r + 1) * tq]
        imp = pq[:n_sel] + pq[n_sel:]
        blk = lax.broadcasted_iota(jnp.int32, (n_sel, tq), 0)
        cur = (t0 + lax.broadcasted_iota(jnp.int32, (n_sel, tq), 1)) // SEL_BLOCK
        imp = jnp.where(blk > cur, -1.0, imp)
        imp = jnp.where(blk == 0, FORCED_FIRST, imp)
        imp = jnp.where(blk == cur, FORCED_CUR, imp)
        n_live = (t0 + tq - 1) // SEL_BLOCK + 1
        notsel = _topk_not_selected(imp, n_live, rank_ref)
        if n_sel < HEAD_DIM:
            notsel = jnp.concatenate([notsel, jnp.zeros((HEAD_DIM - n_sel, tq), F32)], axis=0)
        qa = jnp.concatenate([qg, jnp.concatenate([notsel.astype(BF16)] * Q_PER_KV, axis=1)], axis=0)

        for ref in (m_ref, mw_ref):
            ref[...] = jnp.full((1, rq), NEG, F32)
        for ref in (acc_ref, accw_ref):
            ref[...] = jnp.zeros((V_AUG_ROWS, rq), F32)
        slots = [sa_ref, sb_ref]
        qw = jnp.concatenate([qg, jnp.zeros_like(qg)], axis=0)

        slots[0][...] = _dot(kwa_ref[g, wkt[0]], qw)
        for j, kt in enumerate(wkt):
            if j + 1 < n_wt:
                slots[1 - j % 2][...] = _dot(kwa_ref[g, jnp.maximum(wkt[j + 1], 0)], qw)
            else:
                slots[1 - j % 2][...] = _dot(ksa_ref[g, 0], qa)
            online_update(slots[j % 2], vwa_ref[g, jnp.maximum(kt, 0)], wb_ref[j], mw_ref, accw_ref)
        o_w = normalized(accw_ref[...])

        s_even, s_odd = slots[n_wt % 2], slots[1 - n_wt % 2]

        def sel_pair(j, carry):
            s_odd[...] = _dot(ksa_ref[g, 2 * j + 1], qa)
            online_update(s_even, vsa_ref[g, 2 * j], None, m_ref, acc_ref)
            s_even[...] = _dot(ksa_ref[g, 2 * j + 2], qa)
            online_update(s_odd, vsa_ref[g, 2 * j + 1], None, m_ref, acc_ref)
            return carry

        lax.fori_loop(0, kt_diag // 2, sel_pair, 0)

        @pl.when(kt_diag % 2 == 1)
        def _():
            s_odd[...] = _dot(ksa_ref[g, kt_diag], qa)
            online_update(s_even, vsa_ref[g, kt_diag - 1], None, m_ref, acc_ref)
            online_update(s_odd, vsa_ref[g, kt_diag], wb_ref[0], m_ref, acc_ref)

        @pl.when(kt_diag % 2 == 0)
        def _():
            online_update(s_even, vsa_ref[g, kt_diag], wb_ref[0], m_ref, acc_ref)

        o_s = normalized(acc_ref[...])

        outs = []
        for r in range(Q_PER_KV):
            h = Q_PER_KV * g + r
            cs = slice(r * tq, (r + 1) * tq)
            gc = gatest_ref[N_BRANCH * h:N_BRANCH * h + 1, :]
            gs = gatest_ref[N_BRANCH * h + 1:N_BRANCH * h + 2, :]
            gw = gatest_ref[N_BRANCH * h + 2:N_BRANCH * h + 3, :]
            outs.append(gc * o_c[:, cs] + gs * o_s[:, cs] + gw * o_w[:, cs])
        for pr in range(Q_PER_KV // 2):
            c0 = (Q_PER_KV * g + 2 * pr) * HEAD_DIM
            o_ref[:, c0:c0 + LANES] = jnp.concatenate(outs[2 * pr:2 * pr + 2], axis=0).T.astype(BF16)


def _nsa_attn_prompt(qt, gatest, kc, vct, ksa, vsa, kwa, vwa, batch, t_seq):
    d = qt.shape[1]
    tq = min(ATT_TILE, t_seq)
    nq = t_seq // tq
    n_cmp = t_seq // CMP_BLOCK
    tk = ksa.shape[3]
    featmaj = lambda rows: pl.BlockSpec((None, rows, tq), lambda b, i: (b, 0, i))
    seq = lambda shape: pl.BlockSpec((None,) + shape[1:], lambda b, i: (b, 0, 0, 0, 0))
    return pl.pallas_call(
        _nsa_attn_prompt_kernel,
        grid=(batch, nq),
        in_specs=[featmaj(d), featmaj(LANES),
                  pl.BlockSpec((None, n_cmp, KV_DIM), lambda b, i: (b, 0, 0)),
                  pl.BlockSpec((None, KV_DIM, n_cmp), lambda b, i: (b, 0, 0)),
                  seq(ksa.shape), seq(vsa.shape), seq(kwa.shape), seq(vwa.shape)],
        out_specs=pl.BlockSpec((tq, d), lambda b, i: (b * nq + i, 0)),
        out_shape=jax.ShapeDtypeStruct((batch * t_seq, d), BF16),
        scratch_shapes=[pltpu.VMEM((V_AUG_ROWS, Q_PER_KV * tq), F32), pltpu.VMEM((1, Q_PER_KV * tq), F32),
                        pltpu.VMEM((V_AUG_ROWS, Q_PER_KV * tq), F32), pltpu.VMEM((1, Q_PER_KV * tq), F32),
                        pltpu.VMEM((tk, Q_PER_KV * tq), F32), pltpu.VMEM((tk, Q_PER_KV * tq), F32),
                        pltpu.VMEM((-(-(WINDOW - 1) // tk) + 1, tk, Q_PER_KV * tq), F32),
                        pltpu.VMEM((n_cmp, Q_PER_KV * tq), F32), pltpu.VMEM((n_cmp, Q_PER_KV * tq), F32),
                        pltpu.VMEM((n_cmp // 2, tq), F32)],
        compiler_params=_cparams("parallel", "arbitrary"),
        name="nsa_attn_prompt",
    )(qt, gatest, kc, vct, ksa, vsa, kwa, vwa)


def _compress_sample_kernel(pt_ref, *refs, n_pages, page, t_new, n_cmp):
    del pt_ref
    pages = refs[:n_pages]
    new_ref, pe_ref, w1_ref, w2t_ref, kc_ref, vc_ref = refs[n_pages:n_pages + 6]
    bufs = refs[n_pages + 6:]
    n_pad = bufs[0].shape[1]
    n_past = n_pages * page // CMP_BLOCK
    grp = SUBLANES * CMP_BLOCK
    ppg = grp // page
    tt = lax.broadcasted_iota(jnp.int32, (grp, grp), 0)
    perm_t = jnp.where(lax.broadcasted_iota(jnp.int32, (grp, grp), 1)
                       == (tt % CMP_BLOCK) * SUBLANES + tt // CMP_BLOCK, 1.0, 0.0).astype(BF16)
    for c, buf_ref in enumerate(bufs):
        cs = slice(c * LANES, (c + 1) * LANES)
        for pp in range(n_pages // ppg):
            xt = jnp.concatenate([pages[pp * ppg + j][cs, :] for j in range(ppg)], axis=1)
            xp = _dot(xt.astype(BF16), perm_t).T
            for l in range(CMP_BLOCK):
                buf_ref[l, pp * SUBLANES:(pp + 1) * SUBLANES, :] = xp[l * SUBLANES:(l + 1) * SUBLANES, :]
        buf_ref[:, n_past:, :] = jnp.zeros((CMP_BLOCK, n_pad - n_past, LANES), F32)
        for l in range(t_new):
            buf_ref[l, n_past:n_past + 1, :] = new_ref[l:l + 1, cs]
    n_out = kc_ref.shape[0]
    for kv, out_ref in ((0, kc_ref), (1, vc_ref)):
        def load(l, hf, kv=kv):
            return bufs[2 * kv + hf][l]
        y = _compress_rows(load, pe_ref, w1_ref, w2t_ref, kv, n_pad)
        valid = lax.broadcasted_iota(jnp.int32, (n_pad, HEAD_DIM), 0) < n_cmp
        for g in range(KV_HEADS):
            out_ref[0:n_pad, g * HEAD_DIM:(g + 1) * HEAD_DIM] = jnp.where(
                valid, y[g * n_pad:(g + 1) * n_pad], 0.0).astype(BF16)
        out_ref[n_pad:, :] = jnp.zeros((n_out - n_pad, KV_DIM), BF16)


def _page_specs(layer, n_pages, page, feat_block):
    half = 2 * KV_DIM
    return [pl.BlockSpec((None, None, half, page),
                         functools.partial(lambda b, pt, p: (layer, pt[b, p], feat_block, 0), p=p))
            for p in range(n_pages)]


def _compress_sample(page_table, cache_t, layer, rows_new, pe2, w1, w2t, t_new):
    dec_b, n_pages = page_table.shape
    page = cache_t.shape[-1]
    past = n_pages * page
    l_pad = -(-(past + t_new) // SEL_BLOCK) * SEL_BLOCK
    n_cmp = l_pad // CMP_BLOCK
    n_pad = -(-n_cmp // SUBLANES) * SUBLANES
    n_out = LANES
    half = 2 * KV_DIM
    kern = functools.partial(_compress_sample_kernel, n_pages=n_pages, page=page, t_new=t_new, n_cmp=n_cmp)
    grid_spec = pltpu.PrefetchScalarGridSpec(
        num_scalar_prefetch=1,
        grid=(dec_b,),
        in_specs=_page_specs(layer, n_pages, page, 0) + [
            pl.BlockSpec((t_new, half), lambda b, pt: (b, 0)),
            pl.BlockSpec(pe2.shape, lambda b, pt: (0, 0, 0)),
            pl.BlockSpec(w1.shape, lambda b, pt: (0, 0, 0)),
            pl.BlockSpec(w2t.shape, lambda b, pt: (0, 0, 0))],
        out_specs=[pl.BlockSpec((None, n_out, KV_DIM), lambda b, pt: (b, 0, 0))] * 2,
        scratch_shapes=[pltpu.VMEM((CMP_BLOCK, n_pad, LANES), F32)] * (half // LANES),
    )
    return pl.pallas_call(
        kern,
        grid_spec=grid_spec,
        out_shape=[jax.ShapeDtypeStruct((dec_b, n_out, KV_DIM), BF16)] * 2,
        compiler_params=_cparams("arbitrary"),
        name="nsa_compress_sample",
    )(page_table, *([cache_t] * n_pages), rows_new, pe2, w1, w2t), n_cmp


def _nsa_attn_sample_kernel(pt_ref, *refs, n_pages, page, t_new, n_cmp):
    del pt_ref
    pages = refs[:n_pages]
    (kc_ref, vc_ref, q_ref, new_ref, wnew_ref, win_ref, gates_ref,
     o_ref, nwin_ref, kst_ref, vst_ref) = refs[n_pages:]
    past = n_pages * page
    n_buf = win_ref.shape[1]
    n_sel = n_cmp // 2
    gq = Q_PER_KV * t_new
    rq = KV_HEADS * gq
    nk = past + LANES
    nw = n_buf + LANES

    for p in range(n_pages):
        blk = pages[p][...]
        kst_ref[:, p * page:(p + 1) * page] = blk[:KV_DIM].astype(BF16)
        vst_ref[:, p * page:(p + 1) * page] = blk[KV_DIM:].astype(BF16)
    pad = jnp.zeros((LANES - t_new, KV_DIM), F32)
    new = new_ref[...]
    wnew = wnew_ref[...]
    knew = jnp.concatenate([new[:, :KV_DIM], pad], axis=0).astype(BF16)
    vnew = jnp.concatenate([new[:, KV_DIM:], pad], axis=0).astype(BF16)
    kwnew = jnp.concatenate([wnew[:, :KV_DIM], pad], axis=0).astype(BF16)
    vwnew = jnp.concatenate([wnew[:, KV_DIM:], pad], axis=0).astype(BF16)
    win = win_ref[...]
    kwt = win[:KV_DIM].astype(BF16)
    vwt = win[KV_DIM:].astype(BF16)

    rolled = pltpu.roll(win, n_buf - t_new, 1)
    wnew_t = jnp.concatenate([jnp.zeros((LANES - t_new, 2 * KV_DIM), F32), wnew], axis=0).T
    tail_lane = lax.broadcasted_iota(jnp.int32, (2 * KV_DIM, LANES), 1)
    nwin_ref[:, 0:n_buf - LANES] = rolled[:, 0:n_buf - LANES]
    nwin_ref[:, n_buf - LANES:] = jnp.where(tail_lane >= LANES - t_new, wnew_t, rolled[:, n_buf - LANES:])

    q = q_ref[...]
    lane_g = lax.broadcasted_iota(jnp.int32, (t_new, KV_DIM), 1) // HEAD_DIM
    pieces = []
    for g in range(KV_HEADS):
        for j in range(Q_PER_KV):
            qj = jnp.concatenate([q[:, (Q_PER_KV * gg + j) * HEAD_DIM:(Q_PER_KV * gg + j + 1) * HEAD_DIM]
                                  for gg in range(KV_HEADS)], axis=1)
            pieces.append(jnp.where(lane_g == g, qj, jnp.zeros_like(qj)))
    qbd = jnp.concatenate(pieces, axis=0)
    t_row = past + lax.broadcasted_iota(jnp.int32, (rq, 1), 0) % t_new

    def softmax(s, mask):
        s = jnp.where(mask, s, NEG)
        e = jnp.where(mask, jnp.exp(s - jnp.max(s, axis=1, keepdims=True)), 0.0)
        return e * (1.0 / jnp.maximum(jnp.sum(e, axis=1, keepdims=True), 1e-30))

    ncl = kc_ref.shape[0]
    sc = _dot_nt(qbd, kc_ref[...])
    cidx = lax.broadcasted_iota(jnp.int32, (rq, ncl), 1)
    pc = softmax(sc, (cidx < n_cmp) & ((cidx + 1) * CMP_BLOCK - 1 <= t_row))
    oc = _dot(pc.astype(BF16), vc_ref[...])

    groups = []
    for g in range(KV_HEADS):
        pg = pc[g * gq:g * gq + t_new]
        for j in range(1, Q_PER_KV):
            pg = pg + pc[g * gq + j * t_new:g * gq + (j + 1) * t_new]
        groups.append(pg)
    pg = jnp.concatenate(groups, axis=0)
    nr = KV_HEADS * t_new
    imp = pg + pltpu.roll(pg, ncl - 1, 1)
    lane = lax.broadcasted_iota(jnp.int32, (nr, ncl), 1)
    blk = lane // 2
    cur = (past + lax.broadcasted_iota(jnp.int32, (nr, ncl), 0) % t_new) // SEL_BLOCK
    imp = jnp.where(blk > cur, -1.0, imp)
    imp = jnp.where(blk == 0, FORCED_FIRST, imp)
    imp = jnp.where(blk == cur, FORCED_CUR, imp)
    rank = jnp.zeros((nr, ncl), jnp.int32)
    for bp in range(n_sel):
        other = imp[:, 2 * bp:2 * bp + 1]
        rank = rank + ((other > imp) | ((other == imp) & (bp < blk))).astype(jnp.int32)
    sel = jnp.where((rank < TOP_N) & (lane % 2 == 0) & (blk < n_sel), 1.0, 0.0)
    sel_rows = jnp.concatenate([sel[g * t_new:(g + 1) * t_new] for g in range(KV_HEADS) for _ in range(Q_PER_KV)],
                               axis=0).astype(BF16)
    expand = jnp.where(lax.broadcasted_iota(jnp.int32, (ncl, nk), 0)
                       == 2 * (lax.broadcasted_iota(jnp.int32, (ncl, nk), 1) // SEL_BLOCK), 1.0, 0.0).astype(BF16)
    selmask = _dot(sel_rows, expand) > 0.5

    kpos = lax.broadcasted_iota(jnp.int32, (rq, nk), 1)
    ss = jnp.concatenate([_dot(qbd, kst_ref[...]), _dot_nt(qbd, knew)], axis=1)
    ps = softmax(ss, selmask & (kpos <= t_row)).astype(BF16)
    os_ = _dot_nt(ps[:, :past], vst_ref[...]) + _dot(ps[:, past:], vnew)

    wpos = past - n_buf + lax.broadcasted_iota(jnp.int32, (rq, nw), 1)
    sw = jnp.concatenate([_dot(qbd, kwt), _dot_nt(qbd, kwnew)], axis=1)
    pw = softmax(sw, (wpos <= t_row) & (wpos > t_row - WINDOW)).astype(BF16)
    ow = _dot_nt(pw[:, :n_buf], vwt) + _dot(pw[:, n_buf:], vwnew)

    gates = gates_ref[...]
    for g in range(KV_HEADS):
        for j in range(Q_PER_KV):
            h = Q_PER_KV * g + j
            rs = slice(g * gq + j * t_new, g * gq + (j + 1) * t_new)
            cs = slice(g * HEAD_DIM, (g + 1) * HEAD_DIM)
            gc = gates[:, N_BRANCH * h:N_BRANCH * h + 1]
            gs = gates[:, N_BRANCH * h + 1:N_BRANCH * h + 2]
            gw = gates[:, N_BRANCH * h + 2:N_BRANCH * h + 3]
            o_ref[:, h * HEAD_DIM:(h + 1) * HEAD_DIM] = (
                gc * oc[rs, cs] + gs * os_[rs, cs] + gw * ow[rs, cs]).astype(BF16)


def _nsa_attn_sample(page_table, cache_t, win_t, layer, kc, vc, q, rows_new, wrows_new, gates, t_new, n_cmp):
    dec_b, n_pages = page_table.shape
    page = cache_t.shape[-1]
    past = n_pages * page
    n_buf = win_t.shape[-1]
    half = 2 * KV_DIM
    d = q.shape[1]
    ncl = kc.shape[1]
    kern = functools.partial(_nsa_attn_sample_kernel, n_pages=n_pages, page=page, t_new=t_new, n_cmp=n_cmp)
    grid_spec = pltpu.PrefetchScalarGridSpec(
        num_scalar_prefetch=1,
        grid=(dec_b,),
        in_specs=_page_specs(layer, n_pages, page, 1) + [
            pl.BlockSpec((None, ncl, KV_DIM), lambda b, pt: (b, 0, 0)),
            pl.BlockSpec((None, ncl, KV_DIM), lambda b, pt: (b, 0, 0)),
            pl.BlockSpec((t_new, d), lambda b, pt: (b, 0)),
            pl.BlockSpec((t_new, half), lambda b, pt: (b, 1)),
            pl.BlockSpec((t_new, half), lambda b, pt: (b, 0)),
            pl.BlockSpec((None, None, half, n_buf), lambda b, pt: (layer, b, 0, 0)),
            pl.BlockSpec((t_new, LANES), lambda b, pt: (b, 0))],
        out_specs=[pl.BlockSpec((t_new, d), lambda b, pt: (b, 0)),
                   pl.BlockSpec((None, half, n_buf), lambda b, pt: (b, 0, 0))],
        scratch_shapes=[pltpu.VMEM((KV_DIM, past), BF16), pltpu.VMEM((KV_DIM, past), BF16)],
    )
    return pl.pallas_call(
        kern,
        grid_spec=grid_spec,
        out_shape=[jax.ShapeDtypeStruct((dec_b * t_new, d), BF16), jax.ShapeDtypeStruct((dec_b, half, n_buf), F32)],
        compiler_params=_cparams("arbitrary"),
        name="nsa_attn_sample",
    )(page_table, *([cache_t] * n_pages), kc, vc, q, rows_new, wrows_new, win_t, gates)


def _post_mixer_kernel(x_ref, o_ref, wo_ref, gm_ref, w1_ref, w2_ref, p_ref, gp_ref, wg_ref, wp_ref, gf_ref,
                       y_ref, xn_ref, acc_ref, *, head_major, final_norm):
    j = pl.program_id(1)

    @pl.when(j == 0)
    def _():
        if head_major:
            o = jnp.concatenate([o_ref[h] for h in range(o_ref.shape[0])], axis=1)
        else:
            o = o_ref[...]
        x = x_ref[...] + _dot(o, wo_ref[...])
        xn_ref[...] = _rms(x, gm_ref[...]).astype(BF16)
        acc_ref[...] = x

    h = jnp.maximum(_dot(xn_ref[...], w1_ref[...]), 0.0)
    acc_ref[...] += _dot((h * h).astype(BF16), w2_ref[...])

    @pl.when(j == pl.num_programs(1) - 1)
    def _():
        x = acc_ref[...]
        gate = jax.nn.sigmoid(_dot(_rms(x, gp_ref[...]).astype(BF16), wg_ref[...]))
        y = x + _dot(p_ref[...].astype(BF16), wp_ref[...]) * gate
        if final_norm:
            y = _rms(y, gf_ref[...])
        y_ref[...] = y


def _post_mixer(x, o, w_out, g_mlp, w1, w2, p_all, layer, g_ple, wg, wp, gf, final_norm):
    m, d = x.shape
    ff = w1.shape[1]
    tm, tf = min(ROW_TILE, m), 1024
    head_major = o.ndim == 3
    row = pl.BlockSpec((tm, d), lambda i, j: (i, 0))
    vec = pl.BlockSpec((1, d), lambda i, j: (0, 0))
    full = lambda a: pl.BlockSpec(a.shape, lambda i, j: (0,) * a.ndim)
    o_spec = pl.BlockSpec((o.shape[0], tm, o.shape[2]), lambda i, j: (0, i, 0)) if head_major else row
    return pl.pallas_call(
        functools.partial(_post_mixer_kernel, head_major=head_major, final_norm=final_norm),
        grid=(m // tm, ff // tf),
        in_specs=[row, o_spec, full(w_out), vec,
                  pl.BlockSpec((d, tf), lambda i, j: (0, j)), pl.BlockSpec((tf, d), lambda i, j: (j, 0)),
                  pl.BlockSpec((None, tm, p_all.shape[-1]), lambda i, j: (layer, i, 0)), vec,
                  full(wg), full(wp), vec],
        out_specs=row,
        out_shape=jax.ShapeDtypeStruct((m, d), F32),
        scratch_shapes=[pltpu.VMEM((tm, d), BF16), pltpu.VMEM((tm, d), F32)],
        compiler_params=_cparams("parallel", "arbitrary"),
        name="post_mixer",
    )(x, o, w_out, g_mlp, w1, w2, p_all, g_ple, wg, wp, gf)


def _hg_inproj_kernel(x_ref, g_ref, lbl_ref, w_ref, q_ref, f_ref, k_ref, v_ref, gs_ref, *, layer):
    d = x_ref.shape[1]
    xn = _rms(x_ref[...], g_ref[...]).astype(BF16)

    def put(ref, val):
        for h in range(HG_HEADS):
            ref[h] = val[:, h * HG_DK:(h + 1) * HG_DK]

    lg = lbl_ref[...]
    e = jnp.exp(lg - jnp.max(lg, axis=0, keepdims=True))
    w = e / jnp.sum(e, axis=0, keepdims=True)
    cs = w[0:1]
    for r in range(1, layer + 1):
        cs = cs + w[r:r + 1]
    lb = cs - w[0:1]

    y = _dot(xn, w_ref[:, 0:d])
    put(q_ref, jax.nn.silu(y) * HG_DK ** -0.5)
    y = _dot(xn, w_ref[:, d:2 * d])
    sg = jax.nn.sigmoid(y)
    put(f_ref, jnp.log(lb + (1.0 - lb) * sg))
    put(k_ref, (1.0 - lb) * (1.0 - sg))
    put(v_ref, _dot(xn, w_ref[:, 2 * d:3 * d]))
    put(gs_ref, jax.nn.silu(_dot(xn, w_ref[:, 3 * d:4 * d])))


def _hg_inproj(x, g, lb_logits, w, layer):
    m, d = x.shape
    tm = min(ROW_TILE, m)
    head_major = pl.BlockSpec((HG_HEADS, tm, HG_DK), lambda i: (0, i, 0))
    return pl.pallas_call(
        functools.partial(_hg_inproj_kernel, layer=layer),
        grid=(m // tm,),
        in_specs=[pl.BlockSpec((tm, d), lambda i: (i, 0)), _full((1, d)), _full(lb_logits.shape), _full(w.shape)],
        out_specs=[head_major] * 5,
        out_shape=[jax.ShapeDtypeStruct((HG_HEADS, m, HG_DK), F32)] * 5,
        compiler_params=_cparams("parallel"),
        name="hgrn_inproj",
    )(x, g, lb_logits, w)


def _cumsum_rows(f):
    c = f.shape[0]
    row = lax.broadcasted_iota(jnp.int32, f.shape, 0)
    s = 1
    while s < c:
        f = f + jnp.where(row >= s, pltpu.roll(f, s, 0), 0.0)
        s *= 2
    return f


def _hg_intra(q, k, v, b, o, sub):
    c = q.shape[0]
    ones = jnp.ones((HG_DK, HG_DV), BF16)
    trow = lax.broadcasted_iota(jnp.int32, (sub, HG_DK), 0)
    terms = []
    for r0 in range(0, c, sub):
        qi, ki, bi = q[r0:r0 + sub], k[r0:r0 + sub], b[r0:r0 + sub]
        for s in range(sub):
            causal = trow >= s
            dec = jnp.exp(jnp.where(causal, bi - bi[s:s + 1], 0.0))
            terms.append(jnp.where(causal, qi * ki[s:s + 1] * dec, 0.0).astype(BF16))
    a = _dot(jnp.concatenate(terms, axis=0), ones)
    outs = []
    for r0 in range(0, c, sub):
        oi = o[r0:r0 + sub]
        if r0 > 0:
            bs = b[r0 - 1:r0]
            qp = (q[r0:r0 + sub] * jnp.exp(b[r0:r0 + sub] - bs)).astype(BF16)
            kp = (k[:r0] * jnp.exp(bs - b[:r0])).astype(BF16)
            oi = oi + _dot(_dot_nt(qp, kp).astype(BF16), v[:r0].astype(BF16))
        for s in range(sub):
            oi = oi + a[(r0 + s) * sub:(r0 + s + 1) * sub] * v[r0 + s:r0 + s + 1]
        outs.append(oi)
    return jnp.concatenate(outs, axis=0) if len(outs) > 1 else outs[0]


def _hg_out(o, ng, gs):
    return (_rms(o, ng) * gs).astype(BF16)


def _hg_diag_terms(q, k, b, sub):
    trow = lax.broadcasted_iota(jnp.int32, (sub, HG_DK), 0)
    terms = []
    for r0 in range(0, q.shape[0], sub):
        qi, ki, bi = q[r0:r0 + sub], k[r0:r0 + sub], b[r0:r0 + sub]
        for s in range(sub):
            causal = trow >= s
            dec = jnp.exp(jnp.where(causal, bi - bi[s:s + 1], 0.0))
            terms.append(jnp.where(causal, qi * ki[s:s + 1] * dec, 0.0).astype(BF16))
    return jnp.concatenate(terms, axis=0)


def _hg_offdiag(q, k, v, b, sub):
    c = q.shape[0]
    qps, kps, vps, spans = [], [], [], []
    off = 0
    for r0 in range(sub, c, sub):
        bs = b[r0 - 1:r0]
        qps.append(q[r0:r0 + sub] * jnp.exp(b[r0:r0 + sub] - bs))
        kps.append(k[:r0] * jnp.exp(bs - b[:r0]))
        vps.append(v[:r0])
        spans.append((off, off + r0))
        off += r0
    a = _dot_nt(jnp.concatenate(qps, axis=0).astype(BF16), jnp.concatenate(kps, axis=0).astype(BF16))
    row_blk = lax.broadcasted_iota(jnp.int32, a.shape, 0) // sub
    col = lax.broadcasted_iota(jnp.int32, a.shape, 1)
    keep = jnp.zeros(a.shape, jnp.bool_)
    for i, (lo, hi) in enumerate(spans):
        keep = keep | ((row_blk == i) & (col >= lo) & (col < hi))
    return _dot(jnp.where(keep, a, 0.0).astype(BF16), jnp.concatenate(vps, axis=0).astype(BF16))


def _hg_scan_prompt_kernel(q_ref, f_ref, k_ref, v_ref, gs_ref, ng_ref, o_ref, s_ref,
                           st_ref, b_ref, t_ref, a_ref, i_ref):
    c = pl.program_id(1)
    rows = q_ref.shape[1]
    n_chunk = rows // HG_CHUNK
    tr = HG_CHUNK * HG_SUB
    ones = jnp.ones((HG_DK, HG_DV), BF16)

    @pl.when(c == 0)
    def _():
        st_ref[...] = jnp.zeros(st_ref.shape, F32)

    def heads(it, carry):
        for hh in range(HG_UNROLL_PROMPT):
            h = it * HG_UNROLL_PROMPT + hh
            for ci in range(n_chunk):
                u = hh * n_chunk + ci
                rs = pl.ds(ci * HG_CHUNK, HG_CHUNK)
                b = _cumsum_rows(f_ref[h, rs, :])
                b_ref[u] = b
                t_ref[u * tr:(u + 1) * tr, :] = _hg_diag_terms(q_ref[h, rs, :], k_ref[h, rs, :], b, HG_SUB)
        a_ref[...] = _dot(t_ref[...], ones)
        for hh in range(HG_UNROLL_PROMPT):
            h = it * HG_UNROLL_PROMPT + hh
            for ci in range(n_chunk):
                u = hh * n_chunk + ci
                rs = pl.ds(ci * HG_CHUNK, HG_CHUNK)
                q, k, v, b = q_ref[h, rs, :], k_ref[h, rs, :], v_ref[h, rs, :], b_ref[u]
                off = _hg_offdiag(q, k, v, b, HG_SUB)
                for r0 in range(0, HG_CHUNK, HG_SUB):
                    oi = off[r0 - HG_SUB:r0] if r0 > 0 else jnp.zeros((HG_SUB, HG_DV), F32)
                    for s in range(HG_SUB):
                        a0 = u * tr + (r0 + s) * HG_SUB
                        oi = oi + a_ref[a0:a0 + HG_SUB, :] * v[r0 + s:r0 + s + 1]
                    i_ref[u, r0:r0 + HG_SUB, :] = oi
        for ci in range(n_chunk):
            rs = pl.ds(ci * HG_CHUNK, HG_CHUNK)
            for hh in range(HG_UNROLL_PROMPT):
                h = it * HG_UNROLL_PROMPT + hh
                u = hh * n_chunk + ci
                q, k, v, b = q_ref[h, rs, :], k_ref[h, rs, :], v_ref[h, rs, :], b_ref[u]
                st = st_ref[h]
                o = i_ref[u] + _dot_nt((q * jnp.exp(b)).astype(BF16), st.astype(BF16))
                o_ref[h, rs, :] = _hg_out(o, ng_ref[...], gs_ref[h, rs, :])
                bl = b[HG_CHUNK - 1:HG_CHUNK]
                kd = (k * jnp.exp(bl - b)).astype(BF16)
                st_ref[h] = st * jnp.exp(bl) + lax.dot_general(v.astype(BF16), kd, (((0,), (0,)), ((), ())),
                                                               preferred_element_type=F32)

        @pl.when(c == pl.num_programs(1) - 1)
        def _():
            for hh in range(HG_UNROLL_PROMPT):
                h = it * HG_UNROLL_PROMPT + hh
                s_ref[h] = st_ref[h].T
        return carry

    lax.fori_loop(0, HG_HEADS // HG_UNROLL_PROMPT, heads, 0)


def _hg_scan_prompt(q, f, k, v, gs, ng, batch, t_seq):
    m = q.shape[1]
    rows = 2 * HG_CHUNK
    nc = t_seq // rows
    units = HG_UNROLL_PROMPT * (rows // HG_CHUNK)
    blk = pl.BlockSpec((HG_HEADS, rows, HG_DK), lambda b, c: (0, b * nc + c, 0))
    return pl.pallas_call(
        _hg_scan_prompt_kernel,
        grid=(batch, nc),
        in_specs=[blk] * 5 + [pl.BlockSpec((1, HG_DV), lambda b, c: (0, 0))],
        out_specs=[blk, pl.BlockSpec((None, HG_HEADS, HG_DK, HG_DV), lambda b, c: (b, 0, 0, 0))],
        out_shape=[jax.ShapeDtypeStruct((HG_HEADS, m, HG_DV), BF16),
                   jax.ShapeDtypeStruct((batch, HG_HEADS, HG_DK, HG_DV), F32)],
        scratch_shapes=[pltpu.VMEM((HG_HEADS, HG_DV, HG_DK), F32),
                        pltpu.VMEM((units, HG_CHUNK, HG_DK), F32),
                        pltpu.VMEM((units * HG_CHUNK * HG_SUB, HG_DK), BF16),
                        pltpu.VMEM((units * HG_CHUNK * HG_SUB, HG_DV), F32),
                        pltpu.VMEM((units, HG_CHUNK, HG_DV), F32)],
        compiler_params=_cparams("parallel", "arbitrary"),
        name="hgrn_scan_prompt",
    )(q, f, k, v, gs, ng)


def _hg_step_sample_kernel(q_ref, f_ref, k_ref, v_ref, gs_ref, ng_ref, s0_ref, o_ref, s1_ref, *, t_new):
    n_seq = s0_ref.shape[0]

    def one(sq, h):
        rs = pl.ds(pl.multiple_of(sq * t_new, t_new), t_new)
        q, k, v = q_ref[h, rs, :], k_ref[h, rs, :], v_ref[h, rs, :]
        b = _cumsum_rows(f_ref[h, rs, :])
        s0 = s0_ref[sq, h]
        o = _dot((q * jnp.exp(b)).astype(BF16), s0.astype(BF16))
        o = _hg_intra(q, k, v, b, o, t_new)
        o_ref[h, rs, :] = _hg_out(o, ng_ref[...], gs_ref[h, rs, :])
        bl = b[t_new - 1:t_new]
        kd = k * jnp.exp(bl - b)
        ext = jnp.concatenate([kd, jnp.broadcast_to(jnp.exp(bl), (t_new, HG_DK)),
                               jnp.zeros((HG_DK - 2 * t_new, HG_DK), F32)], axis=0).T
        s1_ref[sq, h] = s0 * ext[:, t_new:t_new + 1] + _dot(ext[:, :t_new].astype(BF16), v.astype(BF16))

    def body(idx, carry):
        per_seq = HG_HEADS // HG_UNROLL_SAMPLE
        for hh in range(HG_UNROLL_SAMPLE):
            one(idx // per_seq, (idx % per_seq) * HG_UNROLL_SAMPLE + hh)
        return carry

    lax.fori_loop(0, n_seq * (HG_HEADS // HG_UNROLL_SAMPLE), body, 0)


def _hg_step_sample(q, f, k, v, gs, ng, s0_all, layer, t_new):
    m = q.shape[1]
    dec_b = s0_all.shape[1]
    sb = min(8, dec_b)
    blk = pl.BlockSpec((HG_HEADS, sb * t_new, HG_DK), lambda i: (0, i, 0))
    st_in = pl.BlockSpec((None, sb, HG_HEADS, HG_DK, HG_DV), lambda i: (layer, i, 0, 0, 0))
    st_out = pl.BlockSpec((sb, HG_HEADS, HG_DK, HG_DV), lambda i: (i, 0, 0, 0))
    return pl.pallas_call(
        functools.partial(_hg_step_sample_kernel, t_new=t_new),
        grid=(dec_b // sb,),
        in_specs=[blk] * 5 + [pl.BlockSpec((1, HG_DV), lambda i: (0, 0)), st_in],
        out_specs=[blk, st_out],
        out_shape=[jax.ShapeDtypeStruct((HG_HEADS, m, HG_DV), BF16), jax.ShapeDtypeStruct(s0_all.shape[1:], F32)],
        compiler_params=_cparams("parallel"),
        name="hgrn_step_sample",
    )(q, f, k, v, gs, ng, s0_all)


def _feature_major_view(a):
    lead = a.shape[:-4]
    t, c, g, d = a.shape[-4:]
    n = len(lead)
    return jnp.transpose(a, tuple(range(n)) + (n + 1, n + 2, n + 3, n)).reshape(lead + (c * g * d, t))


def _token_major_view(a, c):
    lead = a.shape[:-2]
    t = a.shape[-1]
    n = len(lead)
    a = a.reshape(lead + (c, KV_HEADS, HEAD_DIM, t))
    return jnp.transpose(a, tuple(range(n)) + (n + 3, n, n + 1, n + 2))


def kernel(x_prompt, x_sample, cache_nsa_kv, state_nsa_win, state_hgrn, page_table, p_prompt, p_sample, norm_mix, norm_mlp, norm_ple, norm_final, nsa_w_in, nsa_cmp_pe, nsa_cmp_w1, nsa_cmp_w2, nsa_w_out, hg_w_in, hg_lb_logits, hg_norm, hg_w_out, mlp_w1, mlp_w2, ple_w_proj, ple_w_gate):
    batch, t_p, d = x_prompt.shape
    dec_b, t_s, _ = x_sample.shape
    depth = p_prompt.shape[0]
    page = cache_nsa_kv.shape[2]
    past = page_table.shape[1] * page

    xp = x_prompt.reshape(batch * t_p, d)
    xs = x_sample.reshape(dec_b * t_s, d)
    pp = p_prompt.reshape(depth, batch * t_p, -1)
    ps = p_sample.reshape(depth, dec_b * t_s, -1)
    row = lambda a: a.reshape(1, -1)
    cache_t = _feature_major_view(cache_nsa_kv)
    win_t = _feature_major_view(state_nsa_win)
    wl = min(WINDOW, t_p)

    tabs_p = _rope_tables(t_p, 0, t_p)
    tabs_s = _rope_tables(min(ROW_TILE, dec_b * t_s), past, t_s)

    kv_p, kv_s, win_p, win_s, st_p, st_s = [], [], [], [], [], []
    for i in range(depth):
        g_mix = row(norm_mix[i])
        if i % 2 == 0:
            a = i // 2
            n_in = nsa_w_in.shape[2]
            w_t = jnp.pad(jnp.transpose(nsa_w_in[a]).astype(BF16),
                          ((0, Q_DIM + 6 * KV_DIM + LANES - n_in), (0, 0)))
            pe2 = jnp.tile(nsa_cmp_pe[a], (1, 1, LANES // HEAD_DIM))
            cw1 = nsa_cmp_w1[a].reshape(2, CMP_BLOCK * HEAD_DIM, -1).astype(BF16)
            cw2t = jnp.swapaxes(nsa_cmp_w2[a], 1, 2).astype(BF16)
            w_out = nsa_w_out[a].astype(BF16)

            qt, gatest, cmp_rows, rows_t, wrows_t, ksa, vsa, kwa, vwa = _nsa_inproj_prompt(
                xp, g_mix, tabs_p, w_t, batch, t_p)
            kc, vct = _compress_prompt(cmp_rows, pe2, cw1, cw2t, batch, t_p)
            op = _nsa_attn_prompt(qt, gatest, kc, vct, ksa, vsa, kwa, vwa, batch, t_p)
            kv_p.append(rows_t)
            win_p.append(wrows_t[:, :, t_p - wl:])

            q, gates, rows, wrows = _nsa_inproj_sample(xs, g_mix, tabs_s, w_t)
            (kc, vc), n_cmp = _compress_sample(page_table, cache_t, a, rows, pe2, cw1, cw2t, t_s)
            os_, new_win_t = _nsa_attn_sample(page_table, cache_t, win_t, a, kc, vc, q, rows, wrows, gates, t_s, n_cmp)
            kv_s.append(rows.reshape(dec_b, t_s, 4, KV_HEADS, HEAD_DIM))
            win_s.append(new_win_t)
        else:
            r = i // 2
            w_in = hg_w_in[r].astype(BF16)
            w_out = hg_w_out[r].astype(BF16)
            ng = row(hg_norm[r])
            q, f, k, v, gs = _hg_inproj(xp, g_mix, hg_lb_logits, w_in, r)
            op, s_new = _hg_scan_prompt(q, f, k, v, gs, ng, batch, t_p)
            st_p.append(s_new)
            q, f, k, v, gs = _hg_inproj(xs, g_mix, hg_lb_logits, w_in, r)
            os_, s_new = _hg_step_sample(q, f, k, v, gs, ng, state_hgrn, r, t_s)
            st_s.append(s_new)

        w1 = mlp_w1[i].astype(BF16)
        w2 = mlp_w2[i].astype(BF16)
        wgate = ple_w_gate[i].astype(BF16)
        wproj = ple_w_proj[i].astype(BF16)
        last = i == depth - 1
        post = (w_out, row(norm_mlp[i]), w1, w2)
        ple = (i, row(norm_ple[i]), wgate, wproj, row(norm_final), last)
        xp = _post_mixer(xp, op, *post, pp, *ple)
        xs = _post_mixer(xs, os_, *post, ps, *ple)

    return (xp.reshape(batch, t_p, d), xs.reshape(dec_b, t_s, d),
            _token_major_view(jnp.stack(kv_p), 4), jnp.stack(kv_s),
            _token_major_view(jnp.stack(win_p), 2), _token_major_view(jnp.stack(win_s), 2),
            jnp.stack(st_p), jnp.stack(st_s))
```

```python
import functools

import jax
import jax.numpy as jnp
from jax import lax
from jax.experimental import pallas as pl
from jax.experimental.pallas import tpu as pltpu

F32 = jnp.float32
BF16 = jnp.bfloat16

RMS_EPS = 1e-6
ROPE_THETA = 10000.0
N_HEADS = 16
HEAD_DIM = 64
KV_HEADS = 4
Q_PER_KV = N_HEADS // KV_HEADS
Q_DIM = N_HEADS * HEAD_DIM
KV_DIM = KV_HEADS * HEAD_DIM
N_BRANCH = 3
CMP_BLOCK = 32
SEL_BLOCK = 64
TOP_N = 16
WINDOW = 512
FORCED_CUR = 3e4
FORCED_FIRST = 2e4
HG_HEADS = 8
HG_DK = 128
HG_DV = 128
HG_CHUNK = 64
HG_SUB = 16
HG_MAX_FACTOR_EXP = 60.0
HG_UNROLL_PROMPT = 4
HG_UNROLL_SAMPLE = 8
NEG = -1e30
LOG2_E = 1.4426950408889634

LANES = 128
SUBLANES = 8
ROW_TILE = 512
ATT_TILE = 256
V_AUG_ROWS = HEAD_DIM + 2 * SUBLANES
VMEM_LIMIT = 56 * 1024 * 1024


def _cparams(*sem):
    return pltpu.CompilerParams(dimension_semantics=sem, vmem_limit_bytes=VMEM_LIMIT)


def _full(shape):
    return pl.BlockSpec(shape, lambda *_: (0,) * len(shape))


def _rms(x, g):
    return x * lax.rsqrt(jnp.mean(x * x, axis=-1, keepdims=True) + RMS_EPS) * g


def _dot(a, b):
    return jnp.dot(a, b, preferred_element_type=F32)


def _dot_nt(a, b):
    return lax.dot_general(a, b, (((1,), (1,)), ((), ())), preferred_element_type=F32)


def _rope_table_kernel(inv_ref, invc_ref, cos_ref, sin_ref, cost_ref, sint_ref, *, pos0, period):
    i = pl.program_id(0)
    tm = cos_ref.shape[0]
    half = HEAD_DIM // 2
    row = i * tm + lax.broadcasted_iota(jnp.int32, (tm, LANES), 0)
    ang = (pos0 + row % period).astype(F32) * inv_ref[...]
    lane = lax.broadcasted_iota(jnp.int32, (tm, LANES), 1)
    cos_ref[...] = jnp.cos(ang)
    s = jnp.sin(ang)
    sin_ref[...] = jnp.where(lane % HEAD_DIM < half, -s, s)
    col = i * tm + lax.broadcasted_iota(jnp.int32, (HEAD_DIM, tm), 1)
    ang_t = (pos0 + col % period).astype(F32) * jnp.concatenate([invc_ref[...]] * (tm // LANES), axis=1)
    feat = lax.broadcasted_iota(jnp.int32, (HEAD_DIM, tm), 0)
    cost_ref[...] = jnp.cos(ang_t)
    st = jnp.sin(ang_t)
    sint_ref[...] = jnp.where(feat < half, -st, st)


def _rope_tables(rows, pos0, period):
    half = HEAD_DIM // 2
    inv = ROPE_THETA ** (-jnp.arange(half, dtype=F32) / half)
    inv_row = jnp.tile(inv, LANES // half)[None, :]
    inv_col = jnp.broadcast_to(jnp.tile(inv, HEAD_DIM // half)[:, None], (HEAD_DIM, LANES))
    kern = functools.partial(_rope_table_kernel, pos0=pos0, period=period)
    tm = min(ROW_TILE, rows)
    tok = pl.BlockSpec((tm, LANES), lambda i: (i, 0))
    feat = pl.BlockSpec((HEAD_DIM, tm), lambda i: (0, i))
    return pl.pallas_call(
        kern,
        grid=(rows // tm,),
        in_specs=[_full((1, LANES)), _full((HEAD_DIM, LANES))],
        out_specs=[tok, tok, feat, feat],
        out_shape=[jax.ShapeDtypeStruct((rows, LANES), F32)] * 2 + [jax.ShapeDtypeStruct((HEAD_DIM, rows), F32)] * 2,
        compiler_params=_cparams("parallel"),
        name="rope_tables",
    )(inv_row, inv_col)


def _rope_tok(y, cos, sin):
    lane = lax.broadcasted_iota(jnp.int32, y.shape, 1)
    first = lane % HEAD_DIM < HEAD_DIM // 2
    sw = jnp.where(first, pltpu.roll(y, LANES - HEAD_DIM // 2, 1), pltpu.roll(y, HEAD_DIM // 2, 1))
    return y * cos + sw * sin


def _rope_feat(y, cost, sint):
    return y * cost + pltpu.roll(y, HEAD_DIM // 2, 0) * sint


def _nsa_inproj_prompt_kernel(x_ref, g_ref, cos_ref, sin_ref, cost_ref, sint_ref, w_ref,
                              qt_ref, gatest_ref, cmp_ref, rowst_ref, wrowst_ref,
                              ksa_ref, vsa_ref, kwa_ref, vwa_ref):
    tm = x_ref.shape[0]
    tk = ksa_ref.shape[2]
    i = pl.program_id(1)
    xn = _rms(x_ref[...], g_ref[...]).astype(BF16)
    cos, sin = cos_ref[...], sin_ref[...]
    cost, sint = cost_ref[...], sint_ref[...]
    scale = HEAD_DIM ** -0.5 * LOG2_E

    qt = _dot_nt(w_ref[0:Q_DIM, :], xn)
    for h in range(N_HEADS):
        rs = slice(h * HEAD_DIM, (h + 1) * HEAD_DIM)
        qt_ref[rs, :] = (_rope_feat(qt[rs], cost, sint) * scale).astype(BF16)
    g0 = Q_DIM + 6 * KV_DIM
    gatest_ref[...] = jax.nn.sigmoid(_dot_nt(w_ref[g0:g0 + LANES, :], xn))

    ycmp = _dot_nt(xn, w_ref[Q_DIM:Q_DIM + 2 * KV_DIM, :])
    for c in range(2 * KV_DIM // LANES):
        cs = slice(c * LANES, (c + 1) * LANES)
        cmp_ref[:, cs] = _rope_tok(ycmp[:, cs], cos, sin) if c * LANES < KV_DIM else ycmp[:, cs]

    lane = lax.broadcasted_iota(jnp.int32, (tm, LANES), 1)
    lo = lane < HEAD_DIM
    pos = i * tm + lax.broadcasted_iota(jnp.int32, (tm, LANES), 0)
    sel_bias = jnp.where(lane - HEAD_DIM == pos // SEL_BLOCK, NEG, 0.0)
    for part, ref, fill in ((2, ksa_ref, sel_bias), (4, kwa_ref, 0.0)):
        yk = _dot_nt(xn, w_ref[Q_DIM + part * KV_DIM:Q_DIM + (part + 1) * KV_DIM, :])
        for c in range(KV_DIM // LANES):
            y = _rope_tok(yk[:, c * LANES:(c + 1) * LANES], cos, sin)
            pair = (jnp.where(lo, y, fill).astype(BF16), jnp.where(lo, pltpu.roll(y, HEAD_DIM, 1), fill).astype(BF16))
            for gg in range(2):
                for j in range(tm // tk):
                    ref[2 * c + gg, j] = pair[gg][j * tk:(j + 1) * tk, :]

    yt = _dot_nt(w_ref[Q_DIM:Q_DIM + 6 * KV_DIM, :], xn)
    ones = jnp.ones((V_AUG_ROWS - HEAD_DIM, tm), F32)
    for part in range(6):
        for g in range(KV_HEADS):
            r0 = part * KV_DIM + g * HEAD_DIM
            y = yt[r0:r0 + HEAD_DIM, :]
            if part % 2 == 0:
                y = _rope_feat(y, cost, sint)
            if part < 4:
                rowst_ref[r0:r0 + HEAD_DIM, :] = y
            else:
                wrowst_ref[r0 - 4 * KV_DIM:r0 - 4 * KV_DIM + HEAD_DIM, :] = y
            if part in (3, 5):
                ref = vsa_ref if part == 3 else vwa_ref
                ya = jnp.concatenate([y, ones], axis=0).astype(BF16)
                for j in range(tm // tk):
                    ref[g, j] = ya[:, j * tk:(j + 1) * tk]


def _nsa_inproj_prompt(x, g, tabs, w, batch, t_seq):
    m, d = x.shape
    tm = min(ROW_TILE, t_seq)
    tk = min(ATT_TILE, t_seq)
    nt = t_seq // tm
    cos, sin, cost, sint = tabs
    row = lambda cols: pl.BlockSpec((tm, cols), lambda b, i: (b * nt + i, 0))
    tok_tab = pl.BlockSpec((tm, LANES), lambda b, i: (i, 0))
    feat_tab = pl.BlockSpec((HEAD_DIM, tm), lambda b, i: (0, i))
    featmaj = lambda rows: pl.BlockSpec((None, rows, tm), lambda b, i: (b, 0, i))
    ktiles = pl.BlockSpec((None, KV_HEADS, tm // tk, tk, LANES), lambda b, i: (b, 0, i, 0, 0))
    vtiles = pl.BlockSpec((None, KV_HEADS, tm // tk, V_AUG_ROWS, tk), lambda b, i: (b, 0, i, 0, 0))
    kshape = jax.ShapeDtypeStruct((batch, KV_HEADS, t_seq // tk, tk, LANES), BF16)
    vshape = jax.ShapeDtypeStruct((batch, KV_HEADS, t_seq // tk, V_AUG_ROWS, tk), BF16)
    return pl.pallas_call(
        _nsa_inproj_prompt_kernel,
        grid=(batch, nt),
        in_specs=[row(d), _full((1, d)), tok_tab, tok_tab, feat_tab, feat_tab, _full(w.shape)],
        out_specs=[featmaj(Q_DIM), featmaj(LANES), row(2 * KV_DIM), featmaj(4 * KV_DIM), featmaj(2 * KV_DIM),
                   ktiles, vtiles, ktiles, vtiles],
        out_shape=[jax.ShapeDtypeStruct((batch, Q_DIM, t_seq), BF16), jax.ShapeDtypeStruct((batch, LANES, t_seq), F32),
                   jax.ShapeDtypeStruct((m, 2 * KV_DIM), F32),
                   jax.ShapeDtypeStruct((batch, 4 * KV_DIM, t_seq), F32),
                   jax.ShapeDtypeStruct((batch, 2 * KV_DIM, t_seq), F32), kshape, vshape, kshape, vshape],
        compiler_params=_cparams("parallel", "parallel"),
        name="nsa_inproj_prompt",
    )(x, g, cos, sin, cost, sint, w)


def _nsa_inproj_sample_kernel(x_ref, g_ref, cos_ref, sin_ref, w_ref, q_ref, gates_ref, rows_ref, wrows_ref):
    xn = _rms(x_ref[...], g_ref[...]).astype(BF16)
    cos, sin = cos_ref[...], sin_ref[...]
    scale = HEAD_DIM ** -0.5
    yq = _dot_nt(xn, w_ref[0:Q_DIM, :])
    for c in range(Q_DIM // LANES):
        cs = slice(c * LANES, (c + 1) * LANES)
        q_ref[:, cs] = (_rope_tok(yq[:, cs], cos, sin) * scale).astype(BF16)
    g0 = Q_DIM + 6 * KV_DIM
    gates_ref[...] = jax.nn.sigmoid(_dot_nt(xn, w_ref[g0:g0 + LANES, :]))
    ykv = _dot_nt(xn, w_ref[Q_DIM:g0, :])
    for c in range(6 * KV_DIM // LANES):
        cs = slice(c * LANES, (c + 1) * LANES)
        part = c * LANES // KV_DIM
        y = _rope_tok(ykv[:, cs], cos, sin) if part % 2 == 0 else ykv[:, cs]
        if part < 4:
            rows_ref[:, cs] = y
        else:
            wrows_ref[:, c * LANES - 4 * KV_DIM:(c + 1) * LANES - 4 * KV_DIM] = y


def _nsa_inproj_sample(x, g, tabs, w):
    m, d = x.shape
    tm = min(ROW_TILE, m)
    cos, sin = tabs[0], tabs[1]
    n_tab = cos.shape[0] // tm
    row = lambda cols: pl.BlockSpec((tm, cols), lambda i: (i, 0))
    tab = pl.BlockSpec((tm, LANES), lambda i: (i % n_tab, 0))
    return pl.pallas_call(
        _nsa_inproj_sample_kernel,
        grid=(m // tm,),
        in_specs=[row(d), _full((1, d)), tab, tab, _full(w.shape)],
        out_specs=[row(d), row(LANES), row(4 * KV_DIM), row(2 * KV_DIM)],
        out_shape=[jax.ShapeDtypeStruct((m, d), BF16), jax.ShapeDtypeStruct((m, LANES), F32),
                   jax.ShapeDtypeStruct((m, 4 * KV_DIM), F32), jax.ShapeDtypeStruct((m, 2 * KV_DIM), F32)],
        compiler_params=_cparams("parallel"),
        name="nsa_inproj_sample",
    )(x, g, cos, sin, w)


def _compress_rows(load, pe_ref, w1_ref, w2t_ref, kv, n, transposed=False):
    lane = lax.broadcasted_iota(jnp.int32, (n, LANES), 1)
    lo = lane < HEAD_DIM
    acc = jnp.zeros((KV_HEADS * n, w1_ref.shape[-1]), F32)
    lq = 4
    for l0 in range(0, CMP_BLOCK, lq):
        halves = [[], []]
        for l in range(l0, l0 + lq, 2):
            for hf in range(2):
                a = load(l, hf) + pe_ref[kv, l:l + 1, :]
                b = load(l + 1, hf) + pe_ref[kv, l + 1:l + 2, :]
                even = jnp.where(lo, a, pltpu.roll(b, HEAD_DIM, 1))
                odd = jnp.where(lo, pltpu.roll(a, HEAD_DIM, 1), b)
                halves[hf].append((even, odd))
        groups = []
        for g in range(KV_HEADS):
            hf, par = divmod(g, 2)
            groups.append(jnp.concatenate([pc[par] for pc in halves[hf]], axis=1))
        xg = jnp.concatenate(groups, axis=0).astype(BF16)
        acc = acc + _dot(xg, w1_ref[kv, l0 * HEAD_DIM:(l0 + lq) * HEAD_DIM, :])
    h = jax.nn.gelu(acc).astype(BF16)
    return _dot_nt(w2t_ref[kv], h) if transposed else _dot_nt(h, w2t_ref[kv])


def _compress_prompt_kernel(s0_ref, s1_ref, s2_ref, s3_ref, pe_ref, w1_ref, w2t_ref, kc_ref, vct_ref):
    n = kc_ref.shape[0]
    hn = n // 2
    srcs = ((s0_ref, s1_ref), (s2_ref, s3_ref))

    def loader(kv):
        def load(l, hf):
            ev = srcs[kv][hf][pl.ds(l, hn, stride=2 * CMP_BLOCK), :]
            od = srcs[kv][hf][pl.ds(CMP_BLOCK + l, hn, stride=2 * CMP_BLOCK), :]
            return jnp.concatenate([ev, od], axis=0)
        return load

    y = _compress_rows(loader(0), pe_ref, w1_ref, w2t_ref, 0, n)
    yt = _compress_rows(loader(1), pe_ref, w1_ref, w2t_ref, 1, n, transposed=True)
    for g in range(KV_HEADS):
        kc_ref[:, g * HEAD_DIM:(g + 1) * HEAD_DIM] = y[g * n:(g + 1) * n].astype(BF16)
        vct_ref[g * HEAD_DIM:(g + 1) * HEAD_DIM, :] = yt[:, g * n:(g + 1) * n].astype(BF16)


def _compress_prompt(cmp_rows, pe2, w1, w2t, batch, t_seq):
    n = t_seq // CMP_BLOCK
    chunk = lambda c: pl.BlockSpec((t_seq, LANES), lambda b: (b, c))
    return pl.pallas_call(
        _compress_prompt_kernel,
        grid=(batch,),
        in_specs=[chunk(0), chunk(1), chunk(2), chunk(3), _full(pe2.shape), _full(w1.shape), _full(w2t.shape)],
        out_specs=[pl.BlockSpec((None, n, KV_DIM), lambda b: (b, 0, 0)),
                   pl.BlockSpec((None, KV_DIM, n), lambda b: (b, 0, 0))],
        out_shape=[jax.ShapeDtypeStruct((batch, n, KV_DIM), BF16), jax.ShapeDtypeStruct((batch, KV_DIM, n), BF16)],
        compiler_params=_cparams("parallel"),
        name="nsa_compress_prompt",
    )(cmp_rows, cmp_rows, cmp_rows, cmp_rows, pe2, w1, w2t)


def _topk_not_selected(imp_t, n_live, rank_ref):
    n_sel, nq = imp_t.shape
    slabs = [imp_t[r0:r0 + SUBLANES] for r0 in range(0, n_sel, SUBLANES)]
    sub = lax.broadcasted_iota(jnp.int32, (SUBLANES, nq), 0)
    rank_ref[...] = jnp.zeros((n_sel, nq), F32)
    for c0 in range(0, n_sel, SUBLANES):
        @pl.when(c0 < n_live)
        def _():
            for v, slab in enumerate(slabs):
                r0 = v * SUBLANES
                cnt = jnp.zeros((SUBLANES, nq), F32)
                for bp in range(c0, c0 + SUBLANES):
                    other = imp_t[bp:bp + 1, :]
                    if bp < r0:
                        ahead = other >= slab
                    elif bp >= r0 + SUBLANES - 1:
                        ahead = other > slab
                    else:
                        ahead = (other > slab) | ((other == slab) & (bp - r0 < sub))
                    cnt = cnt + jnp.where(ahead, 1.0, 0.0)
                rank_ref[r0:r0 + SUBLANES, :] += cnt
    return jnp.where(rank_ref[...] < TOP_N, 0.0, 1.0)


def _nsa_attn_prompt_kernel(qt_ref, gatest_ref, kc_ref, vct_ref, ksa_ref, vsa_ref, kwa_ref, vwa_ref, o_ref,
                            acc_ref, m_ref, accw_ref, mw_ref, sa_ref, sb_ref, wb_ref, cb_ref, cv_ref, rank_ref):
    tq = qt_ref.shape[1]
    tk = ksa_ref.shape[2]
    n_cmp = kc_ref.shape[0]
    n_sel = n_cmp // 2
    rq = Q_PER_KV * tq
    i = pl.program_id(1)
    t0 = i * tq
    t_col = t0 + lax.broadcasted_iota(jnp.int32, (1, rq), 1) % tq
    key_row = lax.broadcasted_iota(jnp.int32, (tk, rq), 0)
    n_wt = -(-(WINDOW - 1) // tk) + 1
    kt_diag = t0 // tk
    wkt = [kt_diag - j for j in range(n_wt)]

    for j, kt in enumerate(wkt):
        kpos = kt * tk + key_row
        wb_ref[j] = jnp.where((kpos >= 0) & (kpos <= t_col) & (kpos > t_col - WINDOW), 0.0, NEG)
    nrow = lax.broadcasted_iota(jnp.int32, (n_cmp, rq), 0)
    cblk = jnp.where(nrow < n_sel, 2 * nrow, 2 * (nrow - n_sel) + 1)
    cvis = (cblk + 1) * CMP_BLOCK - 1 <= t_col
    cb_ref[...] = jnp.where(cvis, 0.0, NEG)
    cv_ref[...] = jnp.where(cvis, 1.0, 0.0)

    def online_update(s_ref, vt, bias, m_ref, acc_ref):
        s = s_ref[...]
        if bias is not None:
            s = s + bias
        m_old = m_ref[...]
        m_new = jnp.maximum(m_old, jnp.max(s, axis=0, keepdims=True))
        p = jnp.exp2(s - m_new)
        acc_ref[...] = acc_ref[...] * jnp.exp2(m_old - m_new) + _dot(vt, p.astype(BF16))
        m_ref[...] = m_new

    def normalized(acc):
        return acc[:HEAD_DIM] * (1.0 / jnp.maximum(acc[HEAD_DIM:HEAD_DIM + 1], 1e-30))

    for g in range(KV_HEADS):
        qg = jnp.concatenate(
            [qt_ref[(Q_PER_KV * g + r) * HEAD_DIM:(Q_PER_KV * g + r + 1) * HEAD_DIM, :] for r in range(Q_PER_KV)],
            axis=1)

        sc = _dot(kc_ref[:, g * HEAD_DIM:(g + 1) * HEAD_DIM], qg) + cb_ref[...]
        e = jnp.exp2(sc - jnp.max(sc, axis=0, keepdims=True)) * cv_ref[...]
        pc = e * (1.0 / jnp.maximum(jnp.sum(e, axis=0, keepdims=True), 1e-30))
        o_c = _dot(vct_ref[g * HEAD_DIM:(g + 1) * HEAD_DIM, :], pc.astype(BF16))

        pq = pc[:, 0:tq]
        for r in range(1, Q_PER_KV):
            pq = pq + pc[:, r * tq:(r + 1) * tq]
        imp = pq[:n_sel] + pq[n_sel:]
        blk = lax.broadcasted_iota(jnp.int32, (n_sel, tq), 0)
        cur = (t0 + lax.broadcasted_iota(jnp.int32, (n_sel, tq), 1)) // SEL_BLOCK
        imp = jnp.where(blk > cur, -1.0, imp)
        imp = jnp.where(blk == 0, FORCED_FIRST, imp)
        imp = jnp.where(blk == cur, FORCED_CUR, imp)
        n_live = (t0 + tq - 1) // SEL_BLOCK + 1
        notsel = _topk_not_selected(imp, n_live, rank_ref)
        if n_sel < HEAD_DIM:
            notsel = jnp.concatenate([notsel, jnp.zeros((HEAD_DIM - n_sel, tq), F32)], axis=0)
        qa = jnp.concatenate([qg, jnp.concatenate([notsel.astype(BF16)] * Q_PER_KV, axis=1)], axis=0)

        for ref in (m_ref, mw_ref):
            ref[...] = jnp.full((1, rq), NEG, F32)
        for ref in (acc_ref, accw_ref):
            ref[...] = jnp.zeros((V_AUG_ROWS, rq), F32)
        slots = [sa_ref, sb_ref]
        qw = jnp.concatenate([qg, jnp.zeros_like(qg)], axis=0)

        slots[0][...] = _dot(kwa_ref[g, wkt[0]], qw)
        for j, kt in enumerate(wkt):
            if j + 1 < n_wt:
                slots[1 - j % 2][...] = _dot(kwa_ref[g, jnp.maximum(wkt[j + 1], 0)], qw)
            else:
                slots[1 - j % 2][...] = _dot(ksa_ref[g, 0], qa)
            online_update(slots[j % 2], vwa_ref[g, jnp.maximum(kt, 0)], wb_ref[j], mw_ref, accw_ref)
        o_w = normalized(accw_ref[...])

        s_even, s_odd = slots[n_wt % 2], slots[1 - n_wt % 2]

        def sel_pair(j, carry):
            s_odd[...] = _dot(ksa_ref[g, 2 * j + 1], qa)
            online_update(s_even, vsa_ref[g, 2 * j], None, m_ref, acc_ref)
            s_even[...] = _dot(ksa_ref[g, 2 * j + 2], qa)
            online_update(s_odd, vsa_ref[g, 2 * j + 1], None, m_ref, acc_ref)
            return carry

        lax.fori_loop(0, kt_diag // 2, sel_pair, 0)

        @pl.when(kt_diag % 2 == 1)
        def _():
            s_odd[...] = _dot(ksa_ref[g, kt_diag], qa)
            online_update(s_even, vsa_ref[g, kt_diag - 1], None, m_ref, acc_ref)
            online_update(s_odd, vsa_ref[g, kt_diag], wb_ref[0], m_ref, acc_ref)

        @pl.when(kt_diag % 2 == 0)
        def _():
            online_update(s_even, vsa_ref[g, kt_diag], wb_ref[0], m_ref, acc_ref)

        o_s = normalized(acc_ref[...])

        outs = []
        for r in range(Q_PER_KV):
            h = Q_PER_KV * g + r
            cs = slice(r * tq, (r + 1) * tq)
            gc = gatest_ref[N_BRANCH * h:N_BRANCH * h + 1, :]
            gs = gatest_ref[N_BRANCH * h + 1:N_BRANCH * h + 2, :]
            gw = gatest_ref[N_BRANCH * h + 2:N_BRANCH * h + 3, :]
            outs.append(gc * o_c[:, cs] + gs * o_s[:, cs] + gw * o_w[:, cs])
        for pr in range(Q_PER_KV // 2):
            c0 = (Q_PER_KV * g + 2 * pr) * HEAD_DIM
            o_ref[:, c0:c0 + LANES] = jnp.concatenate(outs[2 * pr:2 * pr + 2], axis=0).T.astype(BF16)


def _nsa_attn_prompt(qt, gatest, kc, vct, ksa, vsa, kwa, vwa, batch, t_seq):
    d = qt.shape[1]
    tq = min(ATT_TILE, t_seq)
    nq = t_seq // tq
    n_cmp = t_seq // CMP_BLOCK
    tk = ksa.shape[3]
    featmaj = lambda rows: pl.BlockSpec((None, rows, tq), lambda b, i: (b, 0, i))
    seq = lambda shape: pl.BlockSpec((None,) + shape[1:], lambda b, i: (b, 0, 0, 0, 0))
    return pl.pallas_call(
        _nsa_attn_prompt_kernel,
        grid=(batch, nq),
        in_specs=[featmaj(d), featmaj(LANES),
                  pl.BlockSpec((None, n_cmp, KV_DIM), lambda b, i: (b, 0, 0)),
                  pl.BlockSpec((None, KV_DIM, n_cmp), lambda b, i: (b, 0, 0)),
                  seq(ksa.shape), seq(vsa.shape), seq(kwa.shape), seq(vwa.shape)],
        out_specs=pl.BlockSpec((tq, d), lambda b, i: (b * nq + i, 0)),
        out_shape=jax.ShapeDtypeStruct((batch * t_seq, d), BF16),
        scratch_shapes=[pltpu.VMEM((V_AUG_ROWS, Q_PER_KV * tq), F32), pltpu.VMEM((1, Q_PER_KV * tq), F32),
                        pltpu.VMEM((V_AUG_ROWS, Q_PER_KV * tq), F32), pltpu.VMEM((1, Q_PER_KV * tq), F32),
                        pltpu.VMEM((tk, Q_PER_KV * tq), F32), pltpu.VMEM((tk, Q_PER_KV * tq), F32),
                        pltpu.VMEM((-(-(WINDOW - 1) // tk) + 1, tk, Q_PER_KV * tq), F32),
                        pltpu.VMEM((n_cmp, Q_PER_KV * tq), F32), pltpu.VMEM((n_cmp, Q_PER_KV * tq), F32),
                        pltpu.VMEM((n_cmp // 2, tq), F32)],
        compiler_params=_cparams("parallel", "arbitrary"),
        name="nsa_attn_prompt",
    )(qt, gatest, kc, vct, ksa, vsa, kwa, vwa)


def _compress_sample_kernel(pt_ref, *refs, n_pages, page, t_new, n_cmp):
    del pt_ref
    pages = refs[:n_pages]
    new_ref, pe_ref, w1_ref, w2t_ref, kc_ref, vc_ref = refs[n_pages:n_pages + 6]
    bufs = refs[n_pages + 6:]
    n_pad = bufs[0].shape[1]
    n_past = n_pages * page // CMP_BLOCK
    grp = SUBLANES * CMP_BLOCK
    ppg = grp // page
    tt = lax.broadcasted_iota(jnp.int32, (grp, grp), 0)
    perm_t = jnp.where(lax.broadcasted_iota(jnp.int32, (grp, grp), 1)
                       == (tt % CMP_BLOCK) * SUBLANES + tt // CMP_BLOCK, 1.0, 0.0).astype(BF16)
    for c, buf_ref in enumerate(bufs):
        cs = slice(c * LANES, (c + 1) * LANES)
        for pp in range(n_pages // ppg):
            xt = jnp.concatenate([pages[pp * ppg + j][cs, :] for j in range(ppg)], axis=1)
            xp = _dot(xt.astype(BF16), perm_t).T
            for l in range(CMP_BLOCK):
                buf_ref[l, pp * SUBLANES:(pp + 1) * SUBLANES, :] = xp[l * SUBLANES:(l + 1) * SUBLANES, :]
        buf_ref[:, n_past:, :] = jnp.zeros((CMP_BLOCK, n_pad - n_past, LANES), F32)
        for l in range(t_new):
            buf_ref[l, n_past:n_past + 1, :] = new_ref[l:l + 1, cs]
    n_out = kc_ref.shape[0]
    for kv, out_ref in ((0, kc_ref), (1, vc_ref)):
        def load(l, hf, kv=kv):
            return bufs[2 * kv + hf][l]
        y = _compress_rows(load, pe_ref, w1_ref, w2t_ref, kv, n_pad)
        valid = lax.broadcasted_iota(jnp.int32, (n_pad, HEAD_DIM), 0) < n_cmp
        for g in range(KV_HEADS):
            out_ref[0:n_pad, g * HEAD_DIM:(g + 1) * HEAD_DIM] = jnp.where(
                valid, y[g * n_pad:(g + 1) * n_pad], 0.0).astype(BF16)
        out_ref[n_pad:, :] = jnp.zeros((n_out - n_pad, KV_DIM), BF16)


def _page_specs(layer, n_pages, page, feat_block):
    half = 2 * KV_DIM
    return [pl.BlockSpec((None, None, half, page),
                         functools.partial(lambda b, pt, p: (layer, pt[b, p], feat_block, 0), p=p))
            for p in range(n_pages)]


def _compress_sample(page_table, cache_t, layer, rows_new, pe2, w1, w2t, t_new):
    dec_b, n_pages = page_table.shape
    page = cache_t.shape[-1]
    past = n_pages * page
    l_pad = -(-(past + t_new) // SEL_BLOCK) * SEL_BLOCK
    n_cmp = l_pad // CMP_BLOCK
    n_pad = -(-n_cmp // SUBLANES) * SUBLANES
    n_out = LANES
    half = 2 * KV_DIM
    kern = functools.partial(_compress_sample_kernel, n_pages=n_pages, page=page, t_new=t_new, n_cmp=n_cmp)
    grid_spec = pltpu.PrefetchScalarGridSpec(
        num_scalar_prefetch=1,
        grid=(dec_b,),
        in_specs=_page_specs(layer, n_pages, page, 0) + [
            pl.BlockSpec((t_new, half), lambda b, pt: (b, 0)),
            pl.BlockSpec(pe2.shape, lambda b, pt: (0, 0, 0)),
            pl.BlockSpec(w1.shape, lambda b, pt: (0, 0, 0)),
            pl.BlockSpec(w2t.shape, lambda b, pt: (0, 0, 0))],
        out_specs=[pl.BlockSpec((None, n_out, KV_DIM), lambda b, pt: (b, 0, 0))] * 2,
        scratch_shapes=[pltpu.VMEM((CMP_BLOCK, n_pad, LANES), F32)] * (half // LANES),
    )
    return pl.pallas_call(
        kern,
        grid_spec=grid_spec,
        out_shape=[jax.ShapeDtypeStruct((dec_b, n_out, KV_DIM), BF16)] * 2,
        compiler_params=_cparams("arbitrary"),
        name="nsa_compress_sample",
    )(page_table, *([cache_t] * n_pages), rows_new, pe2, w1, w2t), n_cmp


def _nsa_attn_sample_kernel(pt_ref, *refs, n_pages, page, t_new, n_cmp):
    del pt_ref
    pages = refs[:n_pages]
    (kc_ref, vc_ref, q_ref, new_ref, wnew_ref, win_ref, gates_ref,
     o_ref, nwin_ref, kst_ref, vst_ref) = refs[n_pages:]
    past = n_pages * page
    n_buf = win_ref.shape[1]
    n_sel = n_cmp // 2
    gq = Q_PER_KV * t_new
    rq = KV_HEADS * gq
    nk = past + LANES
    nw = n_buf + LANES

    for p in range(n_pages):
        blk = pages[p][...]
        kst_ref[:, p * page:(p + 1) * page] = blk[:KV_DIM].astype(BF16)
        vst_ref[:, p * page:(p + 1) * page] = blk[KV_DIM:].astype(BF16)
    pad = jnp.zeros((LANES - t_new, KV_DIM), F32)
    new = new_ref[...]
    wnew = wnew_ref[...]
    knew = jnp.concatenate([new[:, :KV_DIM], pad], axis=0).astype(BF16)
    vnew = jnp.concatenate([new[:, KV_DIM:], pad], axis=0).astype(BF16)
    kwnew = jnp.concatenate([wnew[:, :KV_DIM], pad], axis=0).astype(BF16)
    vwnew = jnp.concatenate([wnew[:, KV_DIM:], pad], axis=0).astype(BF16)
    win = win_ref[...]
    kwt = win[:KV_DIM].astype(BF16)
    vwt = win[KV_DIM:].astype(BF16)

    rolled = pltpu.roll(win, n_buf - t_new, 1)
    wnew_t = jnp.concatenate([jnp.zeros((LANES - t_new, 2 * KV_DIM), F32), wnew], axis=0).T
    tail_lane = lax.broadcasted_iota(jnp.int32, (2 * KV_DIM, LANES), 1)
    nwin_ref[:, 0:n_buf - LANES] = rolled[:, 0:n_buf - LANES]
    nwin_ref[:, n_buf - LANES:] = jnp.where(tail_lane >= LANES - t_new, wnew_t, rolled[:, n_buf - LANES:])

    q = q_ref[...]
    lane_g = lax.broadcasted_iota(jnp.int32, (t_new, KV_DIM), 1) // HEAD_DIM
    pieces = []
    for g in range(KV_HEADS):
        for j in range(Q_PER_KV):
            qj = jnp.concatenate([q[:, (Q_PER_KV * gg + j) * HEAD_DIM:(Q_PER_KV * gg + j + 1) * HEAD_DIM]
                                  for gg in range(KV_HEADS)], axis=1)
            pieces.append(jnp.where(lane_g == g, qj, jnp.zeros_like(qj)))
    qbd = jnp.concatenate(pieces, axis=0)
    t_row = past + lax.broadcasted_iota(jnp.int32, (rq, 1), 0) % t_new

    def softmax(s, mask):
        s = jnp.where(mask, s, NEG)
        e = jnp.where(mask, jnp.exp(s - jnp.max(s, axis=1, keepdims=True)), 0.0)
        return e * (1.0 / jnp.maximum(jnp.sum(e, axis=1, keepdims=True), 1e-30))

    ncl = kc_ref.shape[0]
    sc = _dot_nt(qbd, kc_ref[...])
    cidx = lax.broadcasted_iota(jnp.int32, (rq, ncl), 1)
    pc = softmax(sc, (cidx < n_cmp) & ((cidx + 1) * CMP_BLOCK - 1 <= t_row))
    oc = _dot(pc.astype(BF16), vc_ref[...])

    groups = []
    for g in range(KV_HEADS):
        pg = pc[g * gq:g * gq + t_new]
        for j in range(1, Q_PER_KV):
            pg = pg + pc[g * gq + j * t_new:g * gq + (j + 1) * t_new]
        groups.append(pg)
    pg = jnp.concatenate(groups, axis=0)
    nr = KV_HEADS * t_new
    imp = pg + pltpu.roll(pg, ncl - 1, 1)
    lane = lax.broadcasted_iota(jnp.int32, (nr, ncl), 1)
    blk = lane // 2
    cur = (past + lax.broadcasted_iota(jnp.int32, (nr, ncl), 0) % t_new) // SEL_BLOCK
    imp = jnp.where(blk > cur, -1.0, imp)
    imp = jnp.where(blk == 0, FORCED_FIRST, imp)
    imp = jnp.where(blk == cur, FORCED_CUR, imp)
    rank = jnp.zeros((nr, ncl), jnp.int32)
    for bp in range(n_sel):
        other = imp[:, 2 * bp:2 * bp + 1]
        rank = rank + ((other > imp) | ((other == imp) & (bp < blk))).astype(jnp.int32)
    sel = jnp.where((rank < TOP_N) & (lane % 2 == 0) & (blk < n_sel), 1.0, 0.0)
    sel_rows = jnp.concatenate([sel[g * t_new:(g + 1) * t_new] for g in range(KV_HEADS) for _ in range(Q_PER_KV)],
                               axis=0).astype(BF16)
    expand = jnp.where(lax.broadcasted_iota(jnp.int32, (ncl, nk), 0)
                       == 2 * (lax.broadcasted_iota(jnp.int32, (ncl, nk), 1) // SEL_BLOCK), 1.0, 0.0).astype(BF16)
    selmask = _dot(sel_rows, expand) > 0.5

    kpos = lax.broadcasted_iota(jnp.int32, (rq, nk), 1)
    ss = jnp.concatenate([_dot(qbd, kst_ref[...]), _dot_nt(qbd, knew)], axis=1)
    ps = softmax(ss, selmask & (kpos <= t_row)).astype(BF16)
    os_ = _dot_nt(ps[:, :past], vst_ref[...]) + _dot(ps[:, past:], vnew)

    wpos = past - n_buf + lax.broadcasted_iota(jnp.int32, (rq, nw), 1)
    sw = jnp.concatenate([_dot(qbd, kwt), _dot_nt(qbd, kwnew)], axis=1)
    pw = softmax(sw, (wpos <= t_row) & (wpos > t_row - WINDOW)).astype(BF16)
    ow = _dot_nt(pw[:, :n_buf], vwt) + _dot(pw[:, n_buf:], vwnew)

    gates = gates_ref[...]
    for g in range(KV_HEADS):
        for j in range(Q_PER_KV):
            h = Q_PER_KV * g + j
            rs = slice(g * gq + j * t_new, g * gq + (j + 1) * t_new)
            cs = slice(g * HEAD_DIM, (g + 1) * HEAD_DIM)
            gc = gates[:, N_BRANCH * h:N_BRANCH * h + 1]
            gs = gates[:, N_BRANCH * h + 1:N_BRANCH * h + 2]
            gw = gates[:, N_BRANCH * h + 2:N_BRANCH * h + 3]
            o_ref[:, h * HEAD_DIM:(h + 1) * HEAD_DIM] = (
                gc * oc[rs, cs] + gs * os_[rs, cs] + gw * ow[rs, cs]).astype(BF16)


def _nsa_attn_sample(page_table, cache_t, win_t, layer, kc, vc, q, rows_new, wrows_new, gates, t_new, n_cmp):
    dec_b, n_pages = page_table.shape
    page = cache_t.shape[-1]
    past = n_pages * page
    n_buf = win_t.shape[-1]
    half = 2 * KV_DIM
    d = q.shape[1]
    ncl = kc.shape[1]
    kern = functools.partial(_nsa_attn_sample_kernel, n_pages=n_pages, page=page, t_new=t_new, n_cmp=n_cmp)
    grid_spec = pltpu.PrefetchScalarGridSpec(
        num_scalar_prefetch=1,
        grid=(dec_b,),
        in_specs=_page_specs(layer, n_pages, page, 1) + [
            pl.BlockSpec((None, ncl, KV_DIM), lambda b, pt: (b, 0, 0)),
            pl.BlockSpec((None, ncl, KV_DIM), lambda b, pt: (b, 0, 0)),
            pl.BlockSpec((t_new, d), lambda b, pt: (b, 0)),
            pl.BlockSpec((t_new, half), lambda b, pt: (b, 1)),
            pl.BlockSpec((t_new, half), lambda b, pt: (b, 0)),
            pl.BlockSpec((None, None, half, n_buf), lambda b, pt: (layer, b, 0, 0)),
            pl.BlockSpec((t_new, LANES), lambda b, pt: (b, 0))],
        out_specs=[pl.BlockSpec((t_new, d), lambda b, pt: (b, 0)),
                   pl.BlockSpec((None, half, n_buf), lambda b, pt: (b, 0, 0))],
        scratch_shapes=[pltpu.VMEM((KV_DIM, past), BF16), pltpu.VMEM((KV_DIM, past), BF16)],
    )
    return pl.pallas_call(
        kern,
        grid_spec=grid_spec,
        out_shape=[jax.ShapeDtypeStruct((dec_b * t_new, d), BF16), jax.ShapeDtypeStruct((dec_b, half, n_buf), F32)],
        compiler_params=_cparams("arbitrary"),
        name="nsa_attn_sample",
    )(page_table, *([cache_t] * n_pages), kc, vc, q, rows_new, wrows_new, win_t, gates)


def _post_mixer_kernel(x_ref, o_ref, wo_ref, gm_ref, w1_ref, w2_ref, p_ref, gp_ref, wg_ref, wp_ref, gf_ref,
                       y_ref, xn_ref, acc_ref, *, head_major, final_norm):
    j = pl.program_id(1)

    @pl.when(j == 0)
    def _():
        if head_major:
            o = jnp.concatenate([o_ref[h] for h in range(o_ref.shape[0])], axis=1)
        else:
            o = o_ref[...]
        x = x_ref[...] + _dot(o, wo_ref[...])
        xn_ref[...] = _rms(x, gm_ref[...]).astype(BF16)
        acc_ref[...] = x

    h = jnp.maximum(_dot(xn_ref[...], w1_ref[...]), 0.0)
    acc_ref[...] += _dot((h * h).astype(BF16), w2_ref[...])

    @pl.when(j == pl.num_programs(1) - 1)
    def _():
        x = acc_ref[...]
        gate = jax.nn.sigmoid(_dot(_rms(x, gp_ref[...]).astype(BF16), wg_ref[...]))
        y = x + _dot(p_ref[...].astype(BF16), wp_ref[...]) * gate
        if final_norm:
            y = _rms(y, gf_ref[...])
        y_ref[...] = y


def _post_mixer(x, o, w_out, g_mlp, w1, w2, p_all, layer, g_ple, wg, wp, gf, final_norm):
    m, d = x.shape
    ff = w1.shape[1]
    tm, tf = min(ROW_TILE, m), 1024
    head_major = o.ndim == 3
    row = pl.BlockSpec((tm, d), lambda i, j: (i, 0))
    vec = pl.BlockSpec((1, d), lambda i, j: (0, 0))
    full = lambda a: pl.BlockSpec(a.shape, lambda i, j: (0,) * a.ndim)
    o_spec = pl.BlockSpec((o.shape[0], tm, o.shape[2]), lambda i, j: (0, i, 0)) if head_major else row
    return pl.pallas_call(
        functools.partial(_post_mixer_kernel, head_major=head_major, final_norm=final_norm),
        grid=(m // tm, ff // tf),
        in_specs=[row, o_spec, full(w_out), vec,
                  pl.BlockSpec((d, tf), lambda i, j: (0, j)), pl.BlockSpec((tf, d), lambda i, j: (j, 0)),
                  pl.BlockSpec((None, tm, p_all.shape[-1]), lambda i, j: (layer, i, 0)), vec,
                  full(wg), full(wp), vec],
        out_specs=row,
        out_shape=jax.ShapeDtypeStruct((m, d), F32),
        scratch_shapes=[pltpu.VMEM((tm, d), BF16), pltpu.VMEM((tm, d), F32)],
        compiler_params=_cparams("parallel", "arbitrary"),
        name="post_mixer",
    )(x, o, w_out, g_mlp, w1, w2, p_all, g_ple, wg, wp, gf)


def _hg_inproj_kernel(x_ref, g_ref, lbl_ref, w_ref, q_ref, f_ref, k_ref, v_ref, gs_ref, *, layer):
    d = x_ref.shape[1]
    xn = _rms(x_ref[...], g_ref[...]).astype(BF16)

    def put(ref, val):
        for h in range(HG_HEADS):
            ref[h] = val[:, h * HG_DK:(h + 1) * HG_DK]

    lg = lbl_ref[...]
    e = jnp.exp(lg - jnp.max(lg, axis=0, keepdims=True))
    w = e / jnp.sum(e, axis=0, keepdims=True)
    cs = w[0:1]
    for r in range(1, layer + 1):
        cs = cs + w[r:r + 1]
    lb = cs - w[0:1]

    y = _dot(xn, w_ref[:, 0:d])
    put(q_ref, jax.nn.silu(y) * HG_DK ** -0.5)
    y = _dot(xn, w_ref[:, d:2 * d])
    sg = jax.nn.sigmoid(y)
    put(f_ref, jnp.log(lb + (1.0 - lb) * sg))
    put(k_ref, (1.0 - lb) * (1.0 - sg))
    put(v_ref, _dot(xn, w_ref[:, 2 * d:3 * d]))
    put(gs_ref, jax.nn.silu(_dot(xn, w_ref[:, 3 * d:4 * d])))


def _hg_inproj(x, g, lb_logits, w, layer):
    m, d = x.shape
    tm = min(ROW_TILE, m)
    head_major = pl.BlockSpec((HG_HEADS, tm, HG_DK), lambda i: (0, i, 0))
    return pl.pallas_call(
        functools.partial(_hg_inproj_kernel, layer=layer),
        grid=(m // tm,),
        in_specs=[pl.BlockSpec((tm, d), lambda i: (i, 0)), _full((1, d)), _full(lb_logits.shape), _full(w.shape)],
        out_specs=[head_major] * 5,
        out_shape=[jax.ShapeDtypeStruct((HG_HEADS, m, HG_DK), F32)] * 5,
        compiler_params=_cparams("parallel"),
        name="hgrn_inproj",
    )(x, g, lb_logits, w)


def _cumsum_rows(f):
    c = f.shape[0]
    row = lax.broadcasted_iota(jnp.int32, f.shape, 0)
    s = 1
    while s < c:
        f = f + jnp.where(row >= s, pltpu.roll(f, s, 0), 0.0)
        s *= 2
    return f


def _hg_intra(q, k, v, b, o, sub):
    c = q.shape[0]
    ones = jnp.ones((HG_DK, HG_DV), BF16)
    trow = lax.broadcasted_iota(jnp.int32, (sub, HG_DK), 0)
    terms = []
    for r0 in range(0, c, sub):
        qi, ki, bi = q[r0:r0 + sub], k[r0:r0 + sub], b[r0:r0 + sub]
        for s in range(sub):
            causal = trow >= s
            dec = jnp.exp(jnp.where(causal, bi - bi[s:s + 1], 0.0))
            terms.append(jnp.where(causal, qi * ki[s:s + 1] * dec, 0.0).astype(BF16))
    a = _dot(jnp.concatenate(terms, axis=0), ones)
    outs = []
    for r0 in range(0, c, sub):
        oi = o[r0:r0 + sub]
        if r0 > 0:
            bs = b[r0 - 1:r0]
            qp = (q[r0:r0 + sub] * jnp.exp(b[r0:r0 + sub] - bs)).astype(BF16)
            kp = (k[:r0] * jnp.exp(bs - b[:r0])).astype(BF16)
            oi = oi + _dot(_dot_nt(qp, kp).astype(BF16), v[:r0].astype(BF16))
        for s in range(sub):
            oi = oi + a[(r0 + s) * sub:(r0 + s + 1) * sub] * v[r0 + s:r0 + s + 1]
        outs.append(oi)
    return jnp.concatenate(outs, axis=0) if len(outs) > 1 else outs[0]


def _hg_out(o, ng, gs):
    return (_rms(o, ng) * gs).astype(BF16)


def _hg_diag_terms(q, k, b, sub):
    trow = lax.broadcasted_iota(jnp.int32, (sub, HG_DK), 0)
    terms = []
    for r0 in range(0, q.shape[0], sub):
        qi, ki, bi = q[r0:r0 + sub], k[r0:r0 + sub], b[r0:r0 + sub]
        for s in range(sub):
            causal = trow >= s
            dec = jnp.exp(jnp.where(causal, bi - bi[s:s + 1], 0.0))
            terms.append(jnp.where(causal, qi * ki[s:s + 1] * dec, 0.0).astype(BF16))
    return jnp.concatenate(terms, axis=0)


def _hg_offdiag(q, k, v, b, sub):
    c = q.shape[0]
    qps, kps, vps, spans = [], [], [], []
    off = 0
    for r0 in range(sub, c, sub):
        bs = b[r0 - 1:r0]
        qps.append(q[r0:r0 + sub] * jnp.exp(b[r0:r0 + sub] - bs))
        kps.append(k[:r0] * jnp.exp(bs - b[:r0]))
        vps.append(v[:r0])
        spans.append((off, off + r0))
        off += r0
    a = _dot_nt(jnp.concatenate(qps, axis=0).astype(BF16), jnp.concatenate(kps, axis=0).astype(BF16))
    row_blk = lax.broadcasted_iota(jnp.int32, a.shape, 0) // sub
    col = lax.broadcasted_iota(jnp.int32, a.shape, 1)
    keep = jnp.zeros(a.shape, jnp.bool_)
    for i, (lo, hi) in enumerate(spans):
        keep = keep | ((row_blk == i) & (col >= lo) & (col < hi))
    return _dot(jnp.where(keep, a, 0.0).astype(BF16), jnp.concatenate(vps, axis=0).astype(BF16))


def _hg_factored_scores(q, k, b, sub):
    c = q.shape[0]
    qps, kps, spans = [], [], []
    off = 0
    for r0 in range(0, c, sub):
        hi = r0 + sub
        bs = b[r0 - 1:r0] if r0 > 0 else jnp.zeros((1, HG_DK), F32)
        qps.append(q[r0:hi] * jnp.exp(b[r0:hi] - bs))
        kps.append(k[:hi] * jnp.exp(bs - b[:hi]))
        spans.append((off, off + hi))
        off += hi
    a = _dot_nt(jnp.concatenate(qps, axis=0).astype(BF16), jnp.concatenate(kps, axis=0).astype(BF16))
    row = lax.broadcasted_iota(jnp.int32, a.shape, 0)
    col = lax.broadcasted_iota(jnp.int32, a.shape, 1)
    keep = jnp.zeros(a.shape, jnp.bool_)
    for i, (lo, hi) in enumerate(spans):
        keep = keep | ((row // sub == i) & (col >= lo) & (col < hi) & (col - lo <= row))
    return jnp.where(keep, a, 0.0).astype(BF16)


def _hg_stacked_values(v, sub):
    return jnp.concatenate([v[:r0 + sub] for r0 in range(0, v.shape[0], sub)], axis=0).astype(BF16)


def _hg_scan_prompt_kernel(q_ref, f_ref, k_ref, v_ref, gs_ref, ng_ref, o_ref, s_ref,
                           st_ref, b_ref, t_ref, a_ref, i_ref):
    c = pl.program_id(1)
    rows = q_ref.shape[1]
    n_chunk = rows // HG_CHUNK
    tr = HG_CHUNK * HG_SUB
    ones = jnp.ones((HG_DK, HG_DV), BF16)

    @pl.when(c == 0)
    def _():
        st_ref[...] = jnp.zeros(st_ref.shape, F32)

    def heads(it, carry):
        drop = jnp.zeros((1, HG_DK), F32)
        for hh in range(HG_UNROLL_PROMPT):
            h = it * HG_UNROLL_PROMPT + hh
            for ci in range(n_chunk):
                u = hh * n_chunk + ci
                b = _cumsum_rows(f_ref[h, pl.ds(ci * HG_CHUNK, HG_CHUNK), :])
                b_ref[u] = b
                for r0 in range(0, HG_CHUNK, HG_SUB):
                    last = b[r0 + HG_SUB - 1:r0 + HG_SUB]
                    drop = jnp.maximum(drop, -last if r0 == 0 else b[r0 - 1:r0] - last)
        factorable = jnp.max(drop) < HG_MAX_FACTOR_EXP

        @pl.when(factorable)
        def _():
            units = [(it * HG_UNROLL_PROMPT + hh, hh * n_chunk + ci, pl.ds(ci * HG_CHUNK, HG_CHUNK))
                     for hh in range(HG_UNROLL_PROMPT) for ci in range(n_chunk)]
            scores = [_hg_factored_scores(q_ref[h, rs, :], k_ref[h, rs, :], b_ref[u], HG_SUB) for h, u, rs in units]
            for (h, u, rs), a in zip(units, scores):
                i_ref[u] = _dot(a, _hg_stacked_values(v_ref[h, rs, :], HG_SUB))

        @pl.when(jnp.logical_not(factorable))
        def _():
            for hh in range(HG_UNROLL_PROMPT):
                h = it * HG_UNROLL_PROMPT + hh
                for ci in range(n_chunk):
                    u = hh * n_chunk + ci
                    rs = pl.ds(ci * HG_CHUNK, HG_CHUNK)
                    t_ref[u * tr:(u + 1) * tr, :] = _hg_diag_terms(q_ref[h, rs, :], k_ref[h, rs, :], b_ref[u], HG_SUB)
            a_ref[...] = _dot(t_ref[...], ones)
            for hh in range(HG_UNROLL_PROMPT):
                h = it * HG_UNROLL_PROMPT + hh
                for ci in range(n_chunk):
                    u = hh * n_chunk + ci
                    rs = pl.ds(ci * HG_CHUNK, HG_CHUNK)
                    q, k, v, b = q_ref[h, rs, :], k_ref[h, rs, :], v_ref[h, rs, :], b_ref[u]
                    off = _hg_offdiag(q, k, v, b, HG_SUB)
                    for r0 in range(0, HG_CHUNK, HG_SUB):
                        oi = off[r0 - HG_SUB:r0] if r0 > 0 else jnp.zeros((HG_SUB, HG_DV), F32)
                        for s in range(HG_SUB):
                            a0 = u * tr + (r0 + s) * HG_SUB
                            oi = oi + a_ref[a0:a0 + HG_SUB, :] * v[r0 + s:r0 + s + 1]
                        i_ref[u, r0:r0 + HG_SUB, :] = oi

        for ci in range(n_chunk):
            rs = pl.ds(ci * HG_CHUNK, HG_CHUNK)
            hs = [(it * HG_UNROLL_PROMPT + hh, hh * n_chunk + ci) for hh in range(HG_UNROLL_PROMPT)]
            inter = [_dot_nt((q_ref[h, rs, :] * jnp.exp(b_ref[u])).astype(BF16), st_ref[h].astype(BF16))
                     for h, u in hs]
            grow = []
            for h, u in hs:
                b = b_ref[u]
                kd = (k_ref[h, rs, :] * jnp.exp(b[HG_CHUNK - 1:HG_CHUNK] - b)).astype(BF16)
                grow.append(lax.dot_general(v_ref[h, rs, :].astype(BF16), kd, (((0,), (0,)), ((), ())),
                                            preferred_element_type=F32))
            for (h, u), oi, gr in zip(hs, inter, grow):
                o_ref[h, rs, :] = _hg_out(i_ref[u] + oi, ng_ref[...], gs_ref[h, rs, :])
                st_ref[h] = st_ref[h] * jnp.exp(b_ref[u, HG_CHUNK - 1:HG_CHUNK, :]) + gr

        @pl.when(c == pl.num_programs(1) - 1)
        def _():
            for hh in range(HG_UNROLL_PROMPT):
                h = it * HG_UNROLL_PROMPT + hh
                s_ref[h] = st_ref[h].T
        return carry

    lax.fori_loop(0, HG_HEADS // HG_UNROLL_PROMPT, heads, 0)


def _hg_scan_prompt(q, f, k, v, gs, ng, batch, t_seq):
    m = q.shape[1]
    rows = 2 * HG_CHUNK
    nc = t_seq // rows
    units = HG_UNROLL_PROMPT * (rows // HG_CHUNK)
    blk = pl.BlockSpec((HG_HEADS, rows, HG_DK), lambda b, c: (0, b * nc + c, 0))
    return pl.pallas_call(
        _hg_scan_prompt_kernel,
        grid=(batch, nc),
        in_specs=[blk] * 5 + [pl.BlockSpec((1, HG_DV), lambda b, c: (0, 0))],
        out_specs=[blk, pl.BlockSpec((None, HG_HEADS, HG_DK, HG_DV), lambda b, c: (b, 0, 0, 0))],
        out_shape=[jax.ShapeDtypeStruct((HG_HEADS, m, HG_DV), BF16),
                   jax.ShapeDtypeStruct((batch, HG_HEADS, HG_DK, HG_DV), F32)],
        scratch_shapes=[pltpu.VMEM((HG_HEADS, HG_DV, HG_DK), F32),
                        pltpu.VMEM((units, HG_CHUNK, HG_DK), F32),
                        pltpu.VMEM((units * HG_CHUNK * HG_SUB, HG_DK), BF16),
                        pltpu.VMEM((units * HG_CHUNK * HG_SUB, HG_DV), F32),
                        pltpu.VMEM((units, HG_CHUNK, HG_DV), F32)],
        compiler_params=_cparams("parallel", "arbitrary"),
        name="hgrn_scan_prompt",
    )(q, f, k, v, gs, ng)


def _hg_step_sample_kernel(q_ref, f_ref, k_ref, v_ref, gs_ref, ng_ref, s0_ref, o_ref, s1_ref, *, t_new):
    n_seq = s0_ref.shape[0]

    def one(sq, h):
        rs = pl.ds(pl.multiple_of(sq * t_new, t_new), t_new)
        q, k, v = q_ref[h, rs, :], k_ref[h, rs, :], v_ref[h, rs, :]
        b = _cumsum_rows(f_ref[h, rs, :])
        s0 = s0_ref[sq, h]
        o = _dot((q * jnp.exp(b)).astype(BF16), s0.astype(BF16))
        o = _hg_intra(q, k, v, b, o, t_new)
        o_ref[h, rs, :] = _hg_out(o, ng_ref[...], gs_ref[h, rs, :])
        bl = b[t_new - 1:t_new]
        kd = k * jnp.exp(bl - b)
        ext = jnp.concatenate([kd, jnp.broadcast_to(jnp.exp(bl), (t_new, HG_DK)),
                               jnp.zeros((HG_DK - 2 * t_new, HG_DK), F32)], axis=0).T
        s1_ref[sq, h] = s0 * ext[:, t_new:t_new + 1] + _dot(ext[:, :t_new].astype(BF16), v.astype(BF16))

    def body(idx, carry):
        per_seq = HG_HEADS // HG_UNROLL_SAMPLE
        for hh in range(HG_UNROLL_SAMPLE):
            one(idx // per_seq, (idx % per_seq) * HG_UNROLL_SAMPLE + hh)
        return carry

    lax.fori_loop(0, n_seq * (HG_HEADS // HG_UNROLL_SAMPLE), body, 0)


def _hg_step_sample(q, f, k, v, gs, ng, s0_all, layer, t_new):
    m = q.shape[1]
    dec_b = s0_all.shape[1]
    sb = min(8, dec_b)
    blk = pl.BlockSpec((HG_HEADS, sb * t_new, HG_DK), lambda i: (0, i, 0))
    st_in = pl.BlockSpec((None, sb, HG_HEADS, HG_DK, HG_DV), lambda i: (layer, i, 0, 0, 0))
    st_out = pl.BlockSpec((sb, HG_HEADS, HG_DK, HG_DV), lambda i: (i, 0, 0, 0))
    return pl.pallas_call(
        functools.partial(_hg_step_sample_kernel, t_new=t_new),
        grid=(dec_b // sb,),
        in_specs=[blk] * 5 + [pl.BlockSpec((1, HG_DV), lambda i: (0, 0)), st_in],
        out_specs=[blk, st_out],
        out_shape=[jax.ShapeDtypeStruct((HG_HEADS, m, HG_DV), BF16), jax.ShapeDtypeStruct(s0_all.shape[1:], F32)],
        compiler_params=_cparams("parallel"),
        name="hgrn_step_sample",
    )(q, f, k, v, gs, ng, s0_all)


def _feature_major_view(a):
    lead = a.shape[:-4]
    t, c, g, d = a.shape[-4:]
    n = len(lead)
    return jnp.transpose(a, tuple(range(n)) + (n + 1, n + 2, n + 3, n)).reshape(lead + (c * g * d, t))


def _token_major_view(a, c):
    lead = a.shape[:-2]
    t = a.shape[-1]
    n = len(lead)
    a = a.reshape(lead + (c, KV_HEADS, HEAD_DIM, t))
    return jnp.transpose(a, tuple(range(n)) + (n + 3, n, n + 1, n + 2))


def kernel(x_prompt, x_sample, cache_nsa_kv, state_nsa_win, state_hgrn, page_table, p_prompt, p_sample, norm_mix, norm_mlp, norm_ple, norm_final, nsa_w_in, nsa_cmp_pe, nsa_cmp_w1, nsa_cmp_w2, nsa_w_out, hg_w_in, hg_lb_logits, hg_norm, hg_w_out, mlp_w1, mlp_w2, ple_w_proj, ple_w_gate):
    batch, t_p, d = x_prompt.shape
    dec_b, t_s, _ = x_sample.shape
    depth = p_prompt.shape[0]
    page = cache_nsa_kv.shape[2]
    past = page_table.shape[1] * page

    xp = x_prompt.reshape(batch * t_p, d)
    xs = x_sample.reshape(dec_b * t_s, d)
    pp = p_prompt.reshape(depth, batch * t_p, -1)
    ps = p_sample.reshape(depth, dec_b * t_s, -1)
    row = lambda a: a.reshape(1, -1)
    cache_t = _feature_major_view(cache_nsa_kv)
    win_t = _feature_major_view(state_nsa_win)
    wl = min(WINDOW, t_p)

    tabs_p = _rope_tables(t_p, 0, t_p)
    tabs_s = _rope_tables(min(ROW_TILE, dec_b * t_s), past, t_s)

    kv_p, kv_s, win_p, win_s, st_p, st_s = [], [], [], [], [], []
    for i in range(depth):
        g_mix = row(norm_mix[i])
        if i % 2 == 0:
            a = i // 2
            n_in = nsa_w_in.shape[2]
            w_t = jnp.pad(jnp.transpose(nsa_w_in[a]).astype(BF16),
                          ((0, Q_DIM + 6 * KV_DIM + LANES - n_in), (0, 0)))
            pe2 = jnp.tile(nsa_cmp_pe[a], (1, 1, LANES // HEAD_DIM))
            cw1 = nsa_cmp_w1[a].reshape(2, CMP_BLOCK * HEAD_DIM, -1).astype(BF16)
            cw2t = jnp.swapaxes(nsa_cmp_w2[a], 1, 2).astype(BF16)
            w_out = nsa_w_out[a].astype(BF16)

            qt, gatest, cmp_rows, rows_t, wrows_t, ksa, vsa, kwa, vwa = _nsa_inproj_prompt(
                xp, g_mix, tabs_p, w_t, batch, t_p)
            kc, vct = _compress_prompt(cmp_rows, pe2, cw1, cw2t, batch, t_p)
            op = _nsa_attn_prompt(qt, gatest, kc, vct, ksa, vsa, kwa, vwa, batch, t_p)
            kv_p.append(rows_t)
            win_p.append(wrows_t[:, :, t_p - wl:])

            q, gates, rows, wrows = _nsa_inproj_sample(xs, g_mix, tabs_s, w_t)
            (kc, vc), n_cmp = _compress_sample(page_table, cache_t, a, rows, pe2, cw1, cw2t, t_s)
            os_, new_win_t = _nsa_attn_sample(page_table, cache_t, win_t, a, kc, vc, q, rows, wrows, gates, t_s, n_cmp)
            kv_s.append(rows.reshape(dec_b, t_s, 4, KV_HEADS, HEAD_DIM))
            win_s.append(new_win_t)
        else:
            r = i // 2
            w_in = hg_w_in[r].astype(BF16)
            w_out = hg_w_out[r].astype(BF16)
            ng = row(hg_norm[r])
            q, f, k, v, gs = _hg_inproj(xp, g_mix, hg_lb_logits, w_in, r)
            op, s_new = _hg_scan_prompt(q, f, k, v, gs, ng, batch, t_p)
            st_p.append(s_new)
            q, f, k, v, gs = _hg_inproj(xs, g_mix, hg_lb_logits, w_in, r)
            os_, s_new = _hg_step_sample(q, f, k, v, gs, ng, state_hgrn, r, t_s)
            st_s.append(s_new)

        w1 = mlp_w1[i].astype(BF16)
        w2 = mlp_w2[i].astype(BF16)
        wgate = ple_w_gate[i].astype(BF16)
        wproj = ple_w_proj[i].astype(BF16)
        last = i == depth - 1
        post = (w_out, row(norm_mlp[i]), w1, w2)
        ple = (i, row(norm_ple[i]), wgate, wproj, row(norm_final), last)
        xp = _post_mixer(xp, op, *post, pp, *ple)
        xs = _post_mixer(xs, os_, *post, ps, *ple)

    return (xp.reshape(batch, t_p, d), xs.reshape(dec_b, t_s, d),
            _token_major_view(jnp.stack(kv_p), 4), jnp.stack(kv_s),
            _token_major_view(jnp.stack(win_p), 2), _token_major_view(jnp.stack(win_s), 2),
            jnp.stack(st_p), jnp.stack(st_s))
```

```python
import functools

import jax
import jax.numpy as jnp
from jax import lax
from jax.experimental import pallas as pl
from jax.experimental.pallas import tpu as pltpu

F32 = jnp.float32
BF16 = jnp.bfloat16

RMS_EPS = 1e-6
ROPE_THETA = 10000.0
N_HEADS = 16
HEAD_DIM = 64
KV_HEADS = 4
Q_PER_KV = N_HEADS // KV_HEADS
Q_DIM = N_HEADS * HEAD_DIM
KV_DIM = KV_HEADS * HEAD_DIM
N_BRANCH = 3
CMP_BLOCK = 32
SEL_BLOCK = 64
TOP_N = 16
WINDOW = 512
FORCED_CUR = 3e4
FORCED_FIRST = 2e4
HG_HEADS = 8
HG_DK = 128
HG_DV = 128
HG_CHUNK = 64
HG_SUB = 16
HG_MAX_FACTOR_EXP = 60.0
HG_UNROLL_PROMPT = 4
HG_UNROLL_SAMPLE = 8
NEG = -1e30
LOG2_E = 1.4426950408889634

LANES = 128
SUBLANES = 8
ROW_TILE = 512
ATT_TILE = 256
V_AUG_ROWS = HEAD_DIM + 2 * SUBLANES
VMEM_LIMIT = 56 * 1024 * 1024


def _cparams(*sem):
    return pltpu.CompilerParams(dimension_semantics=sem, vmem_limit_bytes=VMEM_LIMIT)


def _full(shape):
    return pl.BlockSpec(shape, lambda *_: (0,) * len(shape))


def _rms(x, g):
    return x * lax.rsqrt(jnp.mean(x * x, axis=-1, keepdims=True) + RMS_EPS) * g


def _dot(a, b):
    return jnp.dot(a, b, preferred_element_type=F32)


def _dot_nt(a, b):
    return lax.dot_general(a, b, (((1,), (1,)), ((), ())), preferred_element_type=F32)


def _rope_table_kernel(inv_ref, invc_ref, cos_ref, sin_ref, cost_ref, sint_ref, *, pos0, period):
    i = pl.program_id(0)
    tm = cos_ref.shape[0]
    half = HEAD_DIM // 2
    row = i * tm + lax.broadcasted_iota(jnp.int32, (tm, LANES), 0)
    ang = (pos0 + row % period).astype(F32) * inv_ref[...]
    lane = lax.broadcasted_iota(jnp.int32, (tm, LANES), 1)
    cos_ref[...] = jnp.cos(ang)
    s = jnp.sin(ang)
    sin_ref[...] = jnp.where(lane % HEAD_DIM < half, -s, s)
    col = i * tm + lax.broadcasted_iota(jnp.int32, (HEAD_DIM, tm), 1)
    ang_t = (pos0 + col % period).astype(F32) * jnp.concatenate([invc_ref[...]] * (tm // LANES), axis=1)
    feat = lax.broadcasted_iota(jnp.int32, (HEAD_DIM, tm), 0)
    cost_ref[...] = jnp.cos(ang_t)
    st = jnp.sin(ang_t)
    sint_ref[...] = jnp.where(feat < half, -st, st)


def _rope_tables(rows, pos0, period):
    half = HEAD_DIM // 2
    inv = ROPE_THETA ** (-jnp.arange(half, dtype=F32) / half)
    inv_row = jnp.tile(inv, LANES // half)[None, :]
    inv_col = jnp.broadcast_to(jnp.tile(inv, HEAD_DIM // half)[:, None], (HEAD_DIM, LANES))
    kern = functools.partial(_rope_table_kernel, pos0=pos0, period=period)
    tm = min(ROW_TILE, rows)
    tok = pl.BlockSpec((tm, LANES), lambda i: (i, 0))
    feat = pl.BlockSpec((HEAD_DIM, tm), lambda i: (0, i))
    return pl.pallas_call(
        kern,
        grid=(rows // tm,),
        in_specs=[_full((1, LANES)), _full((HEAD_DIM, LANES))],
        out_specs=[tok, tok, feat, feat],
        out_shape=[jax.ShapeDtypeStruct((rows, LANES), F32)] * 2 + [jax.ShapeDtypeStruct((HEAD_DIM, rows), F32)] * 2,
        compiler_params=_cparams("parallel"),
        name="rope_tables",
    )(inv_row, inv_col)


def _rope_tok(y, cos, sin):
    lane = lax.broadcasted_iota(jnp.int32, y.shape, 1)
    first = lane % HEAD_DIM < HEAD_DIM // 2
    sw = jnp.where(first, pltpu.roll(y, LANES - HEAD_DIM // 2, 1), pltpu.roll(y, HEAD_DIM // 2, 1))
    return y * cos + sw * sin


def _rope_feat(y, cost, sint):
    return y * cost + pltpu.roll(y, HEAD_DIM // 2, 0) * sint


def _nsa_inproj_prompt_kernel(x_ref, g_ref, cos_ref, sin_ref, cost_ref, sint_ref, w_ref,
                              qt_ref, gatest_ref, cmp_ref, rowst_ref, wrowst_ref,
                              ksa_ref, vsa_ref, kwa_ref, vwa_ref):
    tm = x_ref.shape[0]
    tk = ksa_ref.shape[2]
    i = pl.program_id(1)
    xn = _rms(x_ref[...], g_ref[...]).astype(BF16)
    cos, sin = cos_ref[...], sin_ref[...]
    cost, sint = cost_ref[...], sint_ref[...]
    scale = HEAD_DIM ** -0.5 * LOG2_E

    qt = _dot_nt(w_ref[0:Q_DIM, :], xn)
    for h in range(N_HEADS):
        rs = slice(h * HEAD_DIM, (h + 1) * HEAD_DIM)
        qt_ref[rs, :] = (_rope_feat(qt[rs], cost, sint) * scale).astype(BF16)
    g0 = Q_DIM + 6 * KV_DIM
    gatest_ref[...] = jax.nn.sigmoid(_dot_nt(w_ref[g0:g0 + LANES, :], xn))

    ycmp = _dot_nt(xn, w_ref[Q_DIM:Q_DIM + 2 * KV_DIM, :])
    for c in range(2 * KV_DIM // LANES):
        cs = slice(c * LANES, (c + 1) * LANES)
        cmp_ref[:, cs] = _rope_tok(ycmp[:, cs], cos, sin) if c * LANES < KV_DIM else ycmp[:, cs]

    lane = lax.broadcasted_iota(jnp.int32, (tm, LANES), 1)
    lo = lane < HEAD_DIM
    pos = i * tm + lax.broadcasted_iota(jnp.int32, (tm, LANES), 0)
    sel_bias = jnp.where(lane - HEAD_DIM == pos // SEL_BLOCK, NEG, 0.0)
    for part, ref, fill in ((2, ksa_ref, sel_bias), (4, kwa_ref, 0.0)):
        yk = _dot_nt(xn, w_ref[Q_DIM + part * KV_DIM:Q_DIM + (part + 1) * KV_DIM, :])
        for c in range(KV_DIM // LANES):
            y = _rope_tok(yk[:, c * LANES:(c + 1) * LANES], cos, sin)
            pair = (jnp.where(lo, y, fill).astype(BF16), jnp.where(lo, pltpu.roll(y, HEAD_DIM, 1), fill).astype(BF16))
            for gg in range(2):
                for j in range(tm // tk):
                    ref[2 * c + gg, j] = pair[gg][j * tk:(j + 1) * tk, :]

    yt = _dot_nt(w_ref[Q_DIM:Q_DIM + 6 * KV_DIM, :], xn)
    ones = jnp.ones((V_AUG_ROWS - HEAD_DIM, tm), F32)
    for part in range(6):
        for g in range(KV_HEADS):
            r0 = part * KV_DIM + g * HEAD_DIM
            y = yt[r0:r0 + HEAD_DIM, :]
            if part % 2 == 0:
                y = _rope_feat(y, cost, sint)
            if part < 4:
                rowst_ref[r0:r0 + HEAD_DIM, :] = y
            else:
                wrowst_ref[r0 - 4 * KV_DIM:r0 - 4 * KV_DIM + HEAD_DIM, :] = y
            if part in (3, 5):
                ref = vsa_ref if part == 3 else vwa_ref
                ya = jnp.concatenate([y, ones], axis=0).astype(BF16)
                for j in range(tm // tk):
                    ref[g, j] = ya[:, j * tk:(j + 1) * tk]


def _nsa_inproj_prompt(x, g, tabs, w, batch, t_seq):
    m, d = x.shape
    tm = min(ROW_TILE, t_seq)
    tk = min(ATT_TILE, t_seq)
    nt = t_seq // tm
    cos, sin, cost, sint = tabs
    row = lambda cols: pl.BlockSpec((tm, cols), lambda b, i: (b * nt + i, 0))
    tok_tab = pl.BlockSpec((tm, LANES), lambda b, i: (i, 0))
    feat_tab = pl.BlockSpec((HEAD_DIM, tm), lambda b, i: (0, i))
    featmaj = lambda rows: pl.BlockSpec((None, rows, tm), lambda b, i: (b, 0, i))
    ktiles = pl.BlockSpec((None, KV_HEADS, tm // tk, tk, LANES), lambda b, i: (b, 0, i, 0, 0))
    vtiles = pl.BlockSpec((None, KV_HEADS, tm // tk, V_AUG_ROWS, tk), lambda b, i: (b, 0, i, 0, 0))
    kshape = jax.ShapeDtypeStruct((batch, KV_HEADS, t_seq // tk, tk, LANES), BF16)
    vshape = jax.ShapeDtypeStruct((batch, KV_HEADS, t_seq // tk, V_AUG_ROWS, tk), BF16)
    return pl.pallas_call(
        _nsa_inproj_prompt_kernel,
        grid=(batch, nt),
        in_specs=[row(d), _full((1, d)), tok_tab, tok_tab, feat_tab, feat_tab, _full(w.shape)],
        out_specs=[featmaj(Q_DIM), featmaj(LANES), row(2 * KV_DIM), featmaj(4 * KV_DIM), featmaj(2 * KV_DIM),
                   ktiles, vtiles, ktiles, vtiles],
        out_shape=[jax.ShapeDtypeStruct((batch, Q_DIM, t_seq), BF16), jax.ShapeDtypeStruct((batch, LANES, t_seq), F32),
                   jax.ShapeDtypeStruct((m, 2 * KV_DIM), F32),
                   jax.ShapeDtypeStruct((batch, 4 * KV_DIM, t_seq), F32),
                   jax.ShapeDtypeStruct((batch, 2 * KV_DIM, t_seq), F32), kshape, vshape, kshape, vshape],
        compiler_params=_cparams("parallel", "parallel"),
        name="nsa_inproj_prompt",
    )(x, g, cos, sin, cost, sint, w)


def _nsa_inproj_sample_kernel(x_ref, g_ref, cos_ref, sin_ref, w_ref, q_ref, gates_ref, rows_ref, wrows_ref):
    xn = _rms(x_ref[...], g_ref[...]).astype(BF16)
    cos, sin = cos_ref[...], sin_ref[...]
    scale = HEAD_DIM ** -0.5
    yq = _dot_nt(xn, w_ref[0:Q_DIM, :])
    for c in range(Q_DIM // LANES):
        cs = slice(c * LANES, (c + 1) * LANES)
        q_ref[:, cs] = (_rope_tok(yq[:, cs], cos, sin) * scale).astype(BF16)
    g0 = Q_DIM + 6 * KV_DIM
    gates_ref[...] = jax.nn.sigmoid(_dot_nt(xn, w_ref[g0:g0 + LANES, :]))
    ykv = _dot_nt(xn, w_ref[Q_DIM:g0, :])
    for c in range(6 * KV_DIM // LANES):
        cs = slice(c * LANES, (c + 1) * LANES)
        part = c * LANES // KV_DIM
        y = _rope_tok(ykv[:, cs], cos, sin) if part % 2 == 0 else ykv[:, cs]
        if part < 4:
            rows_ref[:, cs] = y
        else:
            wrows_ref[:, c * LANES - 4 * KV_DIM:(c + 1) * LANES - 4 * KV_DIM] = y


def _nsa_inproj_sample(x, g, tabs, w):
    m, d = x.shape
    tm = min(ROW_TILE, m)
    cos, sin = tabs[0], tabs[1]
    n_tab = cos.shape[0] // tm
    row = lambda cols: pl.BlockSpec((tm, cols), lambda i: (i, 0))
    tab = pl.BlockSpec((tm, LANES), lambda i: (i % n_tab, 0))
    return pl.pallas_call(
        _nsa_inproj_sample_kernel,
        grid=(m // tm,),
        in_specs=[row(d), _full((1, d)), tab, tab, _full(w.shape)],
        out_specs=[row(d), row(LANES), row(4 * KV_DIM), row(2 * KV_DIM)],
        out_shape=[jax.ShapeDtypeStruct((m, d), BF16), jax.ShapeDtypeStruct((m, LANES), F32),
                   jax.ShapeDtypeStruct((m, 4 * KV_DIM), F32), jax.ShapeDtypeStruct((m, 2 * KV_DIM), F32)],
        compiler_params=_cparams("parallel"),
        name="nsa_inproj_sample",
    )(x, g, cos, sin, w)


def _compress_rows(load, pe_ref, w1_ref, w2t_ref, kv, n, transposed=False):
    lane = lax.broadcasted_iota(jnp.int32, (n, LANES), 1)
    lo = lane < HEAD_DIM
    acc = jnp.zeros((KV_HEADS * n, w1_ref.shape[-1]), F32)
    lq = 4
    for l0 in range(0, CMP_BLOCK, lq):
        halves = [[], []]
        for l in range(l0, l0 + lq, 2):
            for hf in range(2):
                a = load(l, hf) + pe_ref[kv, l:l + 1, :]
                b = load(l + 1, hf) + pe_ref[kv, l + 1:l + 2, :]
                even = jnp.where(lo, a, pltpu.roll(b, HEAD_DIM, 1))
                odd = jnp.where(lo, pltpu.roll(a, HEAD_DIM, 1), b)
                halves[hf].append((even, odd))
        groups = []
        for g in range(KV_HEADS):
            hf, par = divmod(g, 2)
            groups.append(jnp.concatenate([pc[par] for pc in halves[hf]], axis=1))
        xg = jnp.concatenate(groups, axis=0).astype(BF16)
        acc = acc + _dot(xg, w1_ref[kv, l0 * HEAD_DIM:(l0 + lq) * HEAD_DIM, :])
    h = jax.nn.gelu(acc).astype(BF16)
    return _dot_nt(w2t_ref[kv], h) if transposed else _dot_nt(h, w2t_ref[kv])


def _compress_prompt_kernel(s0_ref, s1_ref, s2_ref, s3_ref, pe_ref, w1_ref, w2t_ref, kc_ref, vct_ref):
    n = kc_ref.shape[0]
    hn = n // 2
    srcs = ((s0_ref, s1_ref), (s2_ref, s3_ref))

    def loader(kv):
        def load(l, hf):
            ev = srcs[kv][hf][pl.ds(l, hn, stride=2 * CMP_BLOCK), :]
            od = srcs[kv][hf][pl.ds(CMP_BLOCK + l, hn, stride=2 * CMP_BLOCK), :]
            return jnp.concatenate([ev, od], axis=0)
        return load

    y = _compress_rows(loader(0), pe_ref, w1_ref, w2t_ref, 0, n)
    yt = _compress_rows(loader(1), pe_ref, w1_ref, w2t_ref, 1, n, transposed=True)
    for g in range(KV_HEADS):
        kc_ref[:, g * HEAD_DIM:(g + 1) * HEAD_DIM] = y[g * n:(g + 1) * n].astype(BF16)
        vct_ref[g * HEAD_DIM:(g + 1) * HEAD_DIM, :] = yt[:, g * n:(g + 1) * n].astype(BF16)


def _compress_prompt(cmp_rows, pe2, w1, w2t, batch, t_seq):
    n = t_seq // CMP_BLOCK
    chunk = lambda c: pl.BlockSpec((t_seq, LANES), lambda b: (b, c))
    return pl.pallas_call(
        _compress_prompt_kernel,
        grid=(batch,),
        in_specs=[chunk(0), chunk(1), chunk(2), chunk(3), _full(pe2.shape), _full(w1.shape), _full(w2t.shape)],
        out_specs=[pl.BlockSpec((None, n, KV_DIM), lambda b: (b, 0, 0)),
                   pl.BlockSpec((None, KV_DIM, n), lambda b: (b, 0, 0))],
        out_shape=[jax.ShapeDtypeStruct((batch, n, KV_DIM), BF16), jax.ShapeDtypeStruct((batch, KV_DIM, n), BF16)],
        compiler_params=_cparams("parallel"),
        name="nsa_compress_prompt",
    )(cmp_rows, cmp_rows, cmp_rows, cmp_rows, pe2, w1, w2t)


def _topk_not_selected(imp_t, n_live, rank_ref):
    n_sel, nq = imp_t.shape
    slabs = [imp_t[r0:r0 + SUBLANES] for r0 in range(0, n_sel, SUBLANES)]
    sub = lax.broadcasted_iota(jnp.int32, (SUBLANES, nq), 0)
    rank_ref[...] = jnp.zeros((n_sel, nq), F32)
    for c0 in range(0, n_sel, SUBLANES):
        @pl.when(c0 < n_live)
        def _():
            for v, slab in enumerate(slabs):
                r0 = v * SUBLANES
                cnt = jnp.zeros((SUBLANES, nq), F32)
                for bp in range(c0, c0 + SUBLANES):
                    other = imp_t[bp:bp + 1, :]
                    if bp < r0:
                        ahead = other >= slab
                    elif bp >= r0 + SUBLANES - 1:
                        ahead = other > slab
                    else:
                        ahead = (other > slab) | ((other == slab) & (bp - r0 < sub))
                    cnt = cnt + jnp.where(ahead, 1.0, 0.0)
                rank_ref[r0:r0 + SUBLANES, :] += cnt
    return jnp.where(rank_ref[...] < TOP_N, 0.0, 1.0)


def _nsa_attn_prompt_kernel(qt_ref, gatest_ref, kc_ref, vct_ref, ksa_ref, vsa_ref, kwa_ref, vwa_ref, o_ref,
                            acc_ref, m_ref, accw_ref, mw_ref, sa_ref, sb_ref, wb_ref, cb_ref, cv_ref, rank_ref):
    tq = qt_ref.shape[1]
    tk = ksa_ref.shape[2]
    n_cmp = kc_ref.shape[0]
    n_sel = n_cmp // 2
    rq = Q_PER_KV * tq
    i = pl.program_id(1)
    t0 = i * tq
    t_col = t0 + lax.broadcasted_iota(jnp.int32, (1, rq), 1) % tq
    key_row = lax.broadcasted_iota(jnp.int32, (tk, rq), 0)
    n_wt = -(-(WINDOW - 1) // tk) + 1
    kt_diag = t0 // tk
    wkt = [kt_diag - j for j in range(n_wt)]

    for j, kt in enumerate(wkt):
        kpos = kt * tk + key_row
        wb_ref[j] = jnp.where((kpos >= 0) & (kpos <= t_col) & (kpos > t_col - WINDOW), 0.0, NEG)
    nrow = lax.broadcasted_iota(jnp.int32, (n_cmp, rq), 0)
    cblk = jnp.where(nrow < n_sel, 2 * nrow, 2 * (nrow - n_sel) + 1)
    cvis = (cblk + 1) * CMP_BLOCK - 1 <= t_col
    cb_ref[...] = jnp.where(cvis, 0.0, NEG)
    cv_ref[...] = jnp.where(cvis, 1.0, 0.0)

    def online_update(s_ref, vt, bias, m_ref, acc_ref):
        s = s_ref[...]
        if bias is not None:
            s = s + bias
        m_old = m_ref[...]
        m_new = jnp.maximum(m_old, jnp.max(s, axis=0, keepdims=True))
        p = jnp.exp2(s - m_new)
        acc_ref[...] = acc_ref[...] * jnp.exp2(m_old - m_new) + _dot(vt, p.astype(BF16))
        m_ref[...] = m_new

    def normalized(acc):
        return acc[:HEAD_DIM] * (1.0 / jnp.maximum(acc[HEAD_DIM:HEAD_DIM + 1], 1e-30))

    for g in range(KV_HEADS):
        qg = jnp.concatenate(
            [qt_ref[(Q_PER_KV * g + r) * HEAD_DIM:(Q_PER_KV * g + r + 1) * HEAD_DIM, :] for r in range(Q_PER_KV)],
            axis=1)

        sc = _dot(kc_ref[:, g * HEAD_DIM:(g + 1) * HEAD_DIM], qg) + cb_ref[...]
        e = jnp.exp2(sc - jnp.max(sc, axis=0, keepdims=True)) * cv_ref[...]
        pc = e * (1.0 / jnp.maximum(jnp.sum(e, axis=0, keepdims=True), 1e-30))
        o_c = _dot(vct_ref[g * HEAD_DIM:(g + 1) * HEAD_DIM, :], pc.astype(BF16))

        pq = pc[:, 0:tq]
        for r in range(1, Q_PER_KV):
            pq = pq + pc[:, r * tq:(r + 1) * tq]
        imp = pq[:n_sel] + pq[n_sel:]
        blk = lax.broadcasted_iota(jnp.int32, (n_sel, tq), 0)
        cur = (t0 + lax.broadcasted_iota(jnp.int32, (n_sel, tq), 1)) // SEL_BLOCK
        imp = jnp.where(blk > cur, -1.0, imp)
        imp = jnp.where(blk == 0, FORCED_FIRST, imp)
        imp = jnp.where(blk == cur, FORCED_CUR, imp)
        n_live = (t0 + tq - 1) // SEL_BLOCK + 1
        notsel = _topk_not_selected(imp, n_live, rank_ref)
        if n_sel < HEAD_DIM:
            notsel = jnp.concatenate([notsel, jnp.zeros((HEAD_DIM - n_sel, tq), F32)], axis=0)
        qa = jnp.concatenate([qg, jnp.concatenate([notsel.astype(BF16)] * Q_PER_KV, axis=1)], axis=0)

        for ref in (m_ref, mw_ref):
            ref[...] = jnp.full((1, rq), NEG, F32)
        for ref in (acc_ref, accw_ref):
            ref[...] = jnp.zeros((V_AUG_ROWS, rq), F32)
        slots = [sa_ref, sb_ref]
        qw = jnp.concatenate([qg, jnp.zeros_like(qg)], axis=0)

        slots[0][...] = _dot(kwa_ref[g, wkt[0]], qw)
        for j, kt in enumerate(wkt):
            if j + 1 < n_wt:
                slots[1 - j % 2][...] = _dot(kwa_ref[g, jnp.maximum(wkt[j + 1], 0)], qw)
            else:
                slots[1 - j % 2][...] = _dot(ksa_ref[g, 0], qa)
            online_update(slots[j % 2], vwa_ref[g, jnp.maximum(kt, 0)], wb_ref[j], mw_ref, accw_ref)
        o_w = normalized(accw_ref[...])

        s_even, s_odd = slots[n_wt % 2], slots[1 - n_wt % 2]

        def sel_pair(j, carry):
            s_odd[...] = _dot(ksa_ref[g, 2 * j + 1], qa)
            online_update(s_even, vsa_ref[g, 2 * j], None, m_ref, acc_ref)
            s_even[...] = _dot(ksa_ref[g, 2 * j + 2], qa)
            online_update(s_odd, vsa_ref[g, 2 * j + 1], None, m_ref, acc_ref)
            return carry

        lax.fori_loop(0, kt_diag // 2, sel_pair, 0)

        @pl.when(kt_diag % 2 == 1)
        def _():
            s_odd[...] = _dot(ksa_ref[g, kt_diag], qa)
            online_update(s_even, vsa_ref[g, kt_diag - 1], None, m_ref, acc_ref)
            online_update(s_odd, vsa_ref[g, kt_diag], wb_ref[0], m_ref, acc_ref)

        @pl.when(kt_diag % 2 == 0)
        def _():
            online_update(s_even, vsa_ref[g, kt_diag], wb_ref[0], m_ref, acc_ref)

        o_s = normalized(acc_ref[...])

        outs = []
        for r in range(Q_PER_KV):
            h = Q_PER_KV * g + r
            cs = slice(r * tq, (r + 1) * tq)
            gc = gatest_ref[N_BRANCH * h:N_BRANCH * h + 1, :]
            gs = gatest_ref[N_BRANCH * h + 1:N_BRANCH * h + 2, :]
            gw = gatest_ref[N_BRANCH * h + 2:N_BRANCH * h + 3, :]
            outs.append(gc * o_c[:, cs] + gs * o_s[:, cs] + gw * o_w[:, cs])
        for pr in range(Q_PER_KV // 2):
            c0 = (Q_PER_KV * g + 2 * pr) * HEAD_DIM
            o_ref[:, c0:c0 + LANES] = jnp.concatenate(outs[2 * pr:2 * pr + 2], axis=0).T.astype(BF16)


def _nsa_attn_prompt(qt, gatest, kc, vct, ksa, vsa, kwa, vwa, batch, t_seq):
    d = qt.shape[1]
    tq = min(ATT_TILE, t_seq)
    nq = t_seq // tq
    n_cmp = t_seq // CMP_BLOCK
    tk = ksa.shape[3]
    featmaj = lambda rows: pl.BlockSpec((None, rows, tq), lambda b, i: (b, 0, i))
    seq = lambda shape: pl.BlockSpec((None,) + shape[1:], lambda b, i: (b, 0, 0, 0, 0))
    return pl.pallas_call(
        _nsa_attn_prompt_kernel,
        grid=(batch, nq),
        in_specs=[featmaj(d), featmaj(LANES),
                  pl.BlockSpec((None, n_cmp, KV_DIM), lambda b, i: (b, 0, 0)),
                  pl.BlockSpec((None, KV_DIM, n_cmp), lambda b, i: (b, 0, 0)),
                  seq(ksa.shape), seq(vsa.shape), seq(kwa.shape), seq(vwa.shape)],
        out_specs=pl.BlockSpec((tq, d), lambda b, i: (b * nq + i, 0)),
        out_shape=jax.ShapeDtypeStruct((batch * t_seq, d), BF16),
        scratch_shapes=[pltpu.VMEM((V_AUG_ROWS, Q_PER_KV * tq), F32), pltpu.VMEM((1, Q_PER_KV * tq), F32),
                        pltpu.VMEM((V_AUG_ROWS, Q_PER_KV * tq), F32), pltpu.VMEM((1, Q_PER_KV * tq), F32),
                        pltpu.VMEM((tk, Q_PER_KV * tq), F32), pltpu.VMEM((tk, Q_PER_KV * tq), F32),
                        pltpu.VMEM((-(-(WINDOW - 1) // tk) + 1, tk, Q_PER_KV * tq), F32),
                        pltpu.VMEM((n_cmp, Q_PER_KV * tq), F32), pltpu.VMEM((n_cmp, Q_PER_KV * tq), F32),
                        pltpu.VMEM((n_cmp // 2, tq), F32)],
        compiler_params=_cparams("parallel", "arbitrary"),
        name="nsa_attn_prompt",
    )(qt, gatest, kc, vct, ksa, vsa, kwa, vwa)


def _compress_sample_kernel(pt_ref, *refs, n_pages, page, t_new, n_cmp):
    del pt_ref
    pages = refs[:n_pages]
    new_ref, pe_ref, w1_ref, w2t_ref, kc_ref, vc_ref = refs[n_pages:n_pages + 6]
    bufs = refs[n_pages + 6:]
    n_pad = bufs[0].shape[1]
    n_past = n_pages * page // CMP_BLOCK
    grp = SUBLANES * CMP_BLOCK
    ppg = grp // page
    tt = lax.broadcasted_iota(jnp.int32, (grp, grp), 0)
    perm_t = jnp.where(lax.broadcasted_iota(jnp.int32, (grp, grp), 1)
                       == (tt % CMP_BLOCK) * SUBLANES + tt // CMP_BLOCK, 1.0, 0.0).astype(BF16)
    for c, buf_ref in enumerate(bufs):
        cs = slice(c * LANES, (c + 1) * LANES)
        for pp in range(n_pages // ppg):
            xt = jnp.concatenate([pages[pp * ppg + j][cs, :] for j in range(ppg)], axis=1)
            xp = _dot(xt.astype(BF16), perm_t).T
            for l in range(CMP_BLOCK):
                buf_ref[l, pp * SUBLANES:(pp + 1) * SUBLANES, :] = xp[l * SUBLANES:(l + 1) * SUBLANES, :]
        buf_ref[:, n_past:, :] = jnp.zeros((CMP_BLOCK, n_pad - n_past, LANES), F32)
        for l in range(t_new):
            buf_ref[l, n_past:n_past + 1, :] = new_ref[l:l + 1, cs]
    n_out = kc_ref.shape[0]
    for kv, out_ref in ((0, kc_ref), (1, vc_ref)):
        def load(l, hf, kv=kv):
            return bufs[2 * kv + hf][l]
        y = _compress_rows(load, pe_ref, w1_ref, w2t_ref, kv, n_pad)
        valid = lax.broadcasted_iota(jnp.int32, (n_pad, HEAD_DIM), 0) < n_cmp
        for g in range(KV_HEADS):
            out_ref[0:n_pad, g * HEAD_DIM:(g + 1) * HEAD_DIM] = jnp.where(
                valid, y[g * n_pad:(g + 1) * n_pad], 0.0).astype(BF16)
        out_ref[n_pad:, :] = jnp.zeros((n_out - n_pad, KV_DIM), BF16)


def _page_specs(layer, n_pages, page, feat_block):
    half = 2 * KV_DIM
    return [pl.BlockSpec((None, None, half, page),
                         functools.partial(lambda b, pt, p: (layer, pt[b, p], feat_block, 0), p=p))
            for p in range(n_pages)]


def _compress_sample(page_table, cache_t, layer, rows_new, pe2, w1, w2t, t_new):
    dec_b, n_pages = page_table.shape
    page = cache_t.shape[-1]
    past = n_pages * page
    l_pad = -(-(past + t_new) // SEL_BLOCK) * SEL_BLOCK
    n_cmp = l_pad // CMP_BLOCK
    n_pad = -(-n_cmp // SUBLANES) * SUBLANES
    n_out = LANES
    half = 2 * KV_DIM
    kern = functools.partial(_compress_sample_kernel, n_pages=n_pages, page=page, t_new=t_new, n_cmp=n_cmp)
    grid_spec = pltpu.PrefetchScalarGridSpec(
        num_scalar_prefetch=1,
        grid=(dec_b,),
        in_specs=_page_specs(layer, n_pages, page, 0) + [
            pl.BlockSpec((t_new, half), lambda b, pt: (b, 0)),
            pl.BlockSpec(pe2.shape, lambda b, pt: (0, 0, 0)),
            pl.BlockSpec(w1.shape, lambda b, pt: (0, 0, 0)),
            pl.BlockSpec(w2t.shape, lambda b, pt: (0, 0, 0))],
        out_specs=[pl.BlockSpec((None, n_out, KV_DIM), lambda b, pt: (b, 0, 0))] * 2,
        scratch_shapes=[pltpu.VMEM((CMP_BLOCK, n_pad, LANES), F32)] * (half // LANES),
    )
    return pl.pallas_call(
        kern,
        grid_spec=grid_spec,
        out_shape=[jax.ShapeDtypeStruct((dec_b, n_out, KV_DIM), BF16)] * 2,
        compiler_params=_cparams("arbitrary"),
        name="nsa_compress_sample",
    )(page_table, *([cache_t] * n_pages), rows_new, pe2, w1, w2t), n_cmp


def _nsa_attn_sample_kernel(pt_ref, *refs, n_pages, page, t_new, n_cmp):
    del pt_ref
    pages = refs[:n_pages]
    (kc_ref, vc_ref, q_ref, new_ref, wnew_ref, win_ref, gates_ref,
     o_ref, nwin_ref, kst_ref, vst_ref) = refs[n_pages:]
    past = n_pages * page
    n_buf = win_ref.shape[1]
    n_sel = n_cmp // 2
    gq = Q_PER_KV * t_new
    rq = KV_HEADS * gq
    nk = past + LANES
    nw = n_buf + LANES

    for p in range(n_pages):
        blk = pages[p][...]
        kst_ref[:, p * page:(p + 1) * page] = blk[:KV_DIM].astype(BF16)
        vst_ref[:, p * page:(p + 1) * page] = blk[KV_DIM:].astype(BF16)
    pad = jnp.zeros((LANES - t_new, KV_DIM), F32)
    new = new_ref[...]
    wnew = wnew_ref[...]
    knew = jnp.concatenate([new[:, :KV_DIM], pad], axis=0).astype(BF16)
    vnew = jnp.concatenate([new[:, KV_DIM:], pad], axis=0).astype(BF16)
    kwnew = jnp.concatenate([wnew[:, :KV_DIM], pad], axis=0).astype(BF16)
    vwnew = jnp.concatenate([wnew[:, KV_DIM:], pad], axis=0).astype(BF16)
    win = win_ref[...]
    kwt = win[:KV_DIM].astype(BF16)
    vwt = win[KV_DIM:].astype(BF16)

    rolled = pltpu.roll(win, n_buf - t_new, 1)
    wnew_t = jnp.concatenate([jnp.zeros((LANES - t_new, 2 * KV_DIM), F32), wnew], axis=0).T
    tail_lane = lax.broadcasted_iota(jnp.int32, (2 * KV_DIM, LANES), 1)
    nwin_ref[:, 0:n_buf - LANES] = rolled[:, 0:n_buf - LANES]
    nwin_ref[:, n_buf - LANES:] = jnp.where(tail_lane >= LANES - t_new, wnew_t, rolled[:, n_buf - LANES:])

    q = q_ref[...]
    lane_g = lax.broadcasted_iota(jnp.int32, (t_new, KV_DIM), 1) // HEAD_DIM
    pieces = []
    for g in range(KV_HEADS):
        for j in range(Q_PER_KV):
            qj = jnp.concatenate([q[:, (Q_PER_KV * gg + j) * HEAD_DIM:(Q_PER_KV * gg + j + 1) * HEAD_DIM]
                                  for gg in range(KV_HEADS)], axis=1)
            pieces.append(jnp.where(lane_g == g, qj, jnp.zeros_like(qj)))
    qbd = jnp.concatenate(pieces, axis=0)
    t_row = past + lax.broadcasted_iota(jnp.int32, (rq, 1), 0) % t_new

    def softmax(s, mask):
        s = jnp.where(mask, s, NEG)
        e = jnp.where(mask, jnp.exp(s - jnp.max(s, axis=1, keepdims=True)), 0.0)
        return e * (1.0 / jnp.maximum(jnp.sum(e, axis=1, keepdims=True), 1e-30))

    ncl = kc_ref.shape[0]
    sc = _dot_nt(qbd, kc_ref[...])
    sw = jnp.concatenate([_dot(qbd, kwt), _dot_nt(qbd, kwnew)], axis=1)
    ss = jnp.concatenate([_dot(qbd, kst_ref[...]), _dot_nt(qbd, knew)], axis=1)

    cidx = lax.broadcasted_iota(jnp.int32, (rq, ncl), 1)
    pc = softmax(sc, (cidx < n_cmp) & ((cidx + 1) * CMP_BLOCK - 1 <= t_row))
    oc = _dot(pc.astype(BF16), vc_ref[...])

    wpos = past - n_buf + lax.broadcasted_iota(jnp.int32, (rq, nw), 1)
    pw = softmax(sw, (wpos <= t_row) & (wpos > t_row - WINDOW)).astype(BF16)
    ow = _dot_nt(pw[:, :n_buf], vwt) + _dot(pw[:, n_buf:], vwnew)

    groups = []
    for g in range(KV_HEADS):
        pg = pc[g * gq:g * gq + t_new]
        for j in range(1, Q_PER_KV):
            pg = pg + pc[g * gq + j * t_new:g * gq + (j + 1) * t_new]
        groups.append(pg)
    pg = jnp.concatenate(groups, axis=0)
    nr = KV_HEADS * t_new
    imp = pg + pltpu.roll(pg, ncl - 1, 1)
    lane = lax.broadcasted_iota(jnp.int32, (nr, ncl), 1)
    blk = lane // 2
    cur = (past + lax.broadcasted_iota(jnp.int32, (nr, ncl), 0) % t_new) // SEL_BLOCK
    imp = jnp.where(blk > cur, -1.0, imp)
    imp = jnp.where(blk == 0, FORCED_FIRST, imp)
    imp = jnp.where(blk == cur, FORCED_CUR, imp)
    rank = jnp.zeros((nr, ncl), jnp.int32)
    for bp in range(n_sel):
        other = imp[:, 2 * bp:2 * bp + 1]
        rank = rank + ((other > imp) | ((other == imp) & (bp < blk))).astype(jnp.int32)
    sel = jnp.where((rank < TOP_N) & (lane % 2 == 0) & (blk < n_sel), 1.0, 0.0)
    sel_rows = jnp.concatenate([sel[g * t_new:(g + 1) * t_new] for g in range(KV_HEADS) for _ in range(Q_PER_KV)],
                               axis=0).astype(BF16)
    expand = jnp.where(lax.broadcasted_iota(jnp.int32, (ncl, nk), 0)
                       == 2 * (lax.broadcasted_iota(jnp.int32, (ncl, nk), 1) // SEL_BLOCK), 1.0, 0.0).astype(BF16)
    selmask = _dot(sel_rows, expand) > 0.5

    kpos = lax.broadcasted_iota(jnp.int32, (rq, nk), 1)
    ps = softmax(ss, selmask & (kpos <= t_row)).astype(BF16)
    os_ = _dot_nt(ps[:, :past], vst_ref[...]) + _dot(ps[:, past:], vnew)

    gates = gates_ref[...]
    for g in range(KV_HEADS):
        for j in range(Q_PER_KV):
            h = Q_PER_KV * g + j
            rs = slice(g * gq + j * t_new, g * gq + (j + 1) * t_new)
            cs = slice(g * HEAD_DIM, (g + 1) * HEAD_DIM)
            gc = gates[:, N_BRANCH * h:N_BRANCH * h + 1]
            gs = gates[:, N_BRANCH * h + 1:N_BRANCH * h + 2]
            gw = gates[:, N_BRANCH * h + 2:N_BRANCH * h + 3]
            o_ref[:, h * HEAD_DIM:(h + 1) * HEAD_DIM] = (
                gc * oc[rs, cs] + gs * os_[rs, cs] + gw * ow[rs, cs]).astype(BF16)


def _nsa_attn_sample(page_table, cache_t, win_t, layer, kc, vc, q, rows_new, wrows_new, gates, t_new, n_cmp):
    dec_b, n_pages = page_table.shape
    page = cache_t.shape[-1]
    past = n_pages * page
    n_buf = win_t.shape[-1]
    half = 2 * KV_DIM
    d = q.shape[1]
    ncl = kc.shape[1]
    kern = functools.partial(_nsa_attn_sample_kernel, n_pages=n_pages, page=page, t_new=t_new, n_cmp=n_cmp)
    grid_spec = pltpu.PrefetchScalarGridSpec(
        num_scalar_prefetch=1,
        grid=(dec_b,),
        in_specs=_page_specs(layer, n_pages, page, 1) + [
            pl.BlockSpec((None, ncl, KV_DIM), lambda b, pt: (b, 0, 0)),
            pl.BlockSpec((None, ncl, KV_DIM), lambda b, pt: (b, 0, 0)),
            pl.BlockSpec((t_new, d), lambda b, pt: (b, 0)),
            pl.BlockSpec((t_new, half), lambda b, pt: (b, 1)),
            pl.BlockSpec((t_new, half), lambda b, pt: (b, 0)),
            pl.BlockSpec((None, None, half, n_buf), lambda b, pt: (layer, b, 0, 0)),
            pl.BlockSpec((t_new, LANES), lambda b, pt: (b, 0))],
        out_specs=[pl.BlockSpec((t_new, d), lambda b, pt: (b, 0)),
                   pl.BlockSpec((None, half, n_buf), lambda b, pt: (b, 0, 0))],
        scratch_shapes=[pltpu.VMEM((KV_DIM, past), BF16), pltpu.VMEM((KV_DIM, past), BF16)],
    )
    return pl.pallas_call(
        kern,
        grid_spec=grid_spec,
        out_shape=[jax.ShapeDtypeStruct((dec_b * t_new, d), BF16), jax.ShapeDtypeStruct((dec_b, half, n_buf), F32)],
        compiler_params=_cparams("arbitrary"),
        name="nsa_attn_sample",
    )(page_table, *([cache_t] * n_pages), kc, vc, q, rows_new, wrows_new, win_t, gates)


def _post_mixer_kernel(x_ref, o_ref, wo_ref, gm_ref, w1_ref, w2_ref, p_ref, gp_ref, wg_ref, wp_ref, gf_ref,
                       y_ref, xn_ref, acc_ref, *, head_major, final_norm):
    j = pl.program_id(1)

    @pl.when(j == 0)
    def _():
        if head_major:
            o = jnp.concatenate([o_ref[h] for h in range(o_ref.shape[0])], axis=1)
        else:
            o = o_ref[...]
        x = x_ref[...] + _dot(o, wo_ref[...])
        xn_ref[...] = _rms(x, gm_ref[...]).astype(BF16)
        acc_ref[...] = x

    h = jnp.maximum(_dot(xn_ref[...], w1_ref[...]), 0.0)
    acc_ref[...] += _dot((h * h).astype(BF16), w2_ref[...])

    @pl.when(j == pl.num_programs(1) - 1)
    def _():
        x = acc_ref[...]
        gate = jax.nn.sigmoid(_dot(_rms(x, gp_ref[...]).astype(BF16), wg_ref[...]))
        y = x + _dot(p_ref[...].astype(BF16), wp_ref[...]) * gate
        if final_norm:
            y = _rms(y, gf_ref[...])
        y_ref[...] = y


def _post_mixer(x, o, w_out, g_mlp, w1, w2, p_all, layer, g_ple, wg, wp, gf, final_norm):
    m, d = x.shape
    ff = w1.shape[1]
    tm, tf = min(ROW_TILE, m), 1024
    head_major = o.ndim == 3
    row = pl.BlockSpec((tm, d), lambda i, j: (i, 0))
    vec = pl.BlockSpec((1, d), lambda i, j: (0, 0))
    full = lambda a: pl.BlockSpec(a.shape, lambda i, j: (0,) * a.ndim)
    o_spec = pl.BlockSpec((o.shape[0], tm, o.shape[2]), lambda i, j: (0, i, 0)) if head_major else row
    return pl.pallas_call(
        functools.partial(_post_mixer_kernel, head_major=head_major, final_norm=final_norm),
        grid=(m // tm, ff // tf),
        in_specs=[row, o_spec, full(w_out), vec,
                  pl.BlockSpec((d, tf), lambda i, j: (0, j)), pl.BlockSpec((tf, d), lambda i, j: (j, 0)),
                  pl.BlockSpec((None, tm, p_all.shape[-1]), lambda i, j: (layer, i, 0)), vec,
                  full(wg), full(wp), vec],
        out_specs=row,
        out_shape=jax.ShapeDtypeStruct((m, d), F32),
        scratch_shapes=[pltpu.VMEM((tm, d), BF16), pltpu.VMEM((tm, d), F32)],
        compiler_params=_cparams("parallel", "arbitrary"),
        name="post_mixer",
    )(x, o, w_out, g_mlp, w1, w2, p_all, g_ple, wg, wp, gf)


def _hg_inproj_kernel(x_ref, g_ref, lbl_ref, w_ref, q_ref, f_ref, k_ref, v_ref, gs_ref, *, layer):
    d = x_ref.shape[1]
    xn = _rms(x_ref[...], g_ref[...]).astype(BF16)

    def put(ref, val):
        for h in range(HG_HEADS):
            ref[h] = val[:, h * HG_DK:(h + 1) * HG_DK]

    lg = lbl_ref[...]
    e = jnp.exp(lg - jnp.max(lg, axis=0, keepdims=True))
    w = e / jnp.sum(e, axis=0, keepdims=True)
    cs = w[0:1]
    for r in range(1, layer + 1):
        cs = cs + w[r:r + 1]
    lb = cs - w[0:1]

    y = _dot(xn, w_ref[:, 0:d])
    put(q_ref, jax.nn.silu(y) * HG_DK ** -0.5)
    y = _dot(xn, w_ref[:, d:2 * d])
    sg = jax.nn.sigmoid(y)
    put(f_ref, jnp.log(lb + (1.0 - lb) * sg))
    put(k_ref, (1.0 - lb) * (1.0 - sg))
    put(v_ref, _dot(xn, w_ref[:, 2 * d:3 * d]))
    put(gs_ref, jax.nn.silu(_dot(xn, w_ref[:, 3 * d:4 * d])))


def _hg_inproj(x, g, lb_logits, w, layer):
    m, d = x.shape
    tm = min(ROW_TILE, m)
    head_major = pl.BlockSpec((HG_HEADS, tm, HG_DK), lambda i: (0, i, 0))
    return pl.pallas_call(
        functools.partial(_hg_inproj_kernel, layer=layer),
        grid=(m // tm,),
        in_specs=[pl.BlockSpec((tm, d), lambda i: (i, 0)), _full((1, d)), _full(lb_logits.shape), _full(w.shape)],
        out_specs=[head_major] * 5,
        out_shape=[jax.ShapeDtypeStruct((HG_HEADS, m, HG_DK), F32)] * 5,
        compiler_params=_cparams("parallel"),
        name="hgrn_inproj",
    )(x, g, lb_logits, w)


def _cumsum_rows(f):
    c = f.shape[0]
    row = lax.broadcasted_iota(jnp.int32, f.shape, 0)
    s = 1
    while s < c:
        f = f + jnp.where(row >= s, pltpu.roll(f, s, 0), 0.0)
        s *= 2
    return f


def _hg_out(o, ng, gs):
    return (_rms(o, ng) * gs).astype(BF16)


def _hg_diag_terms(q, k, b, sub):
    trow = lax.broadcasted_iota(jnp.int32, (sub, HG_DK), 0)
    terms = []
    for r0 in range(0, q.shape[0], sub):
        qi, ki, bi = q[r0:r0 + sub], k[r0:r0 + sub], b[r0:r0 + sub]
        for s in range(sub):
            causal = trow >= s
            dec = jnp.exp(jnp.where(causal, bi - bi[s:s + 1], 0.0))
            terms.append(jnp.where(causal, qi * ki[s:s + 1] * dec, 0.0).astype(BF16))
    return jnp.concatenate(terms, axis=0)


def _hg_offdiag(q, k, v, b, sub):
    c = q.shape[0]
    qps, kps, vps, spans = [], [], [], []
    off = 0
    for r0 in range(sub, c, sub):
        bs = b[r0 - 1:r0]
        qps.append(q[r0:r0 + sub] * jnp.exp(b[r0:r0 + sub] - bs))
        kps.append(k[:r0] * jnp.exp(bs - b[:r0]))
        vps.append(v[:r0])
        spans.append((off, off + r0))
        off += r0
    a = _dot_nt(jnp.concatenate(qps, axis=0).astype(BF16), jnp.concatenate(kps, axis=0).astype(BF16))
    row_blk = lax.broadcasted_iota(jnp.int32, a.shape, 0) // sub
    col = lax.broadcasted_iota(jnp.int32, a.shape, 1)
    keep = jnp.zeros(a.shape, jnp.bool_)
    for i, (lo, hi) in enumerate(spans):
        keep = keep | ((row_blk == i) & (col >= lo) & (col < hi))
    return _dot(jnp.where(keep, a, 0.0).astype(BF16), jnp.concatenate(vps, axis=0).astype(BF16))


def _hg_factored_scores(q, k, b, sub):
    c = q.shape[0]
    qps, kps, spans = [], [], []
    off = 0
    for r0 in range(0, c, sub):
        hi = r0 + sub
        bs = b[r0 - 1:r0] if r0 > 0 else jnp.zeros((1, HG_DK), F32)
        qps.append(q[r0:hi] * jnp.exp(b[r0:hi] - bs))
        kps.append(k[:hi] * jnp.exp(bs - b[:hi]))
        spans.append((off, off + hi))
        off += hi
    a = _dot_nt(jnp.concatenate(qps, axis=0).astype(BF16), jnp.concatenate(kps, axis=0).astype(BF16))
    row = lax.broadcasted_iota(jnp.int32, a.shape, 0)
    col = lax.broadcasted_iota(jnp.int32, a.shape, 1)
    keep = jnp.zeros(a.shape, jnp.bool_)
    for i, (lo, hi) in enumerate(spans):
        keep = keep | ((row // sub == i) & (col >= lo) & (col < hi) & (col - lo <= row))
    return jnp.where(keep, a, 0.0).astype(BF16)


def _hg_stacked_values(v, sub):
    return jnp.concatenate([v[:r0 + sub] for r0 in range(0, v.shape[0], sub)], axis=0).astype(BF16)


def _hg_scan_prompt_kernel(q_ref, f_ref, k_ref, v_ref, gs_ref, ng_ref, o_ref, s_ref,
                           st_ref, b_ref, t_ref, a_ref, i_ref):
    c = pl.program_id(1)
    rows = q_ref.shape[1]
    n_chunk = rows // HG_CHUNK
    tr = HG_CHUNK * HG_SUB
    ones = jnp.ones((HG_DK, HG_DV), BF16)

    @pl.when(c == 0)
    def _():
        st_ref[...] = jnp.zeros(st_ref.shape, F32)

    def heads(it, carry):
        drop = jnp.zeros((1, HG_DK), F32)
        for hh in range(HG_UNROLL_PROMPT):
            h = it * HG_UNROLL_PROMPT + hh
            for ci in range(n_chunk):
                u = hh * n_chunk + ci
                b = _cumsum_rows(f_ref[h, pl.ds(ci * HG_CHUNK, HG_CHUNK), :])
                b_ref[u] = b
                for r0 in range(0, HG_CHUNK, HG_SUB):
                    last = b[r0 + HG_SUB - 1:r0 + HG_SUB]
                    drop = jnp.maximum(drop, -last if r0 == 0 else b[r0 - 1:r0] - last)
        factorable = jnp.max(drop) < HG_MAX_FACTOR_EXP

        @pl.when(factorable)
        def _():
            units = [(it * HG_UNROLL_PROMPT + hh, hh * n_chunk + ci, pl.ds(ci * HG_CHUNK, HG_CHUNK))
                     for hh in range(HG_UNROLL_PROMPT) for ci in range(n_chunk)]
            scores = [_hg_factored_scores(q_ref[h, rs, :], k_ref[h, rs, :], b_ref[u], HG_SUB) for h, u, rs in units]
            for (h, u, rs), a in zip(units, scores):
                i_ref[u] = _dot(a, _hg_stacked_values(v_ref[h, rs, :], HG_SUB))

        @pl.when(jnp.logical_not(factorable))
        def _():
            for hh in range(HG_UNROLL_PROMPT):
                h = it * HG_UNROLL_PROMPT + hh
                for ci in range(n_chunk):
                    u = hh * n_chunk + ci
                    rs = pl.ds(ci * HG_CHUNK, HG_CHUNK)
                    t_ref[u * tr:(u + 1) * tr, :] = _hg_diag_terms(q_ref[h, rs, :], k_ref[h, rs, :], b_ref[u], HG_SUB)
            a_ref[...] = _dot(t_ref[...], ones)
            for hh in range(HG_UNROLL_PROMPT):
                h = it * HG_UNROLL_PROMPT + hh
                for ci in range(n_chunk):
                    u = hh * n_chunk + ci
                    rs = pl.ds(ci * HG_CHUNK, HG_CHUNK)
                    q, k, v, b = q_ref[h, rs, :], k_ref[h, rs, :], v_ref[h, rs, :], b_ref[u]
                    off = _hg_offdiag(q, k, v, b, HG_SUB)
                    for r0 in range(0, HG_CHUNK, HG_SUB):
                        oi = off[r0 - HG_SUB:r0] if r0 > 0 else jnp.zeros((HG_SUB, HG_DV), F32)
                        for s in range(HG_SUB):
                            a0 = u * tr + (r0 + s) * HG_SUB
                            oi = oi + a_ref[a0:a0 + HG_SUB, :] * v[r0 + s:r0 + s + 1]
                        i_ref[u, r0:r0 + HG_SUB, :] = oi

        for ci in range(n_chunk):
            rs = pl.ds(ci * HG_CHUNK, HG_CHUNK)
            hs = [(it * HG_UNROLL_PROMPT + hh, hh * n_chunk + ci) for hh in range(HG_UNROLL_PROMPT)]
            inter = [_dot_nt((q_ref[h, rs, :] * jnp.exp(b_ref[u])).astype(BF16), st_ref[h].astype(BF16))
                     for h, u in hs]
            grow = []
            for h, u in hs:
                b = b_ref[u]
                kd = (k_ref[h, rs, :] * jnp.exp(b[HG_CHUNK - 1:HG_CHUNK] - b)).astype(BF16)
                grow.append(lax.dot_general(v_ref[h, rs, :].astype(BF16), kd, (((0,), (0,)), ((), ())),
                                            preferred_element_type=F32))
            for (h, u), oi, gr in zip(hs, inter, grow):
                o_ref[h, rs, :] = _hg_out(i_ref[u] + oi, ng_ref[...], gs_ref[h, rs, :])
                st_ref[h] = st_ref[h] * jnp.exp(b_ref[u, HG_CHUNK - 1:HG_CHUNK, :]) + gr

        @pl.when(c == pl.num_programs(1) - 1)
        def _():
            for hh in range(HG_UNROLL_PROMPT):
                h = it * HG_UNROLL_PROMPT + hh
                s_ref[h] = st_ref[h].T
        return carry

    lax.fori_loop(0, HG_HEADS // HG_UNROLL_PROMPT, heads, 0)


def _hg_scan_prompt(q, f, k, v, gs, ng, batch, t_seq):
    m = q.shape[1]
    rows = 2 * HG_CHUNK
    nc = t_seq // rows
    units = HG_UNROLL_PROMPT * (rows // HG_CHUNK)
    blk = pl.BlockSpec((HG_HEADS, rows, HG_DK), lambda b, c: (0, b * nc + c, 0))
    return pl.pallas_call(
        _hg_scan_prompt_kernel,
        grid=(batch, nc),
        in_specs=[blk] * 5 + [pl.BlockSpec((1, HG_DV), lambda b, c: (0, 0))],
        out_specs=[blk, pl.BlockSpec((None, HG_HEADS, HG_DK, HG_DV), lambda b, c: (b, 0, 0, 0))],
        out_shape=[jax.ShapeDtypeStruct((HG_HEADS, m, HG_DV), BF16),
                   jax.ShapeDtypeStruct((batch, HG_HEADS, HG_DK, HG_DV), F32)],
        scratch_shapes=[pltpu.VMEM((HG_HEADS, HG_DV, HG_DK), F32),
                        pltpu.VMEM((units, HG_CHUNK, HG_DK), F32),
                        pltpu.VMEM((units * HG_CHUNK * HG_SUB, HG_DK), BF16),
                        pltpu.VMEM((units * HG_CHUNK * HG_SUB, HG_DV), F32),
                        pltpu.VMEM((units, HG_CHUNK, HG_DV), F32)],
        compiler_params=_cparams("parallel", "arbitrary"),
        name="hgrn_scan_prompt",
    )(q, f, k, v, gs, ng)


def _hg_step_sample_kernel(q_ref, f_ref, k_ref, v_ref, gs_ref, ng_ref, s0_ref, o_ref, s1_ref, *, t_new):
    n_seq = s0_ref.shape[0]

    ones = jnp.ones((HG_DK, HG_DV), BF16)

    def body(idx, carry):
        per_seq = HG_HEADS // HG_UNROLL_SAMPLE
        sq = idx // per_seq
        rs = pl.ds(pl.multiple_of(sq * t_new, t_new), t_new)
        hs = [(idx % per_seq) * HG_UNROLL_SAMPLE + hh for hh in range(HG_UNROLL_SAMPLE)]
        bs = [_cumsum_rows(f_ref[h, rs, :]) for h in hs]
        inter = [_dot((q_ref[h, rs, :] * jnp.exp(b)).astype(BF16), s0_ref[sq, h].astype(BF16))
                 for h, b in zip(hs, bs)]
        sums = [_dot(_hg_diag_terms(q_ref[h, rs, :], k_ref[h, rs, :], b, t_new), ones) for h, b in zip(hs, bs)]
        exts = []
        for h, b in zip(hs, bs):
            bl = b[t_new - 1:t_new]
            kd = k_ref[h, rs, :] * jnp.exp(bl - b)
            exts.append(jnp.concatenate([kd, jnp.broadcast_to(jnp.exp(bl), (t_new, HG_DK)),
                                         jnp.zeros((HG_DK - 2 * t_new, HG_DK), F32)], axis=0).T)
        grow = [_dot(ext[:, :t_new].astype(BF16), v_ref[h, rs, :].astype(BF16)) for h, ext in zip(hs, exts)]
        for h, oi, a, ext, gr in zip(hs, inter, sums, exts, grow):
            v = v_ref[h, rs, :]
            for s in range(t_new):
                oi = oi + a[s * t_new:(s + 1) * t_new] * v[s:s + 1]
            o_ref[h, rs, :] = _hg_out(oi, ng_ref[...], gs_ref[h, rs, :])
            s1_ref[sq, h] = s0_ref[sq, h] * ext[:, t_new:t_new + 1] + gr
        return carry

    lax.fori_loop(0, n_seq * (HG_HEADS // HG_UNROLL_SAMPLE), body, 0)


def _hg_step_sample(q, f, k, v, gs, ng, s0_all, layer, t_new):
    m = q.shape[1]
    dec_b = s0_all.shape[1]
    sb = min(8, dec_b)
    blk = pl.BlockSpec((HG_HEADS, sb * t_new, HG_DK), lambda i: (0, i, 0))
    st_in = pl.BlockSpec((None, sb, HG_HEADS, HG_DK, HG_DV), lambda i: (layer, i, 0, 0, 0))
    st_out = pl.BlockSpec((sb, HG_HEADS, HG_DK, HG_DV), lambda i: (i, 0, 0, 0))
    return pl.pallas_call(
        functools.partial(_hg_step_sample_kernel, t_new=t_new),
        grid=(dec_b // sb,),
        in_specs=[blk] * 5 + [pl.BlockSpec((1, HG_DV), lambda i: (0, 0)), st_in],
        out_specs=[blk, st_out],
        out_shape=[jax.ShapeDtypeStruct((HG_HEADS, m, HG_DV), BF16), jax.ShapeDtypeStruct(s0_all.shape[1:], F32)],
        compiler_params=_cparams("parallel"),
        name="hgrn_step_sample",
    )(q, f, k, v, gs, ng, s0_all)


def _feature_major_view(a):
    lead = a.shape[:-4]
    t, c, g, d = a.shape[-4:]
    n = len(lead)
    return jnp.transpose(a, tuple(range(n)) + (n + 1, n + 2, n + 3, n)).reshape(lead + (c * g * d, t))


def _token_major_view(a, c):
    lead = a.shape[:-2]
    t = a.shape[-1]
    n = len(lead)
    a = a.reshape(lead + (c, KV_HEADS, HEAD_DIM, t))
    return jnp.transpose(a, tuple(range(n)) + (n + 3, n, n + 1, n + 2))


def kernel(x_prompt, x_sample, cache_nsa_kv, state_nsa_win, state_hgrn, page_table, p_prompt, p_sample, norm_mix, norm_mlp, norm_ple, norm_final, nsa_w_in, nsa_cmp_pe, nsa_cmp_w1, nsa_cmp_w2, nsa_w_out, hg_w_in, hg_lb_logits, hg_norm, hg_w_out, mlp_w1, mlp_w2, ple_w_proj, ple_w_gate):
    batch, t_p, d = x_prompt.shape
    dec_b, t_s, _ = x_sample.shape
    depth = p_prompt.shape[0]
    page = cache_nsa_kv.shape[2]
    past = page_table.shape[1] * page

    xp = x_prompt.reshape(batch * t_p, d)
    xs = x_sample.reshape(dec_b * t_s, d)
    pp = p_prompt.reshape(depth, batch * t_p, -1)
    ps = p_sample.reshape(depth, dec_b * t_s, -1)
    row = lambda a: a.reshape(1, -1)
    cache_t = _feature_major_view(cache_nsa_kv)
    win_t = _feature_major_view(state_nsa_win)
    wl = min(WINDOW, t_p)

    tabs_p = _rope_tables(t_p, 0, t_p)
    tabs_s = _rope_tables(min(ROW_TILE, dec_b * t_s), past, t_s)

    kv_p, kv_s, win_p, win_s, st_p, st_s = [], [], [], [], [], []
    for i in range(depth):
        g_mix = row(norm_mix[i])
        if i % 2 == 0:
            a = i // 2
            n_in = nsa_w_in.shape[2]
            w_t = jnp.pad(jnp.transpose(nsa_w_in[a]).astype(BF16),
                          ((0, Q_DIM + 6 * KV_DIM + LANES - n_in), (0, 0)))
            pe2 = jnp.tile(nsa_cmp_pe[a], (1, 1, LANES // HEAD_DIM))
            cw1 = nsa_cmp_w1[a].reshape(2, CMP_BLOCK * HEAD_DIM, -1).astype(BF16)
            cw2t = jnp.swapaxes(nsa_cmp_w2[a], 1, 2).astype(BF16)
            w_out = nsa_w_out[a].astype(BF16)

            qt, gatest, cmp_rows, rows_t, wrows_t, ksa, vsa, kwa, vwa = _nsa_inproj_prompt(
                xp, g_mix, tabs_p, w_t, batch, t_p)
            kc, vct = _compress_prompt(cmp_rows, pe2, cw1, cw2t, batch, t_p)
            op = _nsa_attn_prompt(qt, gatest, kc, vct, ksa, vsa, kwa, vwa, batch, t_p)
            kv_p.append(rows_t)
            win_p.append(wrows_t[:, :, t_p - wl:])

            q, gates, rows, wrows = _nsa_inproj_sample(xs, g_mix, tabs_s, w_t)
            (kc, vc), n_cmp = _compress_sample(page_table, cache_t, a, rows, pe2, cw1, cw2t, t_s)
            os_, new_win_t = _nsa_attn_sample(page_table, cache_t, win_t, a, kc, vc, q, rows, wrows, gates, t_s, n_cmp)
            kv_s.append(rows.reshape(dec_b, t_s, 4, KV_HEADS, HEAD_DIM))
            win_s.append(new_win_t)
        else:
            r = i // 2
            w_in = hg_w_in[r].astype(BF16)
            w_out = hg_w_out[r].astype(BF16)
            ng = row(hg_norm[r])
            q, f, k, v, gs = _hg_inproj(xp, g_mix, hg_lb_logits, w_in, r)
            op, s_new = _hg_scan_prompt(q, f, k, v, gs, ng, batch, t_p)
            st_p.append(s_new)
            q, f, k, v, gs = _hg_inproj(xs, g_mix, hg_lb_logits, w_in, r)
            os_, s_new = _hg_step_sample(q, f, k, v, gs, ng, state_hgrn, r, t_s)
            st_s.append(s_new)

        w1 = mlp_w1[i].astype(BF16)
        w2 = mlp_w2[i].astype(BF16)
        wgate = ple_w_gate[i].astype(BF16)
        wproj = ple_w_proj[i].astype(BF16)
        last = i == depth - 1
        post = (w_out, row(norm_mlp[i]), w1, w2)
        ple = (i, row(norm_ple[i]), wgate, wproj, row(norm_final), last)
        xp = _post_mixer(xp, op, *post, pp, *ple)
        xs = _post_mixer(xs, os_, *post, ps, *ple)

    return (xp.reshape(batch, t_p, d), xs.reshape(dec_b, t_s, d),
            _token_major_view(jnp.stack(kv_p), 4), jnp.stack(kv_s),
            _token_major_view(jnp.stack(win_p), 2), _token_major_view(jnp.stack(win_s), 2),
            jnp.stack(st_p), jnp.stack(st_s))
```

```python
import functools

import jax
import jax.numpy as jnp
from jax import lax
from jax.experimental import pallas as pl
from jax.experimental.pallas import tpu as pltpu

F32 = jnp.float32
BF16 = jnp.bfloat16

RMS_EPS = 1e-6
ROPE_THETA = 10000.0
N_HEADS = 16
HEAD_DIM = 64
KV_HEADS = 4
Q_PER_KV = N_HEADS // KV_HEADS
Q_DIM = N_HEADS * HEAD_DIM
KV_DIM = KV_HEADS * HEAD_DIM
N_BRANCH = 3
CMP_BLOCK = 32
SEL_BLOCK = 64
TOP_N = 16
WINDOW = 512
FORCED_CUR = 3e4
FORCED_FIRST = 2e4
HG_HEADS = 8
HG_DK = 128
HG_DV = 128
HG_CHUNK = 64
HG_SUB = 16
HG_MAX_FACTOR_EXP = 60.0
HG_UNROLL_PROMPT = 4
HG_UNROLL_SAMPLE = 8
NEG = -1e30
LOG2_E = 1.4426950408889634

LANES = 128
SUBLANES = 8
ROW_TILE = 512
ATT_TILE = 256
V_AUG_ROWS = HEAD_DIM + 2 * SUBLANES
VMEM_LIMIT = 56 * 1024 * 1024


def _cparams(*sem):
    return pltpu.CompilerParams(dimension_semantics=sem, vmem_limit_bytes=VMEM_LIMIT)


def _full(shape):
    return pl.BlockSpec(shape, lambda *_: (0,) * len(shape))


def _aliased_prev(prev):
    if prev is None:
        return [], ()
    return [pl.BlockSpec(memory_space=pl.ANY)], (prev,)


def _rms(x, g):
    return x * lax.rsqrt(jnp.mean(x * x, axis=-1, keepdims=True) + RMS_EPS) * g


def _dot(a, b):
    return jnp.dot(a, b, preferred_element_type=F32)


def _dot_nt(a, b):
    return lax.dot_general(a, b, (((1,), (1,)), ((), ())), preferred_element_type=F32)


def _rope_table_kernel(inv_ref, invc_ref, cos_ref, sin_ref, cost_ref, sint_ref, *, pos0, period):
    i = pl.program_id(0)
    tm = cos_ref.shape[0]
    half = HEAD_DIM // 2
    row = i * tm + lax.broadcasted_iota(jnp.int32, (tm, LANES), 0)
    ang = (pos0 + row % period).astype(F32) * inv_ref[...]
    lane = lax.broadcasted_iota(jnp.int32, (tm, LANES), 1)
    cos_ref[...] = jnp.cos(ang)
    s = jnp.sin(ang)
    sin_ref[...] = jnp.where(lane % HEAD_DIM < half, -s, s)
    col = i * tm + lax.broadcasted_iota(jnp.int32, (HEAD_DIM, tm), 1)
    ang_t = (pos0 + col % period).astype(F32) * jnp.concatenate([invc_ref[...]] * (tm // LANES), axis=1)
    feat = lax.broadcasted_iota(jnp.int32, (HEAD_DIM, tm), 0)
    cost_ref[...] = jnp.cos(ang_t)
    st = jnp.sin(ang_t)
    sint_ref[...] = jnp.where(feat < half, -st, st)


def _rope_tables(rows, pos0, period):
    half = HEAD_DIM // 2
    inv = ROPE_THETA ** (-jnp.arange(half, dtype=F32) / half)
    inv_row = jnp.tile(inv, LANES // half)[None, :]
    inv_col = jnp.broadcast_to(jnp.tile(inv, HEAD_DIM // half)[:, None], (HEAD_DIM, LANES))
    kern = functools.partial(_rope_table_kernel, pos0=pos0, period=period)
    tm = min(ROW_TILE, rows)
    tok = pl.BlockSpec((tm, LANES), lambda i: (i, 0))
    feat = pl.BlockSpec((HEAD_DIM, tm), lambda i: (0, i))
    return pl.pallas_call(
        kern,
        grid=(rows // tm,),
        in_specs=[_full((1, LANES)), _full((HEAD_DIM, LANES))],
        out_specs=[tok, tok, feat, feat],
        out_shape=[jax.ShapeDtypeStruct((rows, LANES), F32)] * 2 + [jax.ShapeDtypeStruct((HEAD_DIM, rows), F32)] * 2,
        compiler_params=_cparams("parallel"),
        name="rope_tables",
    )(inv_row, inv_col)


def _rope_tok(y, cos, sin):
    lane = lax.broadcasted_iota(jnp.int32, y.shape, 1)
    first = lane % HEAD_DIM < HEAD_DIM // 2
    sw = jnp.where(first, pltpu.roll(y, LANES - HEAD_DIM // 2, 1), pltpu.roll(y, HEAD_DIM // 2, 1))
    return y * cos + sw * sin


def _rope_feat(y, cost, sint):
    return y * cost + pltpu.roll(y, HEAD_DIM // 2, 0) * sint


def _nsa_inproj_prompt_kernel(x_ref, g_ref, cos_ref, sin_ref, cost_ref, sint_ref, w_ref, *refs):
    qt_ref, gatest_ref, cmp_ref, rowst_ref, wrowst_ref, ksa_ref, vsa_ref, kwa_ref, vwa_ref = refs[-9:]
    _nsa_inproj_prompt_body(x_ref, g_ref, cos_ref, sin_ref, cost_ref, sint_ref, w_ref,
                            qt_ref, gatest_ref, cmp_ref, rowst_ref, wrowst_ref, ksa_ref, vsa_ref, kwa_ref, vwa_ref)


def _nsa_inproj_prompt_body(x_ref, g_ref, cos_ref, sin_ref, cost_ref, sint_ref, w_ref,
                            qt_ref, gatest_ref, cmp_ref, rowst_ref, wrowst_ref,
                            ksa_ref, vsa_ref, kwa_ref, vwa_ref):
    tm = x_ref.shape[0]
    tk = ksa_ref.shape[2]
    i = pl.program_id(1)
    xn = _rms(x_ref[...], g_ref[...]).astype(BF16)
    cos, sin = cos_ref[...], sin_ref[...]
    cost, sint = cost_ref[...], sint_ref[...]
    scale = HEAD_DIM ** -0.5 * LOG2_E

    qt = _dot_nt(w_ref[0:Q_DIM, :], xn)
    for h in range(N_HEADS):
        rs = slice(h * HEAD_DIM, (h + 1) * HEAD_DIM)
        qt_ref[rs, :] = (_rope_feat(qt[rs], cost, sint) * scale).astype(BF16)
    g0 = Q_DIM + 6 * KV_DIM
    gatest_ref[...] = jax.nn.sigmoid(_dot_nt(w_ref[g0:g0 + LANES, :], xn))

    ycmp = _dot_nt(xn, w_ref[Q_DIM:Q_DIM + 2 * KV_DIM, :])
    for c in range(2 * KV_DIM // LANES):
        cs = slice(c * LANES, (c + 1) * LANES)
        cmp_ref[:, cs] = _rope_tok(ycmp[:, cs], cos, sin) if c * LANES < KV_DIM else ycmp[:, cs]

    lane = lax.broadcasted_iota(jnp.int32, (tm, LANES), 1)
    lo = lane < HEAD_DIM
    pos = i * tm + lax.broadcasted_iota(jnp.int32, (tm, LANES), 0)
    sel_bias = jnp.where(lane - HEAD_DIM == pos // SEL_BLOCK, NEG, 0.0)
    for part, ref, fill in ((2, ksa_ref, sel_bias), (4, kwa_ref, 0.0)):
        yk = _dot_nt(xn, w_ref[Q_DIM + part * KV_DIM:Q_DIM + (part + 1) * KV_DIM, :])
        for c in range(KV_DIM // LANES):
            y = _rope_tok(yk[:, c * LANES:(c + 1) * LANES], cos, sin)
            pair = (jnp.where(lo, y, fill).astype(BF16), jnp.where(lo, pltpu.roll(y, HEAD_DIM, 1), fill).astype(BF16))
            for gg in range(2):
                for j in range(tm // tk):
                    ref[2 * c + gg, j] = pair[gg][j * tk:(j + 1) * tk, :]

    yt = _dot_nt(w_ref[Q_DIM:Q_DIM + 6 * KV_DIM, :], xn)
    ones = jnp.ones((V_AUG_ROWS - HEAD_DIM, tm), F32)
    for part in range(6):
        for g in range(KV_HEADS):
            r0 = part * KV_DIM + g * HEAD_DIM
            y = yt[r0:r0 + HEAD_DIM, :]
            if part % 2 == 0:
                y = _rope_feat(y, cost, sint)
            if part < 4:
                rowst_ref[r0:r0 + HEAD_DIM, :] = y
            else:
                wrowst_ref[r0 - 4 * KV_DIM:r0 - 4 * KV_DIM + HEAD_DIM, :] = y
            if part in (3, 5):
                ref = vsa_ref if part == 3 else vwa_ref
                ya = jnp.concatenate([y, ones], axis=0).astype(BF16)
                for j in range(tm // tk):
                    ref[g, j] = ya[:, j * tk:(j + 1) * tk]


def _nsa_inproj_prompt(x, g, tabs, w, batch, t_seq, layer, n_layers, prev_rows):
    m, d = x.shape
    tm = min(ROW_TILE, t_seq)
    tk = min(ATT_TILE, t_seq)
    nt = t_seq // tm
    cos, sin, cost, sint = tabs
    row = lambda cols: pl.BlockSpec((tm, cols), lambda b, i: (b * nt + i, 0))
    tok_tab = pl.BlockSpec((tm, LANES), lambda b, i: (i, 0))
    feat_tab = pl.BlockSpec((HEAD_DIM, tm), lambda b, i: (0, i))
    featmaj = lambda rows: pl.BlockSpec((None, rows, tm), lambda b, i: (b, 0, i))
    ktiles = pl.BlockSpec((None, KV_HEADS, tm // tk, tk, LANES), lambda b, i: (b, 0, i, 0, 0))
    vtiles = pl.BlockSpec((None, KV_HEADS, tm // tk, V_AUG_ROWS, tk), lambda b, i: (b, 0, i, 0, 0))
    kshape = jax.ShapeDtypeStruct((batch, KV_HEADS, t_seq // tk, tk, LANES), BF16)
    vshape = jax.ShapeDtypeStruct((batch, KV_HEADS, t_seq // tk, V_AUG_ROWS, tk), BF16)
    rows_all = pl.BlockSpec((None, None, 4 * KV_DIM, tm), lambda b, i: (layer, b, 0, i))
    prev_spec, prev_args = _aliased_prev(prev_rows)
    args = (x, g, cos, sin, cost, sint, w) + prev_args
    return pl.pallas_call(
        _nsa_inproj_prompt_kernel,
        grid=(batch, nt),
        in_specs=[row(d), _full((1, d)), tok_tab, tok_tab, feat_tab, feat_tab, _full(w.shape)] + prev_spec,
        out_specs=[featmaj(Q_DIM), featmaj(LANES), row(2 * KV_DIM), rows_all, featmaj(2 * KV_DIM),
                   ktiles, vtiles, ktiles, vtiles],
        out_shape=[jax.ShapeDtypeStruct((batch, Q_DIM, t_seq), BF16), jax.ShapeDtypeStruct((batch, LANES, t_seq), F32),
                   jax.ShapeDtypeStruct((m, 2 * KV_DIM), F32),
                   jax.ShapeDtypeStruct((n_layers, batch, 4 * KV_DIM, t_seq), F32),
                   jax.ShapeDtypeStruct((batch, 2 * KV_DIM, t_seq), F32), kshape, vshape, kshape, vshape],
        input_output_aliases={len(args) - 1: 3} if prev_args else {},
        compiler_params=_cparams("parallel", "parallel"),
        name="nsa_inproj_prompt",
    )(*args)


def _nsa_inproj_sample_kernel(x_ref, g_ref, cos_ref, sin_ref, w_ref, q_ref, gates_ref, rows_ref, wrows_ref):
    xn = _rms(x_ref[...], g_ref[...]).astype(BF16)
    cos, sin = cos_ref[...], sin_ref[...]
    scale = HEAD_DIM ** -0.5
    yq = _dot_nt(xn, w_ref[0:Q_DIM, :])
    for c in range(Q_DIM // LANES):
        cs = slice(c * LANES, (c + 1) * LANES)
        q_ref[:, cs] = (_rope_tok(yq[:, cs], cos, sin) * scale).astype(BF16)
    g0 = Q_DIM + 6 * KV_DIM
    gates_ref[...] = jax.nn.sigmoid(_dot_nt(xn, w_ref[g0:g0 + LANES, :]))
    ykv = _dot_nt(xn, w_ref[Q_DIM:g0, :])
    for c in range(6 * KV_DIM // LANES):
        cs = slice(c * LANES, (c + 1) * LANES)
        part = c * LANES // KV_DIM
        y = _rope_tok(ykv[:, cs], cos, sin) if part % 2 == 0 else ykv[:, cs]
        if part < 4:
            rows_ref[:, cs] = y
        else:
            wrows_ref[:, c * LANES - 4 * KV_DIM:(c + 1) * LANES - 4 * KV_DIM] = y


def _nsa_inproj_sample(x, g, tabs, w):
    m, d = x.shape
    tm = min(ROW_TILE, m)
    cos, sin = tabs[0], tabs[1]
    n_tab = cos.shape[0] // tm
    row = lambda cols: pl.BlockSpec((tm, cols), lambda i: (i, 0))
    tab = pl.BlockSpec((tm, LANES), lambda i: (i % n_tab, 0))
    return pl.pallas_call(
        _nsa_inproj_sample_kernel,
        grid=(m // tm,),
        in_specs=[row(d), _full((1, d)), tab, tab, _full(w.shape)],
        out_specs=[row(d), row(LANES), row(4 * KV_DIM), row(2 * KV_DIM)],
        out_shape=[jax.ShapeDtypeStruct((m, d), BF16), jax.ShapeDtypeStruct((m, LANES), F32),
                   jax.ShapeDtypeStruct((m, 4 * KV_DIM), F32), jax.ShapeDtypeStruct((m, 2 * KV_DIM), F32)],
        compiler_params=_cparams("parallel"),
        name="nsa_inproj_sample",
    )(x, g, cos, sin, w)


def _compress_rows(load, pe_ref, w1_ref, w2t_ref, kv, n, transposed=False):
    lane = lax.broadcasted_iota(jnp.int32, (n, LANES), 1)
    lo = lane < HEAD_DIM
    acc = jnp.zeros((KV_HEADS * n, w1_ref.shape[-1]), F32)
    lq = 4
    for l0 in range(0, CMP_BLOCK, lq):
        halves = [[], []]
        for l in range(l0, l0 + lq, 2):
            for hf in range(2):
                a = load(l, hf) + pe_ref[kv, l:l + 1, :]
                b = load(l + 1, hf) + pe_ref[kv, l + 1:l + 2, :]
                even = jnp.where(lo, a, pltpu.roll(b, HEAD_DIM, 1))
                odd = jnp.where(lo, pltpu.roll(a, HEAD_DIM, 1), b)
                halves[hf].append((even, odd))
        groups = []
        for g in range(KV_HEADS):
            hf, par = divmod(g, 2)
            groups.append(jnp.concatenate([pc[par] for pc in halves[hf]], axis=1))
        xg = jnp.concatenate(groups, axis=0).astype(BF16)
        acc = acc + _dot(xg, w1_ref[kv, l0 * HEAD_DIM:(l0 + lq) * HEAD_DIM, :])
    h = jax.nn.gelu(acc).astype(BF16)
    return _dot_nt(w2t_ref[kv], h) if transposed else _dot_nt(h, w2t_ref[kv])


def _compress_prompt_kernel(s0_ref, s1_ref, s2_ref, s3_ref, pe_ref, w1_ref, w2t_ref, kc_ref, vct_ref):
    n = kc_ref.shape[0]
    hn = n // 2
    srcs = ((s0_ref, s1_ref), (s2_ref, s3_ref))

    def loader(kv):
        def load(l, hf):
            ev = srcs[kv][hf][pl.ds(l, hn, stride=2 * CMP_BLOCK), :]
            od = srcs[kv][hf][pl.ds(CMP_BLOCK + l, hn, stride=2 * CMP_BLOCK), :]
            return jnp.concatenate([ev, od], axis=0)
        return load

    y = _compress_rows(loader(0), pe_ref, w1_ref, w2t_ref, 0, n)
    yt = _compress_rows(loader(1), pe_ref, w1_ref, w2t_ref, 1, n, transposed=True)
    for g in range(KV_HEADS):
        kc_ref[:, g * HEAD_DIM:(g + 1) * HEAD_DIM] = y[g * n:(g + 1) * n].astype(BF16)
        vct_ref[g * HEAD_DIM:(g + 1) * HEAD_DIM, :] = yt[:, g * n:(g + 1) * n].astype(BF16)


def _compress_prompt(cmp_rows, pe2, w1, w2t, batch, t_seq):
    n = t_seq // CMP_BLOCK
    chunk = lambda c: pl.BlockSpec((t_seq, LANES), lambda b: (b, c))
    return pl.pallas_call(
        _compress_prompt_kernel,
        grid=(batch,),
        in_specs=[chunk(0), chunk(1), chunk(2), chunk(3), _full(pe2.shape), _full(w1.shape), _full(w2t.shape)],
        out_specs=[pl.BlockSpec((None, n, KV_DIM), lambda b: (b, 0, 0)),
                   pl.BlockSpec((None, KV_DIM, n), lambda b: (b, 0, 0))],
        out_shape=[jax.ShapeDtypeStruct((batch, n, KV_DIM), BF16), jax.ShapeDtypeStruct((batch, KV_DIM, n), BF16)],
        compiler_params=_cparams("parallel"),
        name="nsa_compress_prompt",
    )(cmp_rows, cmp_rows, cmp_rows, cmp_rows, pe2, w1, w2t)


def _topk_not_selected(imp_t, n_live, rank_ref):
    n_sel, nq = imp_t.shape
    slabs = [imp_t[r0:r0 + SUBLANES] for r0 in range(0, n_sel, SUBLANES)]
    sub = lax.broadcasted_iota(jnp.int32, (SUBLANES, nq), 0)
    rank_ref[...] = jnp.zeros((n_sel, nq), F32)
    for c0 in range(0, n_sel, SUBLANES):
        @pl.when(c0 < n_live)
        def _():
            for v, slab in enumerate(slabs):
                r0 = v * SUBLANES
                cnt = jnp.zeros((SUBLANES, nq), F32)
                for bp in range(c0, c0 + SUBLANES):
                    other = imp_t[bp:bp + 1, :]
                    if bp < r0:
                        ahead = other >= slab
                    elif bp >= r0 + SUBLANES - 1:
                        ahead = other > slab
                    else:
                        ahead = (other > slab) | ((other == slab) & (bp - r0 < sub))
                    cnt = cnt + jnp.where(ahead, 1.0, 0.0)
                rank_ref[r0:r0 + SUBLANES, :] += cnt
    return jnp.where(rank_ref[...] < TOP_N, 0.0, 1.0)


def _nsa_attn_prompt_kernel(qt_ref, gatest_ref, kc_ref, vct_ref, ksa_ref, vsa_ref, kwa_ref, vwa_ref, o_ref,
                            acc_ref, m_ref, accw_ref, mw_ref, sa_ref, sb_ref, wb_ref, cb_ref, cv_ref, rank_ref):
    tq = qt_ref.shape[1]
    tk = ksa_ref.shape[2]
    n_cmp = kc_ref.shape[0]
    n_sel = n_cmp // 2
    rq = Q_PER_KV * tq
    i = pl.program_id(1)
    t0 = i * tq
    t_col = t0 + lax.broadcasted_iota(jnp.int32, (1, rq), 1) % tq
    key_row = lax.broadcasted_iota(jnp.int32, (tk, rq), 0)
    n_wt = -(-(WINDOW - 1) // tk) + 1
    kt_diag = t0 // tk
    wkt = [kt_diag - j for j in range(n_wt)]

    for j, kt in enumerate(wkt):
        kpos = kt * tk + key_row
        wb_ref[j] = jnp.where((kpos >= 0) & (kpos <= t_col) & (kpos > t_col - WINDOW), 0.0, NEG)
    nrow = lax.broadcasted_iota(jnp.int32, (n_cmp, rq), 0)
    cblk = jnp.where(nrow < n_sel, 2 * nrow, 2 * (nrow - n_sel) + 1)
    cvis = (cblk + 1) * CMP_BLOCK - 1 <= t_col
    cb_ref[...] = jnp.where(cvis, 0.0, NEG)
    cv_ref[...] = jnp.where(cvis, 1.0, 0.0)

    def online_update(s_ref, vt, bias, m_ref, acc_ref):
        s = s_ref[...]
        if bias is not None:
            s = s + bias
        m_old = m_ref[...]
        m_new = jnp.maximum(m_old, jnp.max(s, axis=0, keepdims=True))
        p = jnp.exp2(s - m_new)
        acc_ref[...] = acc_ref[...] * jnp.exp2(m_old - m_new) + _dot(vt, p.astype(BF16))
        m_ref[...] = m_new

    def normalized(acc):
        return acc[:HEAD_DIM] * (1.0 / jnp.maximum(acc[HEAD_DIM:HEAD_DIM + 1], 1e-30))

    for g in range(KV_HEADS):
        qg = jnp.concatenate(
            [qt_ref[(Q_PER_KV * g + r) * HEAD_DIM:(Q_PER_KV * g + r + 1) * HEAD_DIM, :] for r in range(Q_PER_KV)],
            axis=1)

        sc = _dot(kc_ref[:, g * HEAD_DIM:(g + 1) * HEAD_DIM], qg) + cb_ref[...]
        e = jnp.exp2(sc - jnp.max(sc, axis=0, keepdims=True)) * cv_ref[...]
        pc = e * (1.0 / jnp.maximum(jnp.sum(e, axis=0, keepdims=True), 1e-30))
        o_c = _dot(vct_ref[g * HEAD_DIM:(g + 1) * HEAD_DIM, :], pc.astype(BF16))

        pq = pc[:, 0:tq]
        for r in range(1, Q_PER_KV):
            pq = pq + pc[:, r * tq:(r + 1) * tq]
        imp = pq[:n_sel] + pq[n_sel:]
        blk = lax.broadcasted_iota(jnp.int32, (n_sel, tq), 0)
        cur = (t0 + lax.broadcasted_iota(jnp.int32, (n_sel, tq), 1)) // SEL_BLOCK
        imp = jnp.where(blk > cur, -1.0, imp)
        imp = jnp.where(blk == 0, FORCED_FIRST, imp)
        imp = jnp.where(blk == cur, FORCED_CUR, imp)
        n_live = (t0 + tq - 1) // SEL_BLOCK + 1
        notsel = _topk_not_selected(imp, n_live, rank_ref)
        if n_sel < HEAD_DIM:
            notsel = jnp.concatenate([notsel, jnp.zeros((HEAD_DIM - n_sel, tq), F32)], axis=0)
        qa = jnp.concatenate([qg, jnp.concatenate([notsel.astype(BF16)] * Q_PER_KV, axis=1)], axis=0)

        for ref in (m_ref, mw_ref):
            ref[...] = jnp.full((1, rq), NEG, F32)
        for ref in (acc_ref, accw_ref):
            ref[...] = jnp.zeros((V_AUG_ROWS, rq), F32)
        slots = [sa_ref, sb_ref]
        qw = jnp.concatenate([qg, jnp.zeros_like(qg)], axis=0)

        slots[0][...] = _dot(kwa_ref[g, wkt[0]], qw)
        for j, kt in enumerate(wkt):
            if j + 1 < n_wt:
                slots[1 - j % 2][...] = _dot(kwa_ref[g, jnp.maximum(wkt[j + 1], 0)], qw)
            else:
                slots[1 - j % 2][...] = _dot(ksa_ref[g, 0], qa)
            online_update(slots[j % 2], vwa_ref[g, jnp.maximum(kt, 0)], wb_ref[j], mw_ref, accw_ref)
        o_w = normalized(accw_ref[...])

        s_even, s_odd = slots[n_wt % 2], slots[1 - n_wt % 2]

        def sel_pair(j, carry):
            s_odd[...] = _dot(ksa_ref[g, 2 * j + 1], qa)
            online_update(s_even, vsa_ref[g, 2 * j], None, m_ref, acc_ref)
            s_even[...] = _dot(ksa_ref[g, 2 * j + 2], qa)
            online_update(s_odd, vsa_ref[g, 2 * j + 1], None, m_ref, acc_ref)
            return carry

        lax.fori_loop(0, kt_diag // 2, sel_pair, 0)

        @pl.when(kt_diag % 2 == 1)
        def _():
            s_odd[...] = _dot(ksa_ref[g, kt_diag], qa)
            online_update(s_even, vsa_ref[g, kt_diag - 1], None, m_ref, acc_ref)
            online_update(s_odd, vsa_ref[g, kt_diag], wb_ref[0], m_ref, acc_ref)

        @pl.when(kt_diag % 2 == 0)
        def _():
            online_update(s_even, vsa_ref[g, kt_diag], wb_ref[0], m_ref, acc_ref)

        o_s = normalized(acc_ref[...])

        outs = []
        for r in range(Q_PER_KV):
            h = Q_PER_KV * g + r
            cs = slice(r * tq, (r + 1) * tq)
            gc = gatest_ref[N_BRANCH * h:N_BRANCH * h + 1, :]
            gs = gatest_ref[N_BRANCH * h + 1:N_BRANCH * h + 2, :]
            gw = gatest_ref[N_BRANCH * h + 2:N_BRANCH * h + 3, :]
            outs.append(gc * o_c[:, cs] + gs * o_s[:, cs] + gw * o_w[:, cs])
        for pr in range(Q_PER_KV // 2):
            c0 = (Q_PER_KV * g + 2 * pr) * HEAD_DIM
            o_ref[:, c0:c0 + LANES] = jnp.concatenate(outs[2 * pr:2 * pr + 2], axis=0).T.astype(BF16)


def _nsa_attn_prompt(qt, gatest, kc, vct, ksa, vsa, kwa, vwa, batch, t_seq):
    d = qt.shape[1]
    tq = min(ATT_TILE, t_seq)
    nq = t_seq // tq
    n_cmp = t_seq // CMP_BLOCK
    tk = ksa.shape[3]
    featmaj = lambda rows: pl.BlockSpec((None, rows, tq), lambda b, i: (b, 0, i))
    seq = lambda shape: pl.BlockSpec((None,) + shape[1:], lambda b, i: (b, 0, 0, 0, 0))
    return pl.pallas_call(
        _nsa_attn_prompt_kernel,
        grid=(batch, nq),
        in_specs=[featmaj(d), featmaj(LANES),
                  pl.BlockSpec((None, n_cmp, KV_DIM), lambda b, i: (b, 0, 0)),
                  pl.BlockSpec((None, KV_DIM, n_cmp), lambda b, i: (b, 0, 0)),
                  seq(ksa.shape), seq(vsa.shape), seq(kwa.shape), seq(vwa.shape)],
        out_specs=pl.BlockSpec((tq, d), lambda b, i: (b * nq + i, 0)),
        out_shape=jax.ShapeDtypeStruct((batch * t_seq, d), BF16),
        scratch_shapes=[pltpu.VMEM((V_AUG_ROWS, Q_PER_KV * tq), F32), pltpu.VMEM((1, Q_PER_KV * tq), F32),
                        pltpu.VMEM((V_AUG_ROWS, Q_PER_KV * tq), F32), pltpu.VMEM((1, Q_PER_KV * tq), F32),
                        pltpu.VMEM((tk, Q_PER_KV * tq), F32), pltpu.VMEM((tk, Q_PER_KV * tq), F32),
                        pltpu.VMEM((-(-(WINDOW - 1) // tk) + 1, tk, Q_PER_KV * tq), F32),
                        pltpu.VMEM((n_cmp, Q_PER_KV * tq), F32), pltpu.VMEM((n_cmp, Q_PER_KV * tq), F32),
                        pltpu.VMEM((n_cmp // 2, tq), F32)],
        compiler_params=_cparams("parallel", "arbitrary"),
        name="nsa_attn_prompt",
    )(qt, gatest, kc, vct, ksa, vsa, kwa, vwa)


def _compress_sample_kernel(pt_ref, *refs, n_pages, page, t_new, n_cmp):
    del pt_ref
    pages = refs[:n_pages]
    new_ref, pe_ref, w1_ref, w2t_ref, kc_ref, vc_ref = refs[n_pages:n_pages + 6]
    bufs = refs[n_pages + 6:]
    n_pad = bufs[0].shape[1]
    n_past = n_pages * page // CMP_BLOCK
    grp = SUBLANES * CMP_BLOCK
    ppg = grp // page
    tt = lax.broadcasted_iota(jnp.int32, (grp, grp), 0)
    perm_t = jnp.where(lax.broadcasted_iota(jnp.int32, (grp, grp), 1)
                       == (tt % CMP_BLOCK) * SUBLANES + tt // CMP_BLOCK, 1.0, 0.0).astype(BF16)
    for c, buf_ref in enumerate(bufs):
        cs = slice(c * LANES, (c + 1) * LANES)
        for pp in range(n_pages // ppg):
            xt = jnp.concatenate([pages[pp * ppg + j][cs, :] for j in range(ppg)], axis=1)
            xp = _dot(xt.astype(BF16), perm_t).T
            for l in range(CMP_BLOCK):
                buf_ref[l, pp * SUBLANES:(pp + 1) * SUBLANES, :] = xp[l * SUBLANES:(l + 1) * SUBLANES, :]
        buf_ref[:, n_past:, :] = jnp.zeros((CMP_BLOCK, n_pad - n_past, LANES), F32)
        for l in range(t_new):
            buf_ref[l, n_past:n_past + 1, :] = new_ref[l:l + 1, cs]
    n_out = kc_ref.shape[0]
    for kv, out_ref in ((0, kc_ref), (1, vc_ref)):
        def load(l, hf, kv=kv):
            return bufs[2 * kv + hf][l]
        y = _compress_rows(load, pe_ref, w1_ref, w2t_ref, kv, n_pad)
        valid = lax.broadcasted_iota(jnp.int32, (n_pad, HEAD_DIM), 0) < n_cmp
        for g in range(KV_HEADS):
            out_ref[0:n_pad, g * HEAD_DIM:(g + 1) * HEAD_DIM] = jnp.where(
                valid, y[g * n_pad:(g + 1) * n_pad], 0.0).astype(BF16)
        out_ref[n_pad:, :] = jnp.zeros((n_out - n_pad, KV_DIM), BF16)


def _page_specs(layer, n_pages, page, feat_block):
    half = 2 * KV_DIM
    return [pl.BlockSpec((None, None, half, page),
                         functools.partial(lambda b, pt, p: (layer, pt[b, p], feat_block, 0), p=p))
            for p in range(n_pages)]


def _compress_sample(page_table, cache_t, layer, rows_new, pe2, w1, w2t, t_new):
    dec_b, n_pages = page_table.shape
    page = cache_t.shape[-1]
    past = n_pages * page
    l_pad = -(-(past + t_new) // SEL_BLOCK) * SEL_BLOCK
    n_cmp = l_pad // CMP_BLOCK
    n_pad = -(-n_cmp // SUBLANES) * SUBLANES
    n_out = LANES
    half = 2 * KV_DIM
    kern = functools.partial(_compress_sample_kernel, n_pages=n_pages, page=page, t_new=t_new, n_cmp=n_cmp)
    grid_spec = pltpu.PrefetchScalarGridSpec(
        num_scalar_prefetch=1,
        grid=(dec_b,),
        in_specs=_page_specs(layer, n_pages, page, 0) + [
            pl.BlockSpec((t_new, half), lambda b, pt: (b, 0)),
            pl.BlockSpec(pe2.shape, lambda b, pt: (0, 0, 0)),
            pl.BlockSpec(w1.shape, lambda b, pt: (0, 0, 0)),
            pl.BlockSpec(w2t.shape, lambda b, pt: (0, 0, 0))],
        out_specs=[pl.BlockSpec((None, n_out, KV_DIM), lambda b, pt: (b, 0, 0))] * 2,
        scratch_shapes=[pltpu.VMEM((CMP_BLOCK, n_pad, LANES), F32)] * (half // LANES),
    )
    return pl.pallas_call(
        kern,
        grid_spec=grid_spec,
        out_shape=[jax.ShapeDtypeStruct((dec_b, n_out, KV_DIM), BF16)] * 2,
        compiler_params=_cparams("arbitrary"),
        name="nsa_compress_sample",
    )(page_table, *([cache_t] * n_pages), rows_new, pe2, w1, w2t), n_cmp


def _nsa_attn_sample_kernel(pt_ref, *refs, n_pages, page, t_new, n_cmp):
    del pt_ref
    pages = refs[:n_pages]
    (kc_ref, vc_ref, q_ref, new_ref, wnew_ref, win_ref, gates_ref) = refs[n_pages:n_pages + 7]
    o_ref, nwin_ref, kst_ref, vst_ref = refs[-4:]
    past = n_pages * page
    n_buf = win_ref.shape[1]
    n_sel = n_cmp // 2
    gq = Q_PER_KV * t_new
    rq = KV_HEADS * gq
    nk = past + LANES
    nw = n_buf + LANES

    for p in range(n_pages):
        blk = pages[p][...]
        kst_ref[:, p * page:(p + 1) * page] = blk[:KV_DIM].astype(BF16)
        vst_ref[:, p * page:(p + 1) * page] = blk[KV_DIM:].astype(BF16)
    pad = jnp.zeros((LANES - t_new, KV_DIM), F32)
    new = new_ref[...]
    wnew = wnew_ref[...]
    knew = jnp.concatenate([new[:, :KV_DIM], pad], axis=0).astype(BF16)
    vnew = jnp.concatenate([new[:, KV_DIM:], pad], axis=0).astype(BF16)
    kwnew = jnp.concatenate([wnew[:, :KV_DIM], pad], axis=0).astype(BF16)
    vwnew = jnp.concatenate([wnew[:, KV_DIM:], pad], axis=0).astype(BF16)
    win = win_ref[...]
    kwt = win[:KV_DIM].astype(BF16)
    vwt = win[KV_DIM:].astype(BF16)

    rolled = pltpu.roll(win, n_buf - t_new, 1)
    wnew_t = jnp.concatenate([jnp.zeros((LANES - t_new, 2 * KV_DIM), F32), wnew], axis=0).T
    tail_lane = lax.broadcasted_iota(jnp.int32, (2 * KV_DIM, LANES), 1)
    nwin_ref[:, 0:n_buf - LANES] = rolled[:, 0:n_buf - LANES]
    nwin_ref[:, n_buf - LANES:] = jnp.where(tail_lane >= LANES - t_new, wnew_t, rolled[:, n_buf - LANES:])

    q = q_ref[...]
    lane_g = lax.broadcasted_iota(jnp.int32, (t_new, KV_DIM), 1) // HEAD_DIM
    pieces = []
    for g in range(KV_HEADS):
        for j in range(Q_PER_KV):
            qj = jnp.concatenate([q[:, (Q_PER_KV * gg + j) * HEAD_DIM:(Q_PER_KV * gg + j + 1) * HEAD_DIM]
                                  for gg in range(KV_HEADS)], axis=1)
            pieces.append(jnp.where(lane_g == g, qj, jnp.zeros_like(qj)))
    qbd = jnp.concatenate(pieces, axis=0)
    t_row = past + lax.broadcasted_iota(jnp.int32, (rq, 1), 0) % t_new

    def softmax(s, mask):
        s = jnp.where(mask, s, NEG)
        e = jnp.where(mask, jnp.exp(s - jnp.max(s, axis=1, keepdims=True)), 0.0)
        return e * (1.0 / jnp.maximum(jnp.sum(e, axis=1, keepdims=True), 1e-30))

    ncl = kc_ref.shape[0]
    sc = _dot_nt(qbd, kc_ref[...])
    sw = jnp.concatenate([_dot(qbd, kwt), _dot_nt(qbd, kwnew)], axis=1)
    ss = jnp.concatenate([_dot(qbd, kst_ref[...]), _dot_nt(qbd, knew)], axis=1)

    cidx = lax.broadcasted_iota(jnp.int32, (rq, ncl), 1)
    pc = softmax(sc, (cidx < n_cmp) & ((cidx + 1) * CMP_BLOCK - 1 <= t_row))
    oc = _dot(pc.astype(BF16), vc_ref[...])

    wpos = past - n_buf + lax.broadcasted_iota(jnp.int32, (rq, nw), 1)
    pw = softmax(sw, (wpos <= t_row) & (wpos > t_row - WINDOW)).astype(BF16)
    ow = _dot_nt(pw[:, :n_buf], vwt) + _dot(pw[:, n_buf:], vwnew)

    groups = []
    for g in range(KV_HEADS):
        pg = pc[g * gq:g * gq + t_new]
        for j in range(1, Q_PER_KV):
            pg = pg + pc[g * gq + j * t_new:g * gq + (j + 1) * t_new]
        groups.append(pg)
    pg = jnp.concatenate(groups, axis=0)
    nr = KV_HEADS * t_new
    imp = pg + pltpu.roll(pg, ncl - 1, 1)
    lane = lax.broadcasted_iota(jnp.int32, (nr, ncl), 1)
    blk = lane // 2
    cur = (past + lax.broadcasted_iota(jnp.int32, (nr, ncl), 0) % t_new) // SEL_BLOCK
    imp = jnp.where(blk > cur, -1.0, imp)
    imp = jnp.where(blk == 0, FORCED_FIRST, imp)
    imp = jnp.where(blk == cur, FORCED_CUR, imp)
    rank = jnp.zeros((nr, ncl), jnp.int32)
    for bp in range(n_sel):
        other = imp[:, 2 * bp:2 * bp + 1]
        rank = rank + ((other > imp) | ((other == imp) & (bp < blk))).astype(jnp.int32)
    sel = jnp.where((rank < TOP_N) & (lane % 2 == 0) & (blk < n_sel), 1.0, 0.0)
    sel_rows = jnp.concatenate([sel[g * t_new:(g + 1) * t_new] for g in range(KV_HEADS) for _ in range(Q_PER_KV)],
                               axis=0).astype(BF16)
    expand = jnp.where(lax.broadcasted_iota(jnp.int32, (ncl, nk), 0)
                       == 2 * (lax.broadcasted_iota(jnp.int32, (ncl, nk), 1) // SEL_BLOCK), 1.0, 0.0).astype(BF16)
    selmask = _dot(sel_rows, expand) > 0.5

    kpos = lax.broadcasted_iota(jnp.int32, (rq, nk), 1)
    ps = softmax(ss, selmask & (kpos <= t_row)).astype(BF16)
    os_ = _dot_nt(ps[:, :past], vst_ref[...]) + _dot(ps[:, past:], vnew)

    gates = gates_ref[...]
    for g in range(KV_HEADS):
        for j in range(Q_PER_KV):
            h = Q_PER_KV * g + j
            rs = slice(g * gq + j * t_new, g * gq + (j + 1) * t_new)
            cs = slice(g * HEAD_DIM, (g + 1) * HEAD_DIM)
            gc = gates[:, N_BRANCH * h:N_BRANCH * h + 1]
            gs = gates[:, N_BRANCH * h + 1:N_BRANCH * h + 2]
            gw = gates[:, N_BRANCH * h + 2:N_BRANCH * h + 3]
            o_ref[:, h * HEAD_DIM:(h + 1) * HEAD_DIM] = (
                gc * oc[rs, cs] + gs * os_[rs, cs] + gw * ow[rs, cs]).astype(BF16)


def _nsa_attn_sample(page_table, cache_t, win_t, layer, kc, vc, q, rows_new, wrows_new, gates, t_new, n_cmp, prev_win):
    dec_b, n_pages = page_table.shape
    page = cache_t.shape[-1]
    past = n_pages * page
    n_buf = win_t.shape[-1]
    half = 2 * KV_DIM
    d = q.shape[1]
    ncl = kc.shape[1]
    kern = functools.partial(_nsa_attn_sample_kernel, n_pages=n_pages, page=page, t_new=t_new, n_cmp=n_cmp)
    prev_spec, prev_args = _aliased_prev(prev_win)
    grid_spec = pltpu.PrefetchScalarGridSpec(
        num_scalar_prefetch=1,
        grid=(dec_b,),
        in_specs=_page_specs(layer, n_pages, page, 1) + [
            pl.BlockSpec((None, ncl, KV_DIM), lambda b, pt: (b, 0, 0)),
            pl.BlockSpec((None, ncl, KV_DIM), lambda b, pt: (b, 0, 0)),
            pl.BlockSpec((t_new, d), lambda b, pt: (b, 0)),
            pl.BlockSpec((t_new, half), lambda b, pt: (b, 1)),
            pl.BlockSpec((t_new, half), lambda b, pt: (b, 0)),
            pl.BlockSpec((None, None, half, n_buf), lambda b, pt: (layer, b, 0, 0)),
            pl.BlockSpec((t_new, LANES), lambda b, pt: (b, 0))] + prev_spec,
        out_specs=[pl.BlockSpec((t_new, d), lambda b, pt: (b, 0)),
                   pl.BlockSpec((None, None, half, n_buf), lambda b, pt: (layer, b, 0, 0))],
        scratch_shapes=[pltpu.VMEM((KV_DIM, past), BF16), pltpu.VMEM((KV_DIM, past), BF16)],
    )
    args = (page_table, *([cache_t] * n_pages), kc, vc, q, rows_new, wrows_new, win_t, gates) + prev_args
    return pl.pallas_call(
        kern,
        grid_spec=grid_spec,
        out_shape=[jax.ShapeDtypeStruct((dec_b * t_new, d), BF16), jax.ShapeDtypeStruct(win_t.shape, F32)],
        input_output_aliases={len(args) - 1: 1} if prev_args else {},
        compiler_params=_cparams("arbitrary"),
        name="nsa_attn_sample",
    )(*args)


def _post_mixer_kernel(x_ref, o_ref, wo_ref, gm_ref, w1_ref, w2_ref, p_ref, gp_ref, wg_ref, wp_ref, gf_ref,
                       y_ref, xn_ref, acc_ref, *, head_major, final_norm):
    j = pl.program_id(1)

    @pl.when(j == 0)
    def _():
        if head_major:
            o = jnp.concatenate([o_ref[h] for h in range(o_ref.shape[0])], axis=1)
        else:
            o = o_ref[...]
        x = x_ref[...] + _dot(o, wo_ref[...])
        xn_ref[...] = _rms(x, gm_ref[...]).astype(BF16)
        acc_ref[...] = x

    h = jnp.maximum(_dot(xn_ref[...], w1_ref[...]), 0.0)
    acc_ref[...] += _dot((h * h).astype(BF16), w2_ref[...])

    @pl.when(j == pl.num_programs(1) - 1)
    def _():
        x = acc_ref[...]
        gate = jax.nn.sigmoid(_dot(_rms(x, gp_ref[...]).astype(BF16), wg_ref[...]))
        y = x + _dot(p_ref[...].astype(BF16), wp_ref[...]) * gate
        if final_norm:
            y = _rms(y, gf_ref[...])
        y_ref[...] = y


def _post_mixer(x, o, w_out, g_mlp, w1, w2, p_all, layer, g_ple, wg, wp, gf, final_norm):
    m, d = x.shape
    ff = w1.shape[1]
    tm, tf = min(ROW_TILE, m), 1024
    head_major = o.ndim == 3
    row = pl.BlockSpec((tm, d), lambda i, j: (i, 0))
    vec = pl.BlockSpec((1, d), lambda i, j: (0, 0))
    full = lambda a: pl.BlockSpec(a.shape, lambda i, j: (0,) * a.ndim)
    o_spec = pl.BlockSpec((o.shape[0], tm, o.shape[2]), lambda i, j: (0, i, 0)) if head_major else row
    return pl.pallas_call(
        functools.partial(_post_mixer_kernel, head_major=head_major, final_norm=final_norm),
        grid=(m // tm, ff // tf),
        in_specs=[row, o_spec, full(w_out), vec,
                  pl.BlockSpec((d, tf), lambda i, j: (0, j)), pl.BlockSpec((tf, d), lambda i, j: (j, 0)),
                  pl.BlockSpec((None, tm, p_all.shape[-1]), lambda i, j: (layer, i, 0)), vec,
                  full(wg), full(wp), vec],
        out_specs=row,
        out_shape=jax.ShapeDtypeStruct((m, d), F32),
        scratch_shapes=[pltpu.VMEM((tm, d), BF16), pltpu.VMEM((tm, d), F32)],
        compiler_params=_cparams("parallel", "arbitrary"),
        name="post_mixer",
    )(x, o, w_out, g_mlp, w1, w2, p_all, g_ple, wg, wp, gf)


def _hg_inproj_kernel(x_ref, g_ref, lbl_ref, w_ref, q_ref, f_ref, k_ref, v_ref, gs_ref, *, layer):
    d = x_ref.shape[1]
    xn = _rms(x_ref[...], g_ref[...]).astype(BF16)

    def put(ref, val):
        for h in range(HG_HEADS):
            ref[h] = val[:, h * HG_DK:(h + 1) * HG_DK]

    lg = lbl_ref[...]
    e = jnp.exp(lg - jnp.max(lg, axis=0, keepdims=True))
    w = e / jnp.sum(e, axis=0, keepdims=True)
    cs = w[0:1]
    for r in range(1, layer + 1):
        cs = cs + w[r:r + 1]
    lb = cs - w[0:1]

    y = _dot(xn, w_ref[:, 0:d])
    put(q_ref, jax.nn.silu(y) * HG_DK ** -0.5)
    y = _dot(xn, w_ref[:, d:2 * d])
    sg = jax.nn.sigmoid(y)
    put(f_ref, jnp.log(lb + (1.0 - lb) * sg))
    put(k_ref, (1.0 - lb) * (1.0 - sg))
    put(v_ref, _dot(xn, w_ref[:, 2 * d:3 * d]))
    put(gs_ref, jax.nn.silu(_dot(xn, w_ref[:, 3 * d:4 * d])))


def _hg_inproj(x, g, lb_logits, w, layer):
    m, d = x.shape
    tm = min(ROW_TILE, m)
    head_major = pl.BlockSpec((HG_HEADS, tm, HG_DK), lambda i: (0, i, 0))
    return pl.pallas_call(
        functools.partial(_hg_inproj_kernel, layer=layer),
        grid=(m // tm,),
        in_specs=[pl.BlockSpec((tm, d), lambda i: (i, 0)), _full((1, d)), _full(lb_logits.shape), _full(w.shape)],
        out_specs=[head_major] * 5,
        out_shape=[jax.ShapeDtypeStruct((HG_HEADS, m, HG_DK), F32)] * 5,
        compiler_params=_cparams("parallel"),
        name="hgrn_inproj",
    )(x, g, lb_logits, w)


def _cumsum_rows(f):
    c = f.shape[0]
    row = lax.broadcasted_iota(jnp.int32, f.shape, 0)
    s = 1
    while s < c:
        f = f + jnp.where(row >= s, pltpu.roll(f, s, 0), 0.0)
        s *= 2
    return f


def _hg_out(o, ng, gs):
    return (_rms(o, ng) * gs).astype(BF16)


def _hg_diag_terms(q, k, b, sub):
    trow = lax.broadcasted_iota(jnp.int32, (sub, HG_DK), 0)
    terms = []
    for r0 in range(0, q.shape[0], sub):
        qi, ki, bi = q[r0:r0 + sub], k[r0:r0 + sub], b[r0:r0 + sub]
        for s in range(sub):
            causal = trow >= s
            dec = jnp.exp(jnp.where(causal, bi - bi[s:s + 1], 0.0))
            terms.append(jnp.where(causal, qi * ki[s:s + 1] * dec, 0.0).astype(BF16))
    return jnp.concatenate(terms, axis=0)


def _hg_offdiag(q, k, v, b, sub):
    c = q.shape[0]
    qps, kps, vps, spans = [], [], [], []
    off = 0
    for r0 in range(sub, c, sub):
        bs = b[r0 - 1:r0]
        qps.append(q[r0:r0 + sub] * jnp.exp(b[r0:r0 + sub] - bs))
        kps.append(k[:r0] * jnp.exp(bs - b[:r0]))
        vps.append(v[:r0])
        spans.append((off, off + r0))
        off += r0
    a = _dot_nt(jnp.concatenate(qps, axis=0).astype(BF16), jnp.concatenate(kps, axis=0).astype(BF16))
    row_blk = lax.broadcasted_iota(jnp.int32, a.shape, 0) // sub
    col = lax.broadcasted_iota(jnp.int32, a.shape, 1)
    keep = jnp.zeros(a.shape, jnp.bool_)
    for i, (lo, hi) in enumerate(spans):
        keep = keep | ((row_blk == i) & (col >= lo) & (col < hi))
    return _dot(jnp.where(keep, a, 0.0).astype(BF16), jnp.concatenate(vps, axis=0).astype(BF16))


def _hg_factored_scores(q, k, b, sub):
    c = q.shape[0]
    qps, kps, spans = [], [], []
    off = 0
    for r0 in range(0, c, sub):
        hi = r0 + sub
        bs = b[r0 - 1:r0] if r0 > 0 else jnp.zeros((1, HG_DK), F32)
        qps.append(q[r0:hi] * jnp.exp(b[r0:hi] - bs))
        kps.append(k[:hi] * jnp.exp(bs - b[:hi]))
        spans.append((off, off + hi))
        off += hi
    a = _dot_nt(jnp.concatenate(qps, axis=0).astype(BF16), jnp.concatenate(kps, axis=0).astype(BF16))
    row = lax.broadcasted_iota(jnp.int32, a.shape, 0)
    col = lax.broadcasted_iota(jnp.int32, a.shape, 1)
    keep = jnp.zeros(a.shape, jnp.bool_)
    for i, (lo, hi) in enumerate(spans):
        keep = keep | ((row // sub == i) & (col >= lo) & (col < hi) & (col - lo <= row))
    return jnp.where(keep, a, 0.0).astype(BF16)


def _hg_stacked_values(v, sub):
    return jnp.concatenate([v[:r0 + sub] for r0 in range(0, v.shape[0], sub)], axis=0).astype(BF16)


def _hg_scan_prompt_kernel(q_ref, f_ref, k_ref, v_ref, gs_ref, ng_ref, o_ref, s_ref,
                           st_ref, b_ref, t_ref, a_ref, i_ref):
    c = pl.program_id(1)
    rows = q_ref.shape[1]
    n_chunk = rows // HG_CHUNK
    tr = HG_CHUNK * HG_SUB
    ones = jnp.ones((HG_DK, HG_DV), BF16)

    @pl.when(c == 0)
    def _():
        st_ref[...] = jnp.zeros(st_ref.shape, F32)

    def heads(it, carry):
        drop = jnp.zeros((1, HG_DK), F32)
        for hh in range(HG_UNROLL_PROMPT):
            h = it * HG_UNROLL_PROMPT + hh
            for ci in range(n_chunk):
                u = hh * n_chunk + ci
                b = _cumsum_rows(f_ref[h, pl.ds(ci * HG_CHUNK, HG_CHUNK), :])
                b_ref[u] = b
                for r0 in range(0, HG_CHUNK, HG_SUB):
                    last = b[r0 + HG_SUB - 1:r0 + HG_SUB]
                    drop = jnp.maximum(drop, -last if r0 == 0 else b[r0 - 1:r0] - last)
        factorable = jnp.max(drop) < HG_MAX_FACTOR_EXP

        @pl.when(factorable)
        def _():
            units = [(it * HG_UNROLL_PROMPT + hh, hh * n_chunk + ci, pl.ds(ci * HG_CHUNK, HG_CHUNK))
                     for hh in range(HG_UNROLL_PROMPT) for ci in range(n_chunk)]
            scores = [_hg_factored_scores(q_ref[h, rs, :], k_ref[h, rs, :], b_ref[u], HG_SUB) for h, u, rs in units]
            for (h, u, rs), a in zip(units, scores):
                i_ref[u] = _dot(a, _hg_stacked_values(v_ref[h, rs, :], HG_SUB))

        @pl.when(jnp.logical_not(factorable))
        def _():
            for hh in range(HG_UNROLL_PROMPT):
                h = it * HG_UNROLL_PROMPT + hh
                for ci in range(n_chunk):
                    u = hh * n_chunk + ci
                    rs = pl.ds(ci * HG_CHUNK, HG_CHUNK)
                    t_ref[u * tr:(u + 1) * tr, :] = _hg_diag_terms(q_ref[h, rs, :], k_ref[h, rs, :], b_ref[u], HG_SUB)
            a_ref[...] = _dot(t_ref[...], ones)
            for hh in range(HG_UNROLL_PROMPT):
                h = it * HG_UNROLL_PROMPT + hh
                for ci in range(n_chunk):
                    u = hh * n_chunk + ci
                    rs = pl.ds(ci * HG_CHUNK, HG_CHUNK)
                    q, k, v, b = q_ref[h, rs, :], k_ref[h, rs, :], v_ref[h, rs, :], b_ref[u]
                    off = _hg_offdiag(q, k, v, b, HG_SUB)
                    for r0 in range(0, HG_CHUNK, HG_SUB):
                        oi = off[r0 - HG_SUB:r0] if r0 > 0 else jnp.zeros((HG_SUB, HG_DV), F32)
                        for s in range(HG_SUB):
                            a0 = u * tr + (r0 + s) * HG_SUB
                            oi = oi + a_ref[a0:a0 + HG_SUB, :] * v[r0 + s:r0 + s + 1]
                        i_ref[u, r0:r0 + HG_SUB, :] = oi

        for ci in range(n_chunk):
            rs = pl.ds(ci * HG_CHUNK, HG_CHUNK)
            hs = [(it * HG_UNROLL_PROMPT + hh, hh * n_chunk + ci) for hh in range(HG_UNROLL_PROMPT)]
            inter = [_dot_nt((q_ref[h, rs, :] * jnp.exp(b_ref[u])).astype(BF16), st_ref[h].astype(BF16))
                     for h, u in hs]
            grow = []
            for h, u in hs:
                b = b_ref[u]
                kd = (k_ref[h, rs, :] * jnp.exp(b[HG_CHUNK - 1:HG_CHUNK] - b)).astype(BF16)
                grow.append(lax.dot_general(v_ref[h, rs, :].astype(BF16), kd, (((0,), (0,)), ((), ())),
                                            preferred_element_type=F32))
            for (h, u), oi, gr in zip(hs, inter, grow):
                o_ref[h, rs, :] = _hg_out(i_ref[u] + oi, ng_ref[...], gs_ref[h, rs, :])
                st_ref[h] = st_ref[h] * jnp.exp(b_ref[u, HG_CHUNK - 1:HG_CHUNK, :]) + gr

        @pl.when(c == pl.num_programs(1) - 1)
        def _():
            for hh in range(HG_UNROLL_PROMPT):
                h = it * HG_UNROLL_PROMPT + hh
                s_ref[h] = st_ref[h].T
        return carry

    lax.fori_loop(0, HG_HEADS // HG_UNROLL_PROMPT, heads, 0)


def _hg_scan_prompt(q, f, k, v, gs, ng, batch, t_seq):
    m = q.shape[1]
    rows = 2 * HG_CHUNK
    nc = t_seq // rows
    units = HG_UNROLL_PROMPT * (rows // HG_CHUNK)
    blk = pl.BlockSpec((HG_HEADS, rows, HG_DK), lambda b, c: (0, b * nc + c, 0))
    return pl.pallas_call(
        _hg_scan_prompt_kernel,
        grid=(batch, nc),
        in_specs=[blk] * 5 + [pl.BlockSpec((1, HG_DV), lambda b, c: (0, 0))],
        out_specs=[blk, pl.BlockSpec((None, HG_HEADS, HG_DK, HG_DV), lambda b, c: (b, 0, 0, 0))],
        out_shape=[jax.ShapeDtypeStruct((HG_HEADS, m, HG_DV), BF16),
                   jax.ShapeDtypeStruct((batch, HG_HEADS, HG_DK, HG_DV), F32)],
        scratch_shapes=[pltpu.VMEM((HG_HEADS, HG_DV, HG_DK), F32),
                        pltpu.VMEM((units, HG_CHUNK, HG_DK), F32),
                        pltpu.VMEM((units * HG_CHUNK * HG_SUB, HG_DK), BF16),
                        pltpu.VMEM((units * HG_CHUNK * HG_SUB, HG_DV), F32),
                        pltpu.VMEM((units, HG_CHUNK, HG_DV), F32)],
        compiler_params=_cparams("parallel", "arbitrary"),
        name="hgrn_scan_prompt",
    )(q, f, k, v, gs, ng)


def _hg_step_sample_kernel(q_ref, f_ref, k_ref, v_ref, gs_ref, ng_ref, s0_ref, *rest, t_new):
    o_ref, s1_ref = rest[-2:]
    n_seq = s0_ref.shape[0]

    ones = jnp.ones((HG_DK, HG_DV), BF16)

    def body(idx, carry):
        per_seq = HG_HEADS // HG_UNROLL_SAMPLE
        sq = idx // per_seq
        rs = pl.ds(pl.multiple_of(sq * t_new, t_new), t_new)
        hs = [(idx % per_seq) * HG_UNROLL_SAMPLE + hh for hh in range(HG_UNROLL_SAMPLE)]
        bs = [_cumsum_rows(f_ref[h, rs, :]) for h in hs]
        inter = [_dot((q_ref[h, rs, :] * jnp.exp(b)).astype(BF16), s0_ref[sq, h].astype(BF16))
                 for h, b in zip(hs, bs)]
        sums = [_dot(_hg_diag_terms(q_ref[h, rs, :], k_ref[h, rs, :], b, t_new), ones) for h, b in zip(hs, bs)]
        exts = []
        for h, b in zip(hs, bs):
            bl = b[t_new - 1:t_new]
            kd = k_ref[h, rs, :] * jnp.exp(bl - b)
            exts.append(jnp.concatenate([kd, jnp.broadcast_to(jnp.exp(bl), (t_new, HG_DK)),
                                         jnp.zeros((HG_DK - 2 * t_new, HG_DK), F32)], axis=0).T)
        grow = [_dot(ext[:, :t_new].astype(BF16), v_ref[h, rs, :].astype(BF16)) for h, ext in zip(hs, exts)]
        for h, oi, a, ext, gr in zip(hs, inter, sums, exts, grow):
            v = v_ref[h, rs, :]
            for s in range(t_new):
                oi = oi + a[s * t_new:(s + 1) * t_new] * v[s:s + 1]
            o_ref[h, rs, :] = _hg_out(oi, ng_ref[...], gs_ref[h, rs, :])
            s1_ref[sq, h] = s0_ref[sq, h] * ext[:, t_new:t_new + 1] + gr
        return carry

    lax.fori_loop(0, n_seq * (HG_HEADS // HG_UNROLL_SAMPLE), body, 0)


def _hg_step_sample(q, f, k, v, gs, ng, s0_all, layer, t_new, prev_state):
    m = q.shape[1]
    dec_b = s0_all.shape[1]
    sb = min(8, dec_b)
    blk = pl.BlockSpec((HG_HEADS, sb * t_new, HG_DK), lambda i: (0, i, 0))
    st = pl.BlockSpec((None, sb, HG_HEADS, HG_DK, HG_DV), lambda i: (layer, i, 0, 0, 0))
    prev_spec, prev_args = _aliased_prev(prev_state)
    args = (q, f, k, v, gs, ng, s0_all) + prev_args
    return pl.pallas_call(
        functools.partial(_hg_step_sample_kernel, t_new=t_new),
        grid=(dec_b // sb,),
        in_specs=[blk] * 5 + [pl.BlockSpec((1, HG_DV), lambda i: (0, 0)), st] + prev_spec,
        out_specs=[blk, st],
        out_shape=[jax.ShapeDtypeStruct((HG_HEADS, m, HG_DV), BF16), jax.ShapeDtypeStruct(s0_all.shape, F32)],
        input_output_aliases={len(args) - 1: 1} if prev_args else {},
        compiler_params=_cparams("parallel"),
        name="hgrn_step_sample",
    )(*args)


def _feature_major_view(a):
    lead = a.shape[:-4]
    t, c, g, d = a.shape[-4:]
    n = len(lead)
    return jnp.transpose(a, tuple(range(n)) + (n + 1, n + 2, n + 3, n)).reshape(lead + (c * g * d, t))


def _token_major_view(a, c):
    lead = a.shape[:-2]
    t = a.shape[-1]
    n = len(lead)
    a = a.reshape(lead + (c, KV_HEADS, HEAD_DIM, t))
    return jnp.transpose(a, tuple(range(n)) + (n + 3, n, n + 1, n + 2))


def kernel(x_prompt, x_sample, cache_nsa_kv, state_nsa_win, state_hgrn, page_table, p_prompt, p_sample, norm_mix, norm_mlp, norm_ple, norm_final, nsa_w_in, nsa_cmp_pe, nsa_cmp_w1, nsa_cmp_w2, nsa_w_out, hg_w_in, hg_lb_logits, hg_norm, hg_w_out, mlp_w1, mlp_w2, ple_w_proj, ple_w_gate):
    batch, t_p, d = x_prompt.shape
    dec_b, t_s, _ = x_sample.shape
    depth = p_prompt.shape[0]
    page = cache_nsa_kv.shape[2]
    past = page_table.shape[1] * page

    xp = x_prompt.reshape(batch * t_p, d)
    xs = x_sample.reshape(dec_b * t_s, d)
    pp = p_prompt.reshape(depth, batch * t_p, -1)
    ps = p_sample.reshape(depth, dec_b * t_s, -1)
    row = lambda a: a.reshape(1, -1)
    cache_t = _feature_major_view(cache_nsa_kv)
    win_t = _feature_major_view(state_nsa_win)
    wl = min(WINDOW, t_p)

    tabs_p = _rope_tables(t_p, 0, t_p)
    tabs_s = _rope_tables(min(ROW_TILE, dec_b * t_s), past, t_s)

    kv_s, win_p, st_p = [], [], []
    kv_p_all = win_s_all = st_s_all = None
    n_attn = nsa_w_in.shape[0]
    for i in range(depth):
        g_mix = row(norm_mix[i])
        if i % 2 == 0:
            a = i // 2
            n_in = nsa_w_in.shape[2]
            w_t = jnp.pad(jnp.transpose(nsa_w_in[a]).astype(BF16),
                          ((0, Q_DIM + 6 * KV_DIM + LANES - n_in), (0, 0)))
            pe2 = jnp.tile(nsa_cmp_pe[a], (1, 1, LANES // HEAD_DIM))
            cw1 = nsa_cmp_w1[a].reshape(2, CMP_BLOCK * HEAD_DIM, -1).astype(BF16)
            cw2t = jnp.swapaxes(nsa_cmp_w2[a], 1, 2).astype(BF16)
            w_out = nsa_w_out[a].astype(BF16)

            qt, gatest, cmp_rows, kv_p_all, wrows_t, ksa, vsa, kwa, vwa = _nsa_inproj_prompt(
                xp, g_mix, tabs_p, w_t, batch, t_p, a, n_attn, kv_p_all)
            kc, vct = _compress_prompt(cmp_rows, pe2, cw1, cw2t, batch, t_p)
            op = _nsa_attn_prompt(qt, gatest, kc, vct, ksa, vsa, kwa, vwa, batch, t_p)
            win_p.append(wrows_t[:, :, t_p - wl:])

            q, gates, rows, wrows = _nsa_inproj_sample(xs, g_mix, tabs_s, w_t)
            (kc, vc), n_cmp = _compress_sample(page_table, cache_t, a, rows, pe2, cw1, cw2t, t_s)
            os_, win_s_all = _nsa_attn_sample(page_table, cache_t, win_t, a, kc, vc, q, rows, wrows, gates, t_s,
                                              n_cmp, win_s_all)
            kv_s.append(rows.reshape(dec_b, t_s, 4, KV_HEADS, HEAD_DIM))
        else:
            r = i // 2
            w_in = hg_w_in[r].astype(BF16)
            w_out = hg_w_out[r].astype(BF16)
            ng = row(hg_norm[r])
            q, f, k, v, gs = _hg_inproj(xp, g_mix, hg_lb_logits, w_in, r)
            op, s_new = _hg_scan_prompt(q, f, k, v, gs, ng, batch, t_p)
            st_p.append(s_new)
            q, f, k, v, gs = _hg_inproj(xs, g_mix, hg_lb_logits, w_in, r)
            os_, st_s_all = _hg_step_sample(q, f, k, v, gs, ng, state_hgrn, r, t_s, st_s_all)

        w1 = mlp_w1[i].astype(BF16)
        w2 = mlp_w2[i].astype(BF16)
        wgate = ple_w_gate[i].astype(BF16)
        wproj = ple_w_proj[i].astype(BF16)
        last = i == depth - 1
        post = (w_out, row(norm_mlp[i]), w1, w2)
        ple = (i, row(norm_ple[i]), wgate, wproj, row(norm_final), last)
        xp = _post_mixer(xp, op, *post, pp, *ple)
        xs = _post_mixer(xs, os_, *post, ps, *ple)

    return (xp.reshape(batch, t_p, d), xs.reshape(dec_b, t_s, d),
            _token_major_view(kv_p_all, 4), jnp.stack(kv_s),
            _token_major_view(jnp.stack(win_p), 2), _token_major_view(win_s_all, 2),
            jnp.stack(st_p), st_s_all)
```
